```python
import math
import jax
import jax.numpy as jnp
from jax import lax
import numpy as np

D_MODEL = 1024
BATCH = 32
SEQ = 256
DEPTH = 4
DEC_BATCH = 4
DEC_SEQ = 2048
PAST_LEN = 256

GRID_W = 64
SSD_EXPAND = 2
D_SSD = SSD_EXPAND * D_MODEL
SSD_HEADDIM = 64
SSD_HEADS = D_SSD // SSD_HEADDIM
SSD_GROUPS = 8
D_STATE = 128
D_CONV = 5
SSD_CHUNK = 128
CONV_DIM = D_SSD + 2 * SSD_GROUPS * D_STATE
HGRN_HEADS = 8
HGRN_KDIM = 128
HGRN_FDIM = HGRN_HEADS * HGRN_KDIM
HGRN_WIDTH = D_MODEL
HGRN_VDIM = HGRN_WIDTH // HGRN_HEADS
HGRN_CHUNK = 16
N_BRANCHES = 2
IN_COLS = CONV_DIM + D_SSD + 2 * SSD_HEADS + HGRN_FDIM + 2 * HGRN_FDIM + 2 * HGRN_WIDTH + N_BRANCHES * D_MODEL
FFN_HIDDEN = ((8 * D_MODEL + 767) // 768) * 256
RMS_EPS = 1e-6

kernel_name = "hybrid_ssd_hgrn2_diffusion_step"


def _rmsnorm(x, w):
    x32 = x.astype(jnp.float32)
    y = x32 * lax.rsqrt(jnp.mean(x32 * x32, axis=-1, keepdims=True) + RMS_EPS)
    return (y * w.astype(jnp.float32)).astype(x.dtype)


def _centred_dwconv_silu(u, w, b):
    length = u.shape[-2]
    pad = D_CONV // 2
    up = jnp.pad(u, [(0, 0)] * (u.ndim - 2) + [(pad, pad), (0, 0)])
    out = b
    for k in range(D_CONV):
        out = out + up[..., k:k + length, :] * w[k]
    return jax.nn.silu(out)


def _carry_chunk_states(chunk_decay, chunk_contrib, s0):
    def step(s, inp):
        d, u = inp
        return d * s + u, s
    s_final, starts = lax.scan(step, s0, (jnp.moveaxis(chunk_decay, 1, 0), jnp.moveaxis(chunk_contrib, 1, 0)))
    return jnp.moveaxis(starts, 0, 1), s_final


def _ssd_scan(x, dt, a_neg, bmat, cmat, s0):
    f32 = jnp.float32
    bt, length, n_heads, hd = x.shape
    g, n = bmat.shape[2], bmat.shape[3]
    r = n_heads // g
    q = SSD_CHUNK
    nc = length // q
    xg = x.astype(f32).reshape(bt, nc, q, g, r, hd)
    dtg = dt.astype(f32).reshape(bt, nc, q, g, r)
    bg = bmat.astype(f32).reshape(bt, nc, q, g, n)
    cg = cmat.astype(f32).reshape(bt, nc, q, g, n)
    cs = jnp.cumsum(dtg * a_neg.astype(f32).reshape(g, r), axis=2)
    causal = jnp.tril(jnp.ones((q, q), dtype=bool))[:, :, None, None]
    decay = jnp.exp(jnp.where(causal, cs[:, :, :, None] - cs[:, :, None, :], -jnp.inf))
    cb = jnp.einsum("bcign,bcjgn->bcijg", cg, bg)
    xdt = dtg[..., None] * xg
    y_diag = jnp.einsum("bcijgr,bcjgrp->bcigrp", cb[..., None] * decay, xdt)
    to_end = jnp.exp(cs[:, :, -1:] - cs)
    contrib = jnp.einsum("bcjgn,bcjgrp->bcgrpn", bg, to_end[..., None] * xdt)
    chunk_decay = jnp.exp(cs[:, :, -1])[..., None, None]
    starts, s_fin = _carry_chunk_states(chunk_decay, contrib, s0.astype(f32).reshape(bt, g, r, hd, n))
    y_off = jnp.einsum("bcign,bcgrpn->bcigrp", cg, starts) * jnp.exp(cs)[..., None]
    y = (y_diag + y_off).reshape(bt, length, n_heads, hd)
    return y.astype(x.dtype), s_fin.reshape(bt, n_heads, hd, n).astype(x.dtype)


def _hgrn2_scan(q, log_f, k, v, s0):
    f32 = jnp.float32
    bt, length, n_heads, kd = q.shape
    vd = v.shape[-1]
    c = HGRN_CHUNK
    nc = length // c
    qc = q.astype(f32).reshape(bt, nc, c, n_heads, kd)
    kc = k.astype(f32).reshape(bt, nc, c, n_heads, kd)
    vc = v.astype(f32).reshape(bt, nc, c, n_heads, vd)
    b = jnp.cumsum(log_f.astype(f32).reshape(bt, nc, c, n_heads, kd), axis=2)
    causal = jnp.tril(jnp.ones((c, c), dtype=bool))[:, :, None, None]
    decay = jnp.exp(jnp.where(causal, b[:, :, :, None] - b[:, :, None, :], -jnp.inf))
    attn = jnp.einsum("bcihk,bcijhk,bcjhk->bcijh", qc, decay, kc)
    o_intra = jnp.einsum("bcijh,bcjhv->bcihv", attn, vc)
    contrib = jnp.einsum("bcjhk,bcjhv->bchkv", kc * jnp.exp(b[:, :, -1:] - b), vc)
    chunk_decay = jnp.exp(b[:, :, -1])[..., None]
    starts, s_fin = _carry_chunk_states(chunk_decay, contrib, s0.astype(f32))
    o_off = jnp.einsum("bcihk,bchkv->bcihv", qc * jnp.exp(b), starts)
    o = (o_intra + o_off).reshape(bt, length, n_heads, vd)
    return o.astype(q.dtype), s_fin.astype(q.dtype)


def _hybrid_mixer(h, lw, lb, s_ssm0, s_hgrn0, latent):
    f32 = jnp.float32
    bt, length, _ = h.shape
    proj = h @ lw["w_in"]
    cuts = np.cumsum([CONV_DIM, D_SSD, 2 * SSD_HEADS, HGRN_FDIM, 2 * HGRN_FDIM, HGRN_WIDTH, HGRN_WIDTH]).tolist()
    xbc, z, dt_raw, hq, hf, hi, hg, gate_logits = jnp.split(proj, cuts, axis=-1)

    if latent:
        rows = length // GRID_W
        xbc = _centred_dwconv_silu(xbc.reshape(bt, rows, GRID_W, CONV_DIM), lw["ssd_conv_w"], lw["ssd_conv_b"]).reshape(bt, length, CONV_DIM)
    else:
        xbc = _centred_dwconv_silu(xbc, lw["ssd_conv_w"], lw["ssd_conv_b"])
    xs, bm, cm = jnp.split(xbc, [D_SSD, D_SSD + SSD_GROUPS * D_STATE], axis=-1)
    xs = xs.reshape(bt, length, SSD_HEADS, SSD_HEADDIM)
    bm = bm.reshape(bt, length, SSD_GROUPS, D_STATE)
    cm = cm.reshape(bt, length, SSD_GROUPS, D_STATE)
    dt = jax.nn.softplus(dt_raw.astype(f32).reshape(bt, length, 2, SSD_HEADS) + lw["ssd_dt_bias"].astype(f32))
    a_neg = -jnp.exp(lw["ssd_a_log"].astype(f32))
    y_f, sf = _ssd_scan(xs, dt[:, :, 0], a_neg[0], bm, cm, s_ssm0[:, 0])
    y_b, sb = _ssd_scan(xs[:, ::-1], dt[:, ::-1, 1], a_neg[1], bm[:, ::-1], cm[:, ::-1], s_ssm0[:, 1])
    y = y_f + y_b[:, ::-1] + lw["ssd_d"][:, None] * xs
    y = y.reshape(bt, length, D_SSD) * jax.nn.silu(z)
    y = _rmsnorm(y.reshape(bt, length, SSD_GROUPS, D_SSD // SSD_GROUPS), lw["ssd_norm_w"].reshape(SSD_GROUPS, -1)).reshape(bt, length, D_SSD)
    y_ssd = y @ lw["w_ssd_out"]

    q = jax.nn.silu(hq).reshape(bt, length, HGRN_HEADS, HGRN_KDIM)
    v = hi.reshape(bt, length, HGRN_HEADS, HGRN_VDIM)
    u = hf.astype(f32).reshape(bt, length, 2, HGRN_HEADS, HGRN_KDIM)
    lb3 = lb.reshape(2, HGRN_HEADS, HGRN_KDIM)
    log_f = jnp.logaddexp(jnp.log(lb3), jnp.log1p(-lb3) + jax.nn.log_sigmoid(u))
    k = -jnp.expm1(log_f)
    o_f, hf_state = _hgrn2_scan(q, log_f[:, :, 0], k[:, :, 0], v, s_hgrn0[:, 0])
    o_b, hb_state = _hgrn2_scan(q[:, ::-1], log_f[:, ::-1, 1], k[:, ::-1, 1], v[:, ::-1], s_hgrn0[:, 1])
    o = o_f + o_b[:, ::-1]
    o = _rmsnorm(o, lw["hgrn_norm_w"].reshape(HGRN_HEADS, HGRN_VDIM)).reshape(bt, length, HGRN_WIDTH) * jax.nn.silu(hg)
    y_hgrn = o @ lw["w_hgrn_out"]

    g_ssd, g_hgrn = jnp.split(jax.nn.sigmoid(gate_logits), N_BRANCHES, axis=-1)
    out = (g_ssd * y_ssd + g_hgrn * y_hgrn) @ lw["w_out"]
    return out, jnp.stack([sf, sb], axis=1), jnp.stack([hf_state, hb_state], axis=1)


def _layer(x, mod, lw, lb, s_ssm0, s_hgrn0, latent):
    sh_m, sc_m, g_m, sh_f, sc_f, g_f = jnp.split(mod, 6, axis=-1)
    h = _rmsnorm(x, lw["norm_mix_w"]) * (1 + sc_m) + sh_m
    mix, s_ssm, s_hgrn = _hybrid_mixer(h, lw, lb, s_ssm0, s_hgrn0, latent)
    x = x + g_m * mix
    h = _rmsnorm(x, lw["norm_ffn_w"]) * (1 + sc_f) + sh_f
    ffn = (jax.nn.silu(h @ lw["ffn_w1"]) * (h @ lw["ffn_w3"])) @ lw["ffn_w2"]
    x = x + g_f * ffn
    return x, s_ssm, s_hgrn


def setup_inputs(seed: int = 0) -> dict:
    key = jax.random.key(seed)
    ks = jax.random.split(key, 26)
    f32 = jnp.float32

    def nrm(k, shape, scale):
        return jax.random.normal(k, shape, f32) * scale

    dt0 = jnp.exp(jax.random.uniform(ks[13], (DEPTH, 2, SSD_HEADS), f32, math.log(1e-3), math.log(1e-1)))
    return {
        "x_prompt": nrm(ks[0], (BATCH, SEQ, D_MODEL), 1.0),
        "x_sample": nrm(ks[1], (DEC_BATCH, DEC_SEQ, D_MODEL), 1.0),
        "c": nrm(ks[2], (DEC_BATCH, D_MODEL), 1.0),
        "state_ssm": nrm(ks[3], (DEC_BATCH, DEPTH, 2, SSD_HEADS, SSD_HEADDIM, D_STATE), 0.3),
        "state_hgrn": nrm(ks[4], (DEC_BATCH, DEPTH, 2, HGRN_HEADS, HGRN_KDIM, HGRN_VDIM), 0.5),
        "c_ctx": nrm(ks[5], (D_MODEL,), 1.0),
        "norm_mix_w": 1.0 + nrm(ks[6], (DEPTH, D_MODEL), 0.02),
        "norm_ffn_w": 1.0 + nrm(ks[7], (DEPTH, D_MODEL), 0.02),
        "ada_w": nrm(ks[8], (DEPTH, D_MODEL, 6 * D_MODEL), 0.5 * D_MODEL ** -0.5),
        "ada_b": nrm(ks[9], (DEPTH, 6 * D_MODEL), 0.02),
        "w_in": nrm(ks[10], (DEPTH, D_MODEL, IN_COLS), D_MODEL ** -0.5),
        "ssd_conv_w": nrm(ks[11], (DEPTH, D_CONV, CONV_DIM), D_CONV ** -0.5),
        "ssd_conv_b": nrm(ks[12], (DEPTH, CONV_DIM), 0.02),
        "ssd_dt_bias": dt0 + jnp.log(-jnp.expm1(-dt0)),
        "ssd_a_log": jnp.log(jax.random.uniform(ks[14], (DEPTH, 2, SSD_HEADS), f32, 1.0, 16.0)),
        "ssd_d": 1.0 + nrm(ks[15], (DEPTH, SSD_HEADS), 0.1),
        "ssd_norm_w": 1.0 + nrm(ks[16], (DEPTH, D_SSD), 0.02),
        "hgrn_lower_bounds": nrm(ks[17], (DEPTH, 2, HGRN_FDIM), 1.0),
        "hgrn_norm_w": 1.0 + nrm(ks[18], (DEPTH, HGRN_WIDTH), 0.02),
        "w_ssd_out": nrm(ks[19], (DEPTH, D_SSD, D_MODEL), D_SSD ** -0.5),
        "w_hgrn_out": nrm(ks[20], (DEPTH, HGRN_WIDTH, D_MODEL), HGRN_WIDTH ** -0.5),
        "w_out": nrm(ks[21], (DEPTH, D_MODEL, D_MODEL), D_MODEL ** -0.5),
        "ffn_w1": nrm(ks[22], (DEPTH, D_MODEL, FFN_HIDDEN), D_MODEL ** -0.5),
        "ffn_w3": nrm(ks[23], (DEPTH, D_MODEL, FFN_HIDDEN), D_MODEL ** -0.5),
        "ffn_w2": nrm(ks[24], (DEPTH, FFN_HIDDEN, D_MODEL), FFN_HIDDEN ** -0.5),
        "norm_final_w": 1.0 + nrm(ks[25], (D_MODEL,), 0.02),
    }


def reference(x_prompt, x_sample, c, state_ssm, state_hgrn, c_ctx, norm_mix_w, norm_ffn_w, ada_w, ada_b, w_in,
              ssd_conv_w, ssd_conv_b, ssd_dt_bias, ssd_a_log, ssd_d, ssd_norm_w, hgrn_lower_bounds, hgrn_norm_w,
              w_ssd_out, w_hgrn_out, w_out, ffn_w1, ffn_w3, ffn_w2, norm_final_w):
    lb_all = jnp.cumsum(jax.nn.softmax(hgrn_lower_bounds.astype(jnp.float32), axis=0), axis=0)
    lb_all = lb_all - lb_all[:1]
    bp = x_prompt.shape[0]
    zero_ssm = jnp.zeros((bp, 2, SSD_HEADS, SSD_HEADDIM, D_STATE), x_prompt.dtype)
    zero_hgrn = jnp.zeros((bp, 2, HGRN_HEADS, HGRN_KDIM, HGRN_VDIM), x_prompt.dtype)
    ctx, lat = x_prompt, x_sample
    ssm_states, hgrn_states = [], []
    for l in range(DEPTH):
        lw = {
            "norm_mix_w": norm_mix_w[l], "norm_ffn_w": norm_ffn_w[l], "w_in": w_in[l],
            "ssd_conv_w": ssd_conv_w[l], "ssd_conv_b": ssd_conv_b[l], "ssd_dt_bias": ssd_dt_bias[l],
            "ssd_a_log": ssd_a_log[l], "ssd_d": ssd_d[l], "ssd_norm_w": ssd_norm_w[l],
            "hgrn_norm_w": hgrn_norm_w[l], "w_ssd_out": w_ssd_out[l], "w_hgrn_out": w_hgrn_out[l],
            "w_out": w_out[l], "ffn_w1": ffn_w1[l], "ffn_w3": ffn_w3[l], "ffn_w2": ffn_w2[l],
        }
        mod_ctx = (jax.nn.silu(c_ctx) @ ada_w[l] + ada_b[l])[None, None, :]
        mod_lat = (jax.nn.silu(c) @ ada_w[l] + ada_b[l])[:, None, :]
        ctx, s_ssm, s_hgrn = _layer(ctx, mod_ctx, lw, lb_all[l], zero_ssm, zero_hgrn, False)
        lat, _, _ = _layer(lat, mod_lat, lw, lb_all[l], state_ssm[:, l], state_hgrn[:, l], True)
        ssm_states.append(s_ssm)
        hgrn_states.append(s_hgrn)
    y_prompt = _rmsnorm(ctx, norm_final_w)
    y_sample = _rmsnorm(lat, norm_final_w)
    new_state_ssm = jnp.stack(ssm_states, axis=1)
    new_state_hgrn = jnp.stack(hgrn_states, axis=1)
    return (y_prompt, y_sample, new_state_ssm, new_state_hgrn)
```

```python
import functools
from typing import NamedTuple

import jax
import jax.numpy as jnp
from jax import lax
from jax.experimental import pallas as pl
from jax.experimental.pallas import tpu as pltpu

F32 = jnp.float32
BF16 = jnp.bfloat16

LANE = 128
VMEM_BYTES = 64 * 1024 * 1024
RMS_EPS = 1e-6
D_CONV = 5
GRID_W = 64
SSD_P = 64
SSD_N = 128
SSD_Q = 128
SSD_R = 4
HG_K = 128
HG_V = 128
HG_C = 128
HG_LEVELS = (64, 32, 16)
HG_DIAG = 16
MOD_ROWS = 8


class Dims(NamedTuple):
    d: int
    nb: int
    nl: int
    db: int
    dl: int
    depth: int
    ffn: int

    @property
    def t_ctx(self):
        return self.nb * self.nl

    @property
    def t_lat(self):
        return self.db * self.dl

    @property
    def t(self):
        return self.t_ctx + self.t_lat

    @property
    def d_ssd(self):
        return 2 * self.d

    @property
    def h_ssd(self):
        return self.d_ssd // SSD_P

    @property
    def groups(self):
        return self.h_ssd // SSD_R

    @property
    def hh(self):
        return self.d // HG_K

    @property
    def row_tile(self):
        tm = min(1024, self.dl, self.t_ctx)
        assert self.dl % tm == 0 and self.t_ctx % tm == 0 and tm % self.nl == 0 and tm % GRID_W == 0
        return tm


def _cparams(semantics, vmem_bytes):
    return pltpu.CompilerParams(dimension_semantics=semantics, vmem_limit_bytes=min(vmem_bytes, VMEM_BYTES - (4 << 20)))


def _dot(a, b):
    return jnp.dot(a, b, preferred_element_type=F32)


def _dot_nt(a, b):
    return lax.dot_general(a, b, (((1,), (1,)), ((), ())), preferred_element_type=F32)


def _dot_tn(a, b):
    return lax.dot_general(a, b, (((0,), (0,)), ((), ())), preferred_element_type=F32)


def _split3(x):
    hi = x.astype(BF16)
    r = x - hi.astype(F32)
    mid = r.astype(BF16)
    lo = (r - mid.astype(F32)).astype(BF16)
    return hi, mid, lo


def _sum01_left(m01, x):
    hi, mid, lo = _split3(x)
    return _dot(m01, hi) + _dot(m01, mid) + _dot(m01, lo)


def _sum01_right(x, m01):
    hi, mid, lo = _split3(x)
    return _dot(hi, m01) + _dot(mid, m01) + _dot(lo, m01)


def _silu(x):
    return x * jax.nn.sigmoid(x)


def _softplus(x):
    return jnp.maximum(x, 0.0) + jnp.log1p(jnp.exp(-jnp.abs(x)))


def _mod_row(i, tm, dims):
    nctx = dims.t_ctx // tm
    return jnp.where(i < nctx, 0, 1 + ((i - nctx) * tm) // dims.dl)


def _mod_kernel(c_ref, w_ref, b_ref, o_ref):
    s = _silu(c_ref[...])
    o_ref[0] = _dot(s.astype(BF16), w_ref[0].astype(BF16)) + b_ref[0]


def _mod_call(dims, cvec, ada_w, ada_b):
    d, n = dims.d, 6 * dims.d
    tn = n // 4
    return pl.pallas_call(
        _mod_kernel,
        grid=(dims.depth, n // tn),
        in_specs=[pl.BlockSpec((MOD_ROWS, d), lambda l, j: (0, 0)),
                  pl.BlockSpec((1, d, tn), lambda l, j: (l, 0, j)),
                  pl.BlockSpec((1, 1, tn), lambda l, j: (l, 0, j))],
        out_specs=pl.BlockSpec((1, MOD_ROWS, tn), lambda l, j: (l, 0, j)),
        out_shape=jax.ShapeDtypeStruct((dims.depth, MOD_ROWS, n), F32),
        compiler_params=_cparams(("arbitrary", "arbitrary"), 2 * d * tn * 4 + (8 << 20)),
        name="adaln_mod",
    )(cvec, ada_w, ada_b.reshape(dims.depth, 1, n))


def _inproj_kernel(dims, tm, x_ref, mod_ref, nw_ref, w_ref, cw_ref, cb_ref, wdt_ref, dtb_ref, alog_ref,
                   proj_ref, dt_ref, da_ref, h_s):
    d = dims.d
    i = pl.program_id(0)
    j = pl.program_id(1)
    n_conv = (dims.d_ssd + 2 * dims.groups * SSD_N) // d

    @pl.when(j == 0)
    def _():
        x = x_ref[...]
        y = x * lax.rsqrt(jnp.mean(x * x, axis=-1, keepdims=True) + RMS_EPS) * nw_ref[...]
        mod = mod_ref[0]
        hb = (y * (1.0 + mod[:, d:2 * d]) + mod[:, 0:d]).astype(BF16)
        h_s[...] = hb
        dt = _softplus(_dot_nt(wdt_ref[...], hb) + dtb_ref[...])
        da = dt * -jnp.exp(alog_ref[...])
        for k in range(tm // SSD_Q):
            dt_ref[k] = dt[:, k * SSD_Q:(k + 1) * SSD_Q]
            da_ref[k] = da[:, k * SSD_Q:(k + 1) * SSD_Q]

    res = _dot(h_s[...], w_ref[...])

    @pl.when(j >= n_conv)
    def _():
        proj_ref[...] = res

    @pl.when(j < n_conv)
    def _():
        seg = jnp.where(i < dims.t_ctx // tm, dims.nl, GRID_W)
        pos = lax.broadcasted_iota(jnp.int32, (tm, LANE), 0) & (seg - 1)
        offs = [k - D_CONV // 2 for k in range(D_CONV) if k != D_CONV // 2]
        oks = {o: (pos + o >= 0) & (pos + o < seg) for o in offs}
        for sl in range(res.shape[1] // LANE):
            cols = slice(sl * LANE, (sl + 1) * LANE)
            u = res[:, cols]
            acc = cb_ref[:, cols] + cw_ref[D_CONV // 2:D_CONV // 2 + 1, cols] * u
            for o in offs:
                shifted = pltpu.roll(u, (-o) % tm, 0)
                acc = acc + cw_ref[o + D_CONV // 2:o + D_CONV // 2 + 1, cols] * jnp.where(oks[o], shifted, 0.0)
            proj_ref[:, cols] = _silu(acc)


def _inproj_call(dims, x, mod_l, nw, w, cw, cb, wdt_t, dtb, alog):
    d, t, tm = dims.d, dims.t, dims.row_tile
    ncol = w.shape[1]
    n_conv = (dims.d_ssd + 2 * dims.groups * SSD_N) // d
    nchunk = t // SSD_Q
    out_shapes = (jax.ShapeDtypeStruct((t, ncol), F32),
                  jax.ShapeDtypeStruct((nchunk, LANE, SSD_Q), F32),
                  jax.ShapeDtypeStruct((nchunk, LANE, SSD_Q), F32))
    kc = tm // SSD_Q
    return pl.pallas_call(
        functools.partial(_inproj_kernel, dims, tm),
        grid=(t // tm, ncol // d),
        in_specs=[pl.BlockSpec((tm, d), lambda i, j: (i, 0)),
                  pl.BlockSpec((1, 1, 6 * d), lambda i, j: (_mod_row(i, tm, dims), 0, 0)),
                  pl.BlockSpec((1, d), lambda i, j: (0, 0)),
                  pl.BlockSpec((d, d), lambda i, j: (0, j)),
                  pl.BlockSpec((D_CONV, d), lambda i, j: (0, jnp.minimum(j, n_conv - 1))),
                  pl.BlockSpec((1, d), lambda i, j: (0, jnp.minimum(j, n_conv - 1))),
                  pl.BlockSpec((LANE, d), lambda i, j: (0, 0)),
                  pl.BlockSpec((LANE, 1), lambda i, j: (0, 0)),
                  pl.BlockSpec((LANE, 1), lambda i, j: (0, 0))],
        out_specs=(pl.BlockSpec((tm, d), lambda i, j: (i, j)),
                   pl.BlockSpec((kc, LANE, SSD_Q), lambda i, j: (i, 0, 0)),
                   pl.BlockSpec((kc, LANE, SSD_Q), lambda i, j: (i, 0, 0))),
        out_shape=out_shapes,
        scratch_shapes=[pltpu.VMEM((tm, d), BF16)],
        compiler_params=_cparams(("arbitrary", "arbitrary"), 6 * tm * d * 4 + 2 * d * d * 2 + tm * d * 2 + (8 << 20)),
        name="inproj",
    )(x, mod_l, nw, w, cw, cb, wdt_t, dtb, alog)


def _ssd_kernel(dims, nchunk, has_init, *refs):
    if has_init:
        (x_ref, b_ref, c_ref, z_ref, dt_ref, da_ref, dsk_ref, nw_ref, s0_ref,
         y_ref, sfin_ref, st, yf_s, xt_s, yb_s, bg_s, cg_s, cs_s, ecs_s, wend_s, dec_s, dtr_s) = refs
    else:
        (x_ref, b_ref, c_ref, z_ref, dt_ref, da_ref, dsk_ref, nw_ref,
         y_ref, sfin_ref, st, yf_s, xt_s, yb_s, bg_s, cg_s, cs_s, ecs_s, wend_s, dec_s, dtr_s) = refs
    h, g_n = dims.h_ssd, dims.groups
    q = SSD_Q
    s = pl.program_id(1)
    is_bwd = s >= nchunk
    ck = jnp.where(is_bwd, 2 * nchunk - 1 - s, s)
    first = (s == 0) | (s == nchunk)
    last = (s == nchunk - 1) | (s == 2 * nchunk - 1)

    @pl.when(first)
    def _():
        st[...] = s0_ref[0, 0] if has_init else jnp.zeros(st.shape, F32)

    row0 = pl.multiple_of(jnp.where(is_bwd, h, 0), 8)
    dtr = dt_ref[0, pl.ds(row0, h), :]
    dar = da_ref[0, pl.ds(row0, h), :]
    jj = lax.broadcasted_iota(jnp.int32, (q, q), 0)
    ii = lax.broadcasted_iota(jnp.int32, (q, q), 1)
    seen = jnp.where(is_bwd, jj - ii, ii - jj) >= 0
    cs = _sum01_right(dar, seen.astype(BF16))
    tot = _sum01_right(dar, jnp.ones((q, q), BF16))
    cs_s[...] = cs
    ecs_s[...] = jnp.exp(cs)
    wend_s[...] = jnp.exp(tot - cs) * dtr
    dec_s[...] = jnp.exp(tot)
    dtr_s[...] = dtr
    xt_s[...] = x_ref[...].T
    for g in range(g_n):
        bg_s[g] = b_ref[:, g * SSD_N:(g + 1) * SSD_N].astype(BF16)
        cg_s[g] = c_ref[:, g * SSD_N:(g + 1) * SSD_N].astype(BF16)

    gp = SSD_R * SSD_P

    def group_body(g, carry):
        bg = bg_s[g]
        cg = cg_s[g]
        cbt = _dot_nt(bg, cg)
        r0 = pl.multiple_of(g * gp, gp)
        sg = st[pl.ds(r0, gp), :]
        sc = _dot_nt(sg.astype(BF16), cg)
        xg = xt_s[pl.ds(r0, gp), :]
        xw, decs = [], []
        for r in range(SSD_R):
            hd = g * SSD_R + r
            xh = xg[r * SSD_P:(r + 1) * SSD_P]
            cs_i = cs_s[pl.ds(hd, 1), :]
            cs_j = jnp.broadcast_to(cs_i, (q, q)).T
            decay = jnp.exp(jnp.where(seen, cs_i - cs_j, -jnp.inf))
            mt = (cbt * decay).astype(BF16)
            xdt = (xh * dtr_s[pl.ds(hd, 1), :]).astype(BF16)
            yh = _dot(xdt, mt) + sc[r * SSD_P:(r + 1) * SSD_P] * ecs_s[pl.ds(hd, 1), :]
            yb_s[pl.ds(pl.multiple_of(r0 + r * SSD_P, SSD_P), SSD_P), :] = yh
            xw.append((xh * wend_s[pl.ds(hd, 1), :]).astype(BF16))
            decs.append(jnp.broadcast_to(dec_s[pl.ds(hd, 1), :], (SSD_P, SSD_N)))
        upd = _dot(jnp.concatenate(xw, axis=0), bg)
        st[pl.ds(r0, gp), :] = sg * jnp.concatenate(decs, axis=0) + upd
        return carry

    lax.fori_loop(0, g_n, group_body, 0)

    @pl.when(jnp.logical_not(is_bwd))
    def _():
        yf_s[ck] = yb_s[...]

    @pl.when(is_bwd)
    def _():
        yt = yb_s[...] + yf_s[ck] + dsk_ref[...] * xt_s[...]
        yt = yt * _silu(z_ref[...].T)
        y3 = yt.reshape(g_n, gp, q)
        yn = y3 * lax.rsqrt(jnp.mean(y3 * y3, axis=1, keepdims=True) + RMS_EPS)
        y_ref[...] = (yn.reshape(dims.d_ssd, q) * nw_ref[...]).T

    @pl.when(last)
    def _():
        sfin_ref[0, 0] = st[...]


def _ssd_call(dims, proj, dt_t, da_t, dskip, nw, s0, n_seq, seq_len, tok_off):
    d, ds_ = dims.d, dims.d_ssd
    q = SSD_Q
    nchunk = seq_len // q
    off = tok_off // q
    has_init = s0 is not None

    def ck(s):
        return jnp.where(s < nchunk, s, 2 * nchunk - 1 - s)

    def tok(b, s):
        return off + b * nchunk + ck(s)

    def ytok(b, s):
        return b * nchunk + jnp.where(s < nchunk, nchunk - 1, 2 * nchunk - 1 - s)

    in_specs = [pl.BlockSpec((q, ds_), lambda b, s: (tok(b, s), 0)),
                pl.BlockSpec((q, d), lambda b, s: (tok(b, s), ds_ // d)),
                pl.BlockSpec((q, d), lambda b, s: (tok(b, s), ds_ // d + 1)),
                pl.BlockSpec((q, ds_), lambda b, s: (tok(b, s), (ds_ + 2 * d) // ds_)),
                pl.BlockSpec((1, LANE, q), lambda b, s: (tok(b, s), 0, 0)),
                pl.BlockSpec((1, LANE, q), lambda b, s: (tok(b, s), 0, 0)),
                pl.BlockSpec((ds_, q), lambda b, s: (0, 0)),
                pl.BlockSpec((ds_, q), lambda b, s: (0, 0))]
    args = [proj, proj, proj, proj, dt_t, da_t, dskip, nw]
    if has_init:
        in_specs.append(pl.BlockSpec((1, 1, ds_, SSD_N), lambda b, s: (b, s // nchunk, 0, 0)))
        args.append(s0)
    h = dims.h_ssd
    scratch = [pltpu.VMEM((ds_, SSD_N), F32),
               pltpu.VMEM((nchunk, ds_, q), F32),
               pltpu.VMEM((ds_, q), F32),
               pltpu.VMEM((ds_, q), F32),
               pltpu.VMEM((dims.groups, q, SSD_N), BF16),
               pltpu.VMEM((dims.groups, q, SSD_N), BF16),
               pltpu.VMEM((h, q), F32), pltpu.VMEM((h, q), F32), pltpu.VMEM((h, q), F32),
               pltpu.VMEM((h, q), F32), pltpu.VMEM((h, q), F32)]
    vmem = (2 * (3 * q * ds_ + 2 * q * d + 2 * LANE * q + 2 * ds_ * q + 2 * ds_ * SSD_N) * 4
            + (nchunk + 3) * ds_ * q * 4 + (12 << 20))
    return pl.pallas_call(
        functools.partial(_ssd_kernel, dims, nchunk, has_init),
        grid=(n_seq, 2 * nchunk),
        in_specs=in_specs,
        out_specs=(pl.BlockSpec((q, ds_), lambda b, s: (ytok(b, s), 0)),
                   pl.BlockSpec((1, 1, ds_, SSD_N), lambda b, s: (b, s // nchunk, 0, 0))),
        out_shape=(jax.ShapeDtypeStruct((n_seq * seq_len, ds_), F32),
                   jax.ShapeDtypeStruct((n_seq, 2, ds_, SSD_N), F32)),
        scratch_shapes=scratch,
        compiler_params=_cparams(("arbitrary", "arbitrary"), vmem),
        name="ssd_scan",
    )(*args)


def _hgrn_gates(u, lb):
    log_sig = jnp.minimum(u, 0.0) - jnp.log1p(jnp.exp(-jnp.abs(u)))
    a = jnp.log(lb)
    c = jnp.log1p(-lb) + log_sig
    log_f = jnp.maximum(a, c) + jnp.log1p(jnp.exp(-jnp.abs(a - c)))
    return log_f, (1.0 - lb) * jax.nn.sigmoid(-u)


def _bcast_block_rows(x, block, row_in_block):
    n = x.shape[0]
    parts = [jnp.broadcast_to(x[p * block + row_in_block:p * block + row_in_block + 1, :], (block, x.shape[1]))
             for p in range(n // block)]
    return parts[0] if len(parts) == 1 else jnp.concatenate(parts, axis=0)


def _keep_half(x, half, keep_hi):
    n = x.shape[0]
    z = jnp.zeros((half, x.shape[1]), x.dtype)
    parts = []
    for p in range(n // (2 * half)):
        lo = x[p * 2 * half:p * 2 * half + half]
        hi = x[p * 2 * half + half:(p + 1) * 2 * half]
        parts += [z, hi] if keep_hi else [lo, z]
    return jnp.concatenate(parts, axis=0)


def _hgrn_kernel(dims, nblk, tb, has_init, *refs):
    if has_init:
        (q_ref, ff_ref, fb_ref, v_ref, g_ref, lb_ref, nw_ref, s0_ref, o_ref, sfin_ref, st, o_s) = refs
    else:
        (q_ref, ff_ref, fb_ref, v_ref, g_ref, lb_ref, nw_ref, o_ref, sfin_ref, st, o_s) = refs
    hh = dims.hh
    c = HG_C
    nck = tb // c
    s = pl.program_id(1)
    is_bwd = s >= nblk
    blk = jnp.where(is_bwd, 2 * nblk - 1 - s, s)
    first = (s == 0) | (s == nblk)
    last = (s == nblk - 1) | (s == 2 * nblk - 1)

    @pl.when(first)
    def _():
        for hd in range(hh):
            st[hd] = s0_ref[0, 0, hd].T if has_init else jnp.zeros((HG_V, HG_K), F32)

    ti = lax.broadcasted_iota(jnp.int32, (c, c), 0)
    tj = lax.broadcasted_iota(jnp.int32, (c, c), 1)
    le = (tj <= ti)
    ge = (tj >= ti)
    ltri = le.astype(BF16)
    utri = ge.astype(BF16)
    same_diag = (ti // HG_DIAG) == (tj // HG_DIAG)
    mask_df = (same_diag & le).astype(F32)
    mask_db = (same_diag & ge).astype(F32)
    parent = {hf: ((ti // (2 * hf)) == (tj // (2 * hf))).astype(F32) for hf in HG_LEVELS if 2 * hf < c}
    lb_f = lb_ref[0:1, :]
    lb_b = lb_ref[1:2, :]

    def heads(x):
        return [x[:, hd * HG_K:(hd + 1) * HG_K] for hd in range(hh)]

    def state_step(hd, qs_h, ke_h, v_h, dec_h):
        sh = st[hd]
        off = _dot_nt(qs_h.astype(BF16), sh.astype(BF16))
        st[hd] = sh * dec_h + _dot_tn(v_h.astype(BF16), ke_h.astype(BF16))
        return off

    def fwd_chunk(idx, carry):
        r0 = pl.multiple_of(idx * c, c)
        rows = pl.ds(r0, c)
        qq = _silu(q_ref[rows, :])
        vv = v_ref[rows, :]
        lf_f, k_f = _hgrn_gates(ff_ref[rows, :], lb_f)
        lf_b, k_b = _hgrn_gates(fb_ref[rows, :], lb_b)
        b_f = _sum01_left(ltri, lf_f)
        r_b = _sum01_left(utri, lf_b)
        lv_q, lv_k = [], []
        for hf in HG_LEVELS:
            x_f = jnp.exp(-jnp.abs(b_f - _bcast_block_rows(b_f, 2 * hf, hf - 1)))
            x_b = jnp.exp(-jnp.abs(r_b - _bcast_block_rows(r_b, 2 * hf, hf)))
            qf = _keep_half(qq * x_f, hf, True).astype(BF16)
            kf = _keep_half(k_f * x_f, hf, False).astype(BF16)
            qb = _keep_half(qq * x_b, hf, False).astype(BF16)
            kb = _keep_half(k_b * x_b, hf, True).astype(BF16)
            lv_q.append((heads(qf), heads(qb)))
            lv_k.append((heads(kf), heads(kb)))
        dm_f = b_f - _bcast_block_rows(b_f, HG_DIAG, HG_DIAG // 2 - 1)
        dm_b = r_b - _bcast_block_rows(r_b, HG_DIAG, HG_DIAG // 2)
        qd_f = heads((qq * jnp.exp(dm_f)).astype(BF16))
        kd_f = heads((k_f * jnp.exp(-dm_f)).astype(BF16))
        qd_b = heads((qq * jnp.exp(dm_b)).astype(BF16))
        kd_b = heads((k_b * jnp.exp(-dm_b)).astype(BF16))
        qs = heads(qq * jnp.exp(b_f))
        ke = heads(k_f * jnp.exp(b_f[c - 1:c, :] - b_f))
        dec = heads(jnp.exp(b_f[c - 1:c, :]))
        vh = heads(vv)
        outs = []
        for hd in range(hh):
            att = _dot_nt(qd_f[hd], kd_f[hd]) * mask_df + _dot_nt(qd_b[hd], kd_b[hd]) * mask_db
            for li, hf in enumerate(HG_LEVELS):
                p = _dot_nt(jnp.concatenate([lv_q[li][0][hd], lv_q[li][1][hd]], axis=1),
                            jnp.concatenate([lv_k[li][0][hd], lv_k[li][1][hd]], axis=1))
                att = att + (p * parent[hf] if hf in parent else p)
            o_h = _dot(att.astype(BF16), vh[hd].astype(BF16))
            outs.append(o_h + state_step(hd, qs[hd], ke[hd], vh[hd], dec[hd]))
        o_s[pl.ds(pl.multiple_of(blk * tb + r0, c), c), :] = jnp.concatenate(outs, axis=1)
        return carry

    def bwd_chunk(idx, carry):
        r0 = pl.multiple_of((nck - 1 - idx) * c, c)
        rows = pl.ds(r0, c)
        qq = _silu(q_ref[rows, :])
        vh = heads(v_ref[rows, :])
        lf_b, k_b = _hgrn_gates(fb_ref[rows, :], lb_b)
        r_b = _sum01_left(utri, lf_b)
        qs = heads(qq * jnp.exp(r_b))
        ke = heads(k_b * jnp.exp(r_b[0:1, :] - r_b))
        dec = heads(jnp.exp(r_b[0:1, :]))
        o_prev = heads(o_s[pl.ds(pl.multiple_of(blk * tb + r0, c), c), :])
        gate = heads(_silu(g_ref[rows, :]))
        nw = heads(nw_ref[...])
        outs = []
        for hd in range(hh):
            o_h = o_prev[hd] + state_step(hd, qs[hd], ke[hd], vh[hd], dec[hd])
            o_h = o_h * lax.rsqrt(jnp.mean(o_h * o_h, axis=-1, keepdims=True) + RMS_EPS) * nw[hd]
            outs.append(o_h * gate[hd])
        o_ref[rows, :] = jnp.concatenate(outs, axis=1)
        return carry

    @pl.when(jnp.logical_not(is_bwd))
    def _():
        lax.fori_loop(0, nck, fwd_chunk, 0)

    @pl.when(is_bwd)
    def _():
        lax.fori_loop(0, nck, bwd_chunk, 0)

    @pl.when(last)
    def _():
        for hd in range(hh):
            sfin_ref[0, 0, hd] = st[hd].T


def _hgrn_call(dims, proj, lb, nw, s0, n_seq, seq_len, tok_off):
    d, hh = dims.d, dims.hh
    tb = min(512, seq_len)
    nblk = seq_len // tb
    off = tok_off // tb
    has_init = s0 is not None
    col0 = (dims.d_ssd + 2 * d + dims.d_ssd) // d

    def tok(b, s):
        return off + b * nblk + jnp.where(s < nblk, s, 2 * nblk - 1 - s)

    def otok(b, s):
        return b * nblk + jnp.where(s < nblk, nblk - 1, 2 * nblk - 1 - s)

    in_specs = [pl.BlockSpec((tb, d), functools.partial(lambda cc, b, s: (tok(b, s), cc), col0 + k)) for k in range(5)]
    in_specs += [pl.BlockSpec((8, d), lambda b, s: (0, 0)), pl.BlockSpec((1, d), lambda b, s: (0, 0))]
    args = [proj] * 5 + [lb, nw]
    if has_init:
        in_specs.append(pl.BlockSpec((1, 1, hh, HG_K, HG_V), lambda b, s: (b, s // nblk, 0, 0, 0)))
        args.append(s0)
    vmem = 2 * 6 * tb * d * 4 + seq_len * d * 4 + 4 * hh * HG_K * HG_V * 4 + (16 << 20)
    return pl.pallas_call(
        functools.partial(_hgrn_kernel, dims, nblk, tb, has_init),
        grid=(n_seq, 2 * nblk),
        in_specs=in_specs,
        out_specs=(pl.BlockSpec((tb, d), lambda b, s: (otok(b, s), 0)),
                   pl.BlockSpec((1, 1, hh, HG_K, HG_V), lambda b, s: (b, s // nblk, 0, 0, 0))),
        out_shape=(jax.ShapeDtypeStruct((n_seq * seq_len, d), F32),
                   jax.ShapeDtypeStruct((n_seq, 2, hh, HG_K, HG_V), F32)),
        scratch_shapes=[pltpu.VMEM((hh, HG_V, HG_K), F32), pltpu.VMEM((seq_len, d), F32)],
        compiler_params=_cparams(("arbitrary", "arbitrary"), vmem),
        name="hgrn_scan",
    )(*args)


def _merge_kernel(dims, tm, yc_ref, yl_ref, oc_ref, ol_ref, g1_ref, g2_ref, x_ref, mod_ref,
                  wso_ref, who_ref, wo_ref, out_ref):
    d = dims.d
    is_ctx = pl.program_id(0) < dims.t_ctx // tm
    y = jnp.where(is_ctx, yc_ref[...], yl_ref[...])
    o = jnp.where(is_ctx, oc_ref[...], ol_ref[...])
    y_ssd = _dot(y.astype(BF16), wso_ref[...])
    y_hg = _dot(o.astype(BF16), who_ref[...])
    merged = jax.nn.sigmoid(g1_ref[...]) * y_ssd + jax.nn.sigmoid(g2_ref[...]) * y_hg
    mix = _dot(merged.astype(BF16), wo_ref[...])
    out_ref[...] = x_ref[...] + mod_ref[0][:, 2 * d:3 * d] * mix


def _merge_call(dims, y_ctx, y_lat, o_ctx, o_lat, proj, x, mod_l, wso, who, wo):
    d, ds_, t = dims.d, dims.d_ssd, dims.t
    tm = min(512, dims.row_tile)
    nctx = dims.t_ctx // tm
    gcol = proj.shape[1] // d - 2

    def ctx_blk(i):
        return jnp.minimum(i, nctx - 1)

    def lat_blk(i):
        return jnp.maximum(i - nctx, 0)

    const = lambda i: (0, 0)
    return pl.pallas_call(
        functools.partial(_merge_kernel, dims, tm),
        grid=(t // tm,),
        in_specs=[pl.BlockSpec((tm, ds_), lambda i: (ctx_blk(i), 0)),
                  pl.BlockSpec((tm, ds_), lambda i: (lat_blk(i), 0)),
                  pl.BlockSpec((tm, d), lambda i: (ctx_blk(i), 0)),
                  pl.BlockSpec((tm, d), lambda i: (lat_blk(i), 0)),
                  pl.BlockSpec((tm, d), lambda i: (i, gcol)),
                  pl.BlockSpec((tm, d), lambda i: (i, gcol + 1)),
                  pl.BlockSpec((tm, d), lambda i: (i, 0)),
                  pl.BlockSpec((1, 1, 6 * d), lambda i: (_mod_row(i, tm, dims), 0, 0)),
                  pl.BlockSpec((ds_, d), const), pl.BlockSpec((d, d), const), pl.BlockSpec((d, d), const)],
        out_specs=pl.BlockSpec((tm, d), lambda i: (i, 0)),
        out_shape=jax.ShapeDtypeStruct((t, d), F32),
        compiler_params=_cparams(("arbitrary",), 2 * (2 * tm * ds_ + 6 * tm * d) * 4 + 2 * (ds_ + 2 * d) * d * 2 + (12 << 20)),
        name="merge_outproj",
    )(y_ctx, y_lat, o_ctx, o_lat, proj, proj, x, mod_l, wso, who, wo)


def _ffn_kernel(dims, x_ref, mod_ref, nw_ref, w1_ref, w3_ref, w2_ref, out_ref, h_s, acc_s):
    d = dims.d
    j = pl.program_id(1)

    @pl.when(j == 0)
    def _():
        x = x_ref[...]
        y = x * lax.rsqrt(jnp.mean(x * x, axis=-1, keepdims=True) + RMS_EPS) * nw_ref[...]
        mod = mod_ref[0]
        h_s[...] = (y * (1.0 + mod[:, 4 * d:5 * d]) + mod[:, 3 * d:4 * d]).astype(BF16)
        acc_s[...] = jnp.zeros(acc_s.shape, F32)

    hb = h_s[...]
    act = _silu(_dot(hb, w1_ref[...])) * _dot(hb, w3_ref[...])
    acc_s[...] += _dot(act.astype(BF16), w2_ref[...])

    @pl.when(j == pl.num_programs(1) - 1)
    def _():
        out_ref[...] = x_ref[...] + mod_ref[0][:, 5 * d:6 * d] * acc_s[...]


def _ffn_call(dims, x, mod_l, nw, w1, w3, w2):
    d, t, f = dims.d, dims.t, dims.ffn
    tm = min(512, dims.row_tile)
    th = f // 2 if (f // 2) % LANE == 0 and f % 2 == 0 else f
    return pl.pallas_call(
        functools.partial(_ffn_kernel, dims),
        grid=(t // tm, f // th),
        in_specs=[pl.BlockSpec((tm, d), lambda i, j: (i, 0)),
                  pl.BlockSpec((1, 1, 6 * d), lambda i, j: (_mod_row(i, tm, dims), 0, 0)),
                  pl.BlockSpec((1, d), lambda i, j: (0, 0)),
                  pl.BlockSpec((d, th), lambda i, j: (0, j)),
                  pl.BlockSpec((d, th), lambda i, j: (0, j)),
                  pl.BlockSpec((th, d), lambda i, j: (j, 0))],
        out_specs=pl.BlockSpec((tm, d), lambda i, j: (i, 0)),
        out_shape=jax.ShapeDtypeStruct((t, d), F32),
        scratch_shapes=[pltpu.VMEM((tm, d), BF16), pltpu.VMEM((tm, d), F32)],
        compiler_params=_cparams(("arbitrary", "arbitrary"), 4 * tm * d * 4 + 2 * 3 * d * th * 2 + 4 * tm * th * 4 + (12 << 20)),
        name="ffn",
    )(x, mod_l, nw, w1, w3, w2)


def _final_norm_kernel(x_ref, w_ref, o_ref):
    x = x_ref[...]
    o_ref[...] = x * lax.rsqrt(jnp.mean(x * x, axis=-1, keepdims=True) + RMS_EPS) * w_ref[...]


def _final_norm_call(dims, x, w):
    d, t = dims.d, dims.t
    tm = dims.row_tile
    return pl.pallas_call(
        _final_norm_kernel,
        grid=(t // tm,),
        in_specs=[pl.BlockSpec((tm, d), lambda i: (i, 0)), pl.BlockSpec((1, d), lambda i: (0, 0))],
        out_specs=pl.BlockSpec((tm, d), lambda i: (i, 0)),
        out_shape=jax.ShapeDtypeStruct((t, d), F32),
        compiler_params=_cparams(("arbitrary",), 4 * tm * d * 4 + (8 << 20)),
        name="final_norm",
    )(x, w)


def _forward(dims, x_prompt, x_sample, c, state_ssm, state_hgrn, c_ctx, norm_mix_w, norm_ffn_w, ada_w, ada_b, w_in,
             ssd_conv_w, ssd_conv_b, ssd_dt_bias, ssd_a_log, ssd_d, ssd_norm_w, hgrn_lower_bounds, hgrn_norm_w,
             w_ssd_out, w_hgrn_out, w_out, ffn_w1, ffn_w3, ffn_w2, norm_final_w):
    d, ds_, h = dims.d, dims.d_ssd, dims.h_ssd
    assert dims.db + 1 <= MOD_ROWS and 2 * h <= LANE and dims.nl % HG_C == 0 and dims.dl % 512 == 0
    conv_dim = ds_ + 2 * dims.groups * SSD_N
    x = jnp.concatenate([x_prompt.reshape(dims.t_ctx, d), x_sample.reshape(dims.t_lat, d)], axis=0)
    cvec = jnp.zeros((MOD_ROWS, d), F32).at[0].set(c_ctx).at[1:1 + dims.db].set(c)
    mod = _mod_call(dims, cvec, ada_w, ada_b)
    lb_all = jnp.cumsum(jax.nn.softmax(hgrn_lower_bounds.astype(F32), axis=0), axis=0)
    lb_all = lb_all - lb_all[:1]
    dt_cols = slice(conv_dim + ds_, conv_dim + ds_ + 2 * h)

    ssm_states, hgrn_states = [], []
    for l in range(dims.depth):
        mod_l = mod[l].reshape(MOD_ROWS, 1, 6 * d)
        w_l = w_in[l]
        w_main = jnp.concatenate([w_l[:, :dt_cols.start], w_l[:, dt_cols.stop:]], axis=1).astype(BF16)
        wdt_t = jnp.zeros((LANE, d), F32).at[:2 * h].set(w_l[:, dt_cols].T).astype(BF16)
        dtb = jnp.zeros((LANE, 1), F32).at[:2 * h, 0].set(ssd_dt_bias[l].reshape(-1))
        alog = jnp.zeros((LANE, 1), F32).at[:2 * h, 0].set(ssd_a_log[l].reshape(-1))
        proj, dt_t, da_t = _inproj_call(dims, x, mod_l, norm_mix_w[l].reshape(1, d), w_main,
                                        ssd_conv_w[l], ssd_conv_b[l].reshape(1, conv_dim), wdt_t, dtb, alog)
        dskip = jnp.broadcast_to(jnp.repeat(ssd_d[l], SSD_P)[:, None], (ds_, SSD_Q))
        ssd_nw = jnp.broadcast_to(ssd_norm_w[l][:, None], (ds_, SSD_Q))
        y_ctx, s_ssm = _ssd_call(dims, proj, dt_t, da_t, dskip, ssd_nw, None, dims.nb, dims.nl, 0)
        y_lat, _ = _ssd_call(dims, proj, dt_t, da_t, dskip, ssd_nw,
                             state_ssm[:, l].reshape(dims.db, 2, ds_, SSD_N), dims.db, dims.dl, dims.t_ctx)
        lb = jnp.zeros((8, d), F32).at[:2].set(lb_all[l])
        hg_nw = hgrn_norm_w[l].reshape(1, d)
        o_ctx, s_hg = _hgrn_call(dims, proj, lb, hg_nw, None, dims.nb, dims.nl, 0)
        o_lat, _ = _hgrn_call(dims, proj, lb, hg_nw, state_hgrn[:, l], dims.db, dims.dl, dims.t_ctx)
        x = _merge_call(dims, y_ctx, y_lat, o_ctx, o_lat, proj, x, mod_l,
                        w_ssd_out[l].astype(BF16), w_hgrn_out[l].astype(BF16), w_out[l].astype(BF16))
        x = _ffn_call(dims, x, mod_l, norm_ffn_w[l].reshape(1, d),
                      ffn_w1[l].astype(BF16), ffn_w3[l].astype(BF16), ffn_w2[l].astype(BF16))
        ssm_states.append(s_ssm.reshape(dims.nb, 2, h, SSD_P, SSD_N))
        hgrn_states.append(s_hg)
    xn = _final_norm_call(dims, x, norm_final_w.reshape(1, d))
    y_prompt = xn[:dims.t_ctx].reshape(dims.nb, dims.nl, d)
    y_sample = xn[dims.t_ctx:].reshape(dims.db, dims.dl, d)
    return y_prompt, y_sample, jnp.stack(ssm_states, axis=1), jnp.stack(hgrn_states, axis=1)


def kernel(x_prompt, x_sample, c, state_ssm, state_hgrn, c_ctx, norm_mix_w, norm_ffn_w, ada_w, ada_b, w_in,
           ssd_conv_w, ssd_conv_b, ssd_dt_bias, ssd_a_log, ssd_d, ssd_norm_w, hgrn_lower_bounds, hgrn_norm_w,
           w_ssd_out, w_hgrn_out, w_out, ffn_w1, ffn_w3, ffn_w2, norm_final_w):
    dims = Dims(d=x_prompt.shape[2], nb=x_prompt.shape[0], nl=x_prompt.shape[1], db=x_sample.shape[0],
                dl=x_sample.shape[1], depth=w_in.shape[0], ffn=ffn_w1.shape[2])
    return _forward(dims, x_prompt, x_sample, c, state_ssm, state_hgrn, c_ctx, norm_mix_w, norm_ffn_w, ada_w, ada_b,
                    w_in, ssd_conv_w, ssd_conv_b, ssd_dt_bias, ssd_a_log, ssd_d, ssd_norm_w, hgrn_lower_bounds,
                    hgrn_norm_w, w_ssd_out, w_hgrn_out, w_out, ffn_w1, ffn_w3, ffn_w2, norm_final_w)
```

```python
import functools
from typing import NamedTuple

import jax
import jax.numpy as jnp
from jax import lax
from jax.experimental import pallas as pl
from jax.experimental.pallas import tpu as pltpu

F32 = jnp.float32
BF16 = jnp.bfloat16

LANE = 128
MXU_COLS = 256
VMEM_BYTES = 64 * 1024 * 1024
RMS_EPS = 1e-6
D_CONV = 5
GRID_W = 64
SSD_P = 64
SSD_N = 128
SSD_Q = 128
SSD_R = 4
HG_K = 128
HG_V = 128
HG_C = 128
HG_LEVELS = (64, 32, 16)
HG_DIAG = 16
MOD_ROWS = 8


class Dims(NamedTuple):
    d: int
    nb: int
    nl: int
    db: int
    dl: int
    depth: int
    ffn: int

    @property
    def t_ctx(self):
        return self.nb * self.nl

    @property
    def t_lat(self):
        return self.db * self.dl

    @property
    def t(self):
        return self.t_ctx + self.t_lat

    @property
    def d_ssd(self):
        return 2 * self.d

    @property
    def h_ssd(self):
        return self.d_ssd // SSD_P

    @property
    def groups(self):
        return self.h_ssd // SSD_R

    @property
    def hh(self):
        return self.d // HG_K

    @property
    def row_tile(self):
        tm = min(1024, self.dl, self.t_ctx)
        assert self.dl % tm == 0 and self.t_ctx % tm == 0 and tm % self.nl == 0 and tm % GRID_W == 0
        return tm


def _cparams(semantics, vmem_bytes):
    return pltpu.CompilerParams(dimension_semantics=semantics, vmem_limit_bytes=min(vmem_bytes, VMEM_BYTES - (4 << 20)))


def _dot(a, b):
    return jnp.dot(a, b, preferred_element_type=F32)


def _dot_nt(a, b):
    return lax.dot_general(a, b, (((1,), (1,)), ((), ())), preferred_element_type=F32)


def _dot_tn(a, b):
    return lax.dot_general(a, b, (((0,), (0,)), ((), ())), preferred_element_type=F32)


def _split3(x):
    hi = x.astype(BF16)
    r = x - hi.astype(F32)
    mid = r.astype(BF16)
    lo = (r - mid.astype(F32)).astype(BF16)
    return hi, mid, lo


def _sum01_left(m01, x):
    hi, mid, lo = _split3(x)
    return _dot(m01, hi) + _dot(m01, mid) + _dot(m01, lo)


def _sum01_right(x, m01):
    hi, mid, lo = _split3(x)
    return _dot(hi, m01) + _dot(mid, m01) + _dot(lo, m01)


def _sigmoid(x):
    return 0.5 * jnp.tanh(0.5 * x) + 0.5


def _silu(x):
    return x * _sigmoid(x)


def _softplus(x):
    return jnp.maximum(x, 0.0) + jnp.log1p(jnp.exp(-jnp.abs(x)))


def _mod_row(i, tm, dims):
    nctx = dims.t_ctx // tm
    return jnp.where(i < nctx, 0, 1 + ((i - nctx) * tm) // dims.dl)


def _mod_kernel(c_ref, w_ref, b_ref, o_ref):
    s = _silu(c_ref[...])
    o_ref[0] = _dot(s.astype(BF16), w_ref[0].astype(BF16)) + b_ref[0]


def _mod_call(dims, cvec, ada_w, ada_b):
    d, n = dims.d, 6 * dims.d
    tn = n // 4
    return pl.pallas_call(
        _mod_kernel,
        grid=(dims.depth, n // tn),
        in_specs=[pl.BlockSpec((MOD_ROWS, d), lambda l, j: (0, 0)),
                  pl.BlockSpec((1, d, tn), lambda l, j: (l, 0, j)),
                  pl.BlockSpec((1, 1, tn), lambda l, j: (l, 0, j))],
        out_specs=pl.BlockSpec((1, MOD_ROWS, tn), lambda l, j: (l, 0, j)),
        out_shape=jax.ShapeDtypeStruct((dims.depth, MOD_ROWS, n), F32),
        compiler_params=_cparams(("arbitrary", "arbitrary"), 2 * d * tn * 4 + (8 << 20)),
        name="adaln_mod",
    )(cvec, ada_w, ada_b.reshape(dims.depth, 1, n))


def _inproj_kernel(dims, tm, x_ref, mod_ref, nw_ref, w_ref, cw_ref, cb_ref, wdt_ref, dtb_ref, alog_ref,
                   proj_ref, dt_ref, da_ref, h_s):
    d = dims.d
    i = pl.program_id(0)
    j = pl.program_id(1)
    n_conv = (dims.d_ssd + 2 * dims.groups * SSD_N) // d

    @pl.when(j == 0)
    def _():
        x = x_ref[...]
        y = x * lax.rsqrt(jnp.mean(x * x, axis=-1, keepdims=True) + RMS_EPS) * nw_ref[...]
        mod = mod_ref[0]
        hb = (y * (1.0 + mod[:, d:2 * d]) + mod[:, 0:d]).astype(BF16)
        h_s[...] = hb
        dt = _softplus(_dot_nt(wdt_ref[...], hb) + dtb_ref[...])
        da = dt * -jnp.exp(alog_ref[...])
        for k in range(tm // SSD_Q):
            dt_ref[k] = dt[:, k * SSD_Q:(k + 1) * SSD_Q]
            da_ref[k] = da[:, k * SSD_Q:(k + 1) * SSD_Q]

    @pl.when(j >= n_conv)
    def _():
        proj_ref[...] = _dot(h_s[...], w_ref[...])

    def conv_silu(seg):
        ns, half, sub = tm // seg, D_CONV // 2, 8
        n8 = ns * sub
        r8 = lax.broadcasted_iota(jnp.int32, (n8, LANE), 0) & (sub - 1)
        res = None
        for sl in range(d // LANE):
            cols = slice(sl * LANE, (sl + 1) * LANE)
            if sl % (MXU_COLS // LANE) == 0:
                res = _dot(h_s[...], w_ref[:, sl * LANE:sl * LANE + MXU_COLS])
            u = res[:, (sl * LANE) % MXU_COLS:(sl * LANE) % MXU_COLS + LANE]
            w = [cw_ref[k:k + 1, cols] for k in range(D_CONV)]
            acc = cb_ref[:, cols] + w[half] * u
            for k in range(D_CONV):
                if k != half:
                    acc = acc + w[k] * pltpu.roll(u, (half - k) % tm, 0)
            u3 = u.reshape(ns, seg, LANE)
            a3 = acc.reshape(ns, seg, LANE)
            edge_lo = u3[:, 0:sub, :].reshape(n8, LANE)
            edge_hi = u3[:, seg - sub:seg, :].reshape(n8, LANE)
            leak_lo = (w[0] * jnp.where(r8 < 2, pltpu.roll(edge_hi, 2, 0), 0.0)
                       + w[1] * jnp.where(r8 < 1, pltpu.roll(edge_hi, 1, 0), 0.0))
            leak_hi = (w[4] * jnp.where(r8 >= sub - 2, pltpu.roll(edge_lo, n8 - 2, 0), 0.0)
                       + w[3] * jnp.where(r8 >= sub - 1, pltpu.roll(edge_lo, n8 - 1, 0), 0.0))
            a3 = jnp.concatenate([a3[:, 0:sub, :] - leak_lo.reshape(ns, sub, LANE),
                                  a3[:, sub:seg - sub, :],
                                  a3[:, seg - sub:seg, :] - leak_hi.reshape(ns, sub, LANE)], axis=1)
            proj_ref[:, cols] = _silu(a3.reshape(tm, LANE))

    is_ctx = i < dims.t_ctx // tm

    @pl.when((j < n_conv) & is_ctx)
    def _():
        conv_silu(dims.nl)

    @pl.when((j < n_conv) & jnp.logical_not(is_ctx))
    def _():
        conv_silu(GRID_W)


def _inproj_call(dims, x, mod_l, nw, w, cw, cb, wdt_t, dtb, alog):
    d, t, tm = dims.d, dims.t, dims.row_tile
    ncol = w.shape[1]
    n_conv = (dims.d_ssd + 2 * dims.groups * SSD_N) // d
    nchunk = t // SSD_Q
    out_shapes = (jax.ShapeDtypeStruct((t, ncol), F32),
                  jax.ShapeDtypeStruct((nchunk, LANE, SSD_Q), F32),
                  jax.ShapeDtypeStruct((nchunk, LANE, SSD_Q), F32))
    kc = tm // SSD_Q
    return pl.pallas_call(
        functools.partial(_inproj_kernel, dims, tm),
        grid=(t // tm, ncol // d),
        in_specs=[pl.BlockSpec((tm, d), lambda i, j: (i, 0)),
                  pl.BlockSpec((1, 1, 6 * d), lambda i, j: (_mod_row(i, tm, dims), 0, 0)),
                  pl.BlockSpec((1, d), lambda i, j: (0, 0)),
                  pl.BlockSpec((d, d), lambda i, j: (0, j)),
                  pl.BlockSpec((D_CONV, d), lambda i, j: (0, jnp.minimum(j, n_conv - 1))),
                  pl.BlockSpec((1, d), lambda i, j: (0, jnp.minimum(j, n_conv - 1))),
                  pl.BlockSpec((LANE, d), lambda i, j: (0, 0)),
                  pl.BlockSpec((LANE, 1), lambda i, j: (0, 0)),
                  pl.BlockSpec((LANE, 1), lambda i, j: (0, 0))],
        out_specs=(pl.BlockSpec((tm, d), lambda i, j: (i, j)),
                   pl.BlockSpec((kc, LANE, SSD_Q), lambda i, j: (i, 0, 0)),
                   pl.BlockSpec((kc, LANE, SSD_Q), lambda i, j: (i, 0, 0))),
        out_shape=out_shapes,
        scratch_shapes=[pltpu.VMEM((tm, d), BF16)],
        compiler_params=_cparams(("arbitrary", "arbitrary"), 6 * tm * d * 4 + 2 * d * d * 2 + tm * d * 2 + (8 << 20)),
        name="inproj",
    )(x, mod_l, nw, w, cw, cb, wdt_t, dtb, alog)


def _ssd_kernel(dims, nchunk, has_init, *refs):
    if has_init:
        (x_ref, b_ref, c_ref, z_ref, dt_ref, da_ref, dsk_ref, nw_ref, s0_ref,
         y_ref, sfin_ref, st, yf_s, xt_s, yb_s, bg_s, cg_s, cs_s, ecs_s, wend_s, dec_s, dtr_s) = refs
    else:
        (x_ref, b_ref, c_ref, z_ref, dt_ref, da_ref, dsk_ref, nw_ref,
         y_ref, sfin_ref, st, yf_s, xt_s, yb_s, bg_s, cg_s, cs_s, ecs_s, wend_s, dec_s, dtr_s) = refs
    h, g_n = dims.h_ssd, dims.groups
    q = SSD_Q
    s = pl.program_id(1)
    is_bwd = s >= nchunk
    ck = jnp.where(is_bwd, 2 * nchunk - 1 - s, s)
    first = (s == 0) | (s == nchunk)
    last = (s == nchunk - 1) | (s == 2 * nchunk - 1)

    @pl.when(first)
    def _():
        st[...] = s0_ref[0, 0] if has_init else jnp.zeros(st.shape, F32)

    row0 = pl.multiple_of(jnp.where(is_bwd, h, 0), 8)
    dtr = dt_ref[0, pl.ds(row0, h), :]
    dar = da_ref[0, pl.ds(row0, h), :]
    jj = lax.broadcasted_iota(jnp.int32, (q, q), 0)
    ii = lax.broadcasted_iota(jnp.int32, (q, q), 1)
    seen = jnp.where(is_bwd, jj - ii, ii - jj) >= 0
    cs = _sum01_right(dar, seen.astype(BF16))
    tot = _sum01_right(dar, jnp.ones((q, q), BF16))
    cs_s[...] = cs
    ecs_s[...] = jnp.exp(cs)
    wend_s[...] = jnp.exp(tot - cs) * dtr
    dec_s[...] = jnp.exp(tot)
    dtr_s[...] = dtr
    xt_s[...] = x_ref[...].T
    for g in range(g_n):
        bg_s[g] = b_ref[:, g * SSD_N:(g + 1) * SSD_N].astype(BF16)
        cg_s[g] = c_ref[:, g * SSD_N:(g + 1) * SSD_N].astype(BF16)

    gp = SSD_R * SSD_P

    def group_body(g, carry):
        bg = bg_s[g]
        cg = cg_s[g]
        cbt = _dot_nt(bg, cg)
        r0 = pl.multiple_of(g * gp, gp)
        sg = st[pl.ds(r0, gp), :]
        sc = _dot_nt(sg.astype(BF16), cg)
        xg = xt_s[pl.ds(r0, gp), :]
        xw, decs = [], []
        for r in range(SSD_R):
            hd = g * SSD_R + r
            xh = xg[r * SSD_P:(r + 1) * SSD_P]
            cs_i = cs_s[pl.ds(hd, 1), :]
            cs_j = jnp.broadcast_to(cs_i, (q, q)).T
            decay = jnp.exp(jnp.where(seen, cs_i - cs_j, -jnp.inf))
            mt = (cbt * decay).astype(BF16)
            xdt = (xh * dtr_s[pl.ds(hd, 1), :]).astype(BF16)
            yh = _dot(xdt, mt) + sc[r * SSD_P:(r + 1) * SSD_P] * ecs_s[pl.ds(hd, 1), :]
            yb_s[pl.ds(pl.multiple_of(r0 + r * SSD_P, SSD_P), SSD_P), :] = yh
            xw.append((xh * wend_s[pl.ds(hd, 1), :]).astype(BF16))
            decs.append(jnp.broadcast_to(dec_s[pl.ds(hd, 1), :], (SSD_P, SSD_N)))
        upd = _dot(jnp.concatenate(xw, axis=0), bg)
        st[pl.ds(r0, gp), :] = sg * jnp.concatenate(decs, axis=0) + upd
        return carry

    lax.fori_loop(0, g_n, group_body, 0, unroll=True)

    @pl.when(jnp.logical_not(is_bwd))
    def _():
        yf_s[ck] = yb_s[...]

    @pl.when(is_bwd)
    def _():
        yt = yb_s[...] + yf_s[ck] + dsk_ref[...] * xt_s[...]
        yt = yt * _silu(z_ref[...].T)
        y3 = yt.reshape(g_n, gp, q)
        yn = y3 * lax.rsqrt(jnp.mean(y3 * y3, axis=1, keepdims=True) + RMS_EPS)
        y_ref[...] = (yn.reshape(dims.d_ssd, q) * nw_ref[...]).T

    @pl.when(last)
    def _():
        sfin_ref[0, 0] = st[...]


def _ssd_call(dims, proj, dt_t, da_t, dskip, nw, s0, n_seq, seq_len, tok_off):
    d, ds_ = dims.d, dims.d_ssd
    q = SSD_Q
    nchunk = seq_len // q
    off = tok_off // q
    has_init = s0 is not None

    def ck(s):
        return jnp.where(s < nchunk, s, 2 * nchunk - 1 - s)

    def tok(b, s):
        return off + b * nchunk + ck(s)

    def ytok(b, s):
        return b * nchunk + jnp.where(s < nchunk, nchunk - 1, 2 * nchunk - 1 - s)

    in_specs = [pl.BlockSpec((q, ds_), lambda b, s: (tok(b, s), 0)),
                pl.BlockSpec((q, d), lambda b, s: (tok(b, s), ds_ // d)),
                pl.BlockSpec((q, d), lambda b, s: (tok(b, s), ds_ // d + 1)),
                pl.BlockSpec((q, ds_), lambda b, s: (tok(b, s), (ds_ + 2 * d) // ds_)),
                pl.BlockSpec((1, LANE, q), lambda b, s: (tok(b, s), 0, 0)),
                pl.BlockSpec((1, LANE, q), lambda b, s: (tok(b, s), 0, 0)),
                pl.BlockSpec((ds_, q), lambda b, s: (0, 0)),
                pl.BlockSpec((ds_, q), lambda b, s: (0, 0))]
    args = [proj, proj, proj, proj, dt_t, da_t, dskip, nw]
    if has_init:
        in_specs.append(pl.BlockSpec((1, 1, ds_, SSD_N), lambda b, s: (b, s // nchunk, 0, 0)))
        args.append(s0)
    h = dims.h_ssd
    scratch = [pltpu.VMEM((ds_, SSD_N), F32),
               pltpu.VMEM((nchunk, ds_, q), F32),
               pltpu.VMEM((ds_, q), F32),
               pltpu.VMEM((ds_, q), F32),
               pltpu.VMEM((dims.groups, q, SSD_N), BF16),
               pltpu.VMEM((dims.groups, q, SSD_N), BF16),
               pltpu.VMEM((h, q), F32), pltpu.VMEM((h, q), F32), pltpu.VMEM((h, q), F32),
               pltpu.VMEM((h, q), F32), pltpu.VMEM((h, q), F32)]
    vmem = (2 * (3 * q * ds_ + 2 * q * d + 2 * LANE * q + 2 * ds_ * q + 2 * ds_ * SSD_N) * 4
            + (nchunk + 3) * ds_ * q * 4 + (12 << 20))
    return pl.pallas_call(
        functools.partial(_ssd_kernel, dims, nchunk, has_init),
        grid=(n_seq, 2 * nchunk),
        in_specs=in_specs,
        out_specs=(pl.BlockSpec((q, ds_), lambda b, s: (ytok(b, s), 0)),
                   pl.BlockSpec((1, 1, ds_, SSD_N), lambda b, s: (b, s // nchunk, 0, 0))),
        out_shape=(jax.ShapeDtypeStruct((n_seq * seq_len, ds_), F32),
                   jax.ShapeDtypeStruct((n_seq, 2, ds_, SSD_N), F32)),
        scratch_shapes=scratch,
        compiler_params=_cparams(("arbitrary", "arbitrary"), vmem),
        name="ssd_scan",
    )(*args)


def _hgrn_gates(u, lb):
    t = jnp.exp(-jnp.abs(u))
    pos = u >= 0.0
    den = 1.0 + t
    num = jnp.where(pos, 1.0 + lb * t, lb + t)
    log_num = jnp.where(pos | (jnp.broadcast_to(lb, u.shape) > 0.0), jnp.log(num), u)
    return log_num - jnp.log(den), (1.0 - lb) * jnp.where(pos, t, 1.0) / den


def _bcast_block_rows(x, block, row_in_block):
    n = x.shape[0]
    parts = [jnp.broadcast_to(x[p * block + row_in_block:p * block + row_in_block + 1, :], (block, x.shape[1]))
             for p in range(n // block)]
    return parts[0] if len(parts) == 1 else jnp.concatenate(parts, axis=0)


def _keep_half(x, half, keep_hi):
    n = x.shape[0]
    z = jnp.zeros((half, x.shape[1]), x.dtype)
    parts = []
    for p in range(n // (2 * half)):
        lo = x[p * 2 * half:p * 2 * half + half]
        hi = x[p * 2 * half + half:(p + 1) * 2 * half]
        parts += [z, hi] if keep_hi else [lo, z]
    return jnp.concatenate(parts, axis=0)


def _hgrn_kernel(dims, nblk, tb, has_init, *refs):
    if has_init:
        (q_ref, ff_ref, fb_ref, v_ref, g_ref, lb_ref, nw_ref, s0_ref, o_ref, sfin_ref, st, o_s) = refs
    else:
        (q_ref, ff_ref, fb_ref, v_ref, g_ref, lb_ref, nw_ref, o_ref, sfin_ref, st, o_s) = refs
    hh = dims.hh
    c = HG_C
    nck = tb // c
    s = pl.program_id(1)
    is_bwd = s >= nblk
    blk = jnp.where(is_bwd, 2 * nblk - 1 - s, s)
    first = (s == 0) | (s == nblk)
    last = (s == nblk - 1) | (s == 2 * nblk - 1)

    @pl.when(first)
    def _():
        for hd in range(hh):
            st[hd] = s0_ref[0, 0, hd].T if has_init else jnp.zeros((HG_V, HG_K), F32)

    ti = lax.broadcasted_iota(jnp.int32, (c, c), 0)
    tj = lax.broadcasted_iota(jnp.int32, (c, c), 1)
    le = (tj <= ti)
    ge = (tj >= ti)
    ltri = le.astype(BF16)
    utri = ge.astype(BF16)
    same_diag = (ti // HG_DIAG) == (tj // HG_DIAG)
    mask_df = (same_diag & le).astype(F32)
    mask_db = (same_diag & ge).astype(F32)
    parent = {hf: ((ti // (2 * hf)) == (tj // (2 * hf))).astype(F32) for hf in HG_LEVELS if 2 * hf < c}
    lb_f = lb_ref[0:1, :]
    lb_b = lb_ref[1:2, :]

    def heads(x):
        return [x[:, hd * HG_K:(hd + 1) * HG_K] for hd in range(hh)]

    def state_step(hd, qs_h, ke_h, v_h, dec_h):
        sh = st[hd]
        off = _dot_nt(qs_h.astype(BF16), sh.astype(BF16))
        st[hd] = sh * dec_h + _dot_tn(v_h.astype(BF16), ke_h.astype(BF16))
        return off

    def fwd_chunk(idx, carry):
        r0 = pl.multiple_of(idx * c, c)
        rows = pl.ds(r0, c)
        qq = _silu(q_ref[rows, :])
        vv = v_ref[rows, :]
        lf_f, k_f = _hgrn_gates(ff_ref[rows, :], lb_f)
        lf_b, k_b = _hgrn_gates(fb_ref[rows, :], lb_b)
        b_f = _sum01_left(ltri, lf_f)
        r_b = _sum01_left(utri, lf_b)
        lv_q, lv_k = [], []
        for hf in HG_LEVELS:
            x_f = jnp.exp(-jnp.abs(b_f - _bcast_block_rows(b_f, 2 * hf, hf - 1)))
            x_b = jnp.exp(-jnp.abs(r_b - _bcast_block_rows(r_b, 2 * hf, hf)))
            qf = _keep_half(qq * x_f, hf, True).astype(BF16)
            kf = _keep_half(k_f * x_f, hf, False).astype(BF16)
            qb = _keep_half(qq * x_b, hf, False).astype(BF16)
            kb = _keep_half(k_b * x_b, hf, True).astype(BF16)
            lv_q.append((heads(qf), heads(qb)))
            lv_k.append((heads(kf), heads(kb)))
        dm_f = b_f - _bcast_block_rows(b_f, HG_DIAG, HG_DIAG // 2 - 1)
        dm_b = r_b - _bcast_block_rows(r_b, HG_DIAG, HG_DIAG // 2)
        qd_f = heads((qq * jnp.exp(dm_f)).astype(BF16))
        kd_f = heads((k_f * jnp.exp(-dm_f)).astype(BF16))
        qd_b = heads((qq * jnp.exp(dm_b)).astype(BF16))
        kd_b = heads((k_b * jnp.exp(-dm_b)).astype(BF16))
        qs = heads(qq * jnp.exp(b_f))
        ke = heads(k_f * jnp.exp(b_f[c - 1:c, :] - b_f))
        dec = heads(jnp.exp(b_f[c - 1:c, :]))
        vh = heads(vv)
        outs = []
        for hd in range(hh):
            att = _dot_nt(qd_f[hd], kd_f[hd]) * mask_df + _dot_nt(qd_b[hd], kd_b[hd]) * mask_db
            for li, hf in enumerate(HG_LEVELS):
                p = _dot_nt(jnp.concatenate([lv_q[li][0][hd], lv_q[li][1][hd]], axis=1),
                            jnp.concatenate([lv_k[li][0][hd], lv_k[li][1][hd]], axis=1))
                att = att + (p * parent[hf] if hf in parent else p)
            o_h = _dot(att.astype(BF16), vh[hd].astype(BF16))
            outs.append(o_h + state_step(hd, qs[hd], ke[hd], vh[hd], dec[hd]))
        o_s[pl.ds(pl.multiple_of(blk * tb + r0, c), c), :] = jnp.concatenate(outs, axis=1)
        return carry

    def bwd_chunk(idx, carry):
        r0 = pl.multiple_of((nck - 1 - idx) * c, c)
        rows = pl.ds(r0, c)
        qq = _silu(q_ref[rows, :])
        vh = heads(v_ref[rows, :])
        lf_b, k_b = _hgrn_gates(fb_ref[rows, :], lb_b)
        r_b = _sum01_left(utri, lf_b)
        qs = heads(qq * jnp.exp(r_b))
        ke = heads(k_b * jnp.exp(r_b[0:1, :] - r_b))
        dec = heads(jnp.exp(r_b[0:1, :]))
        o_prev = heads(o_s[pl.ds(pl.multiple_of(blk * tb + r0, c), c), :])
        gate = heads(_silu(g_ref[rows, :]))
        nw = heads(nw_ref[...])
        outs = []
        for hd in range(hh):
            o_h = o_prev[hd] + state_step(hd, qs[hd], ke[hd], vh[hd], dec[hd])
            o_h = o_h * lax.rsqrt(jnp.mean(o_h * o_h, axis=-1, keepdims=True) + RMS_EPS) * nw[hd]
            outs.append(o_h * gate[hd])
        o_ref[rows, :] = jnp.concatenate(outs, axis=1)
        return carry

    @pl.when(jnp.logical_not(is_bwd))
    def _():
        lax.fori_loop(0, nck, fwd_chunk, 0)

    @pl.when(is_bwd)
    def _():
        lax.fori_loop(0, nck, bwd_chunk, 0)

    @pl.when(last)
    def _():
        for hd in range(hh):
            sfin_ref[0, 0, hd] = st[hd].T


def _hgrn_call(dims, proj, lb, nw, s0, n_seq, seq_len, tok_off):
    d, hh = dims.d, dims.hh
    tb = min(512, seq_len)
    nblk = seq_len // tb
    off = tok_off // tb
    has_init = s0 is not None
    col0 = (dims.d_ssd + 2 * d + dims.d_ssd) // d

    def tok(b, s):
        return off + b * nblk + jnp.where(s < nblk, s, 2 * nblk - 1 - s)

    def otok(b, s):
        return b * nblk + jnp.where(s < nblk, nblk - 1, 2 * nblk - 1 - s)

    in_specs = [pl.BlockSpec((tb, d), functools.partial(lambda cc, b, s: (tok(b, s), cc), col0 + k)) for k in range(5)]
    in_specs += [pl.BlockSpec((8, d), lambda b, s: (0, 0)), pl.BlockSpec((1, d), lambda b, s: (0, 0))]
    args = [proj] * 5 + [lb, nw]
    if has_init:
        in_specs.append(pl.BlockSpec((1, 1, hh, HG_K, HG_V), lambda b, s: (b, s // nblk, 0, 0, 0)))
        args.append(s0)
    vmem = 2 * 6 * tb * d * 4 + seq_len * d * 4 + 4 * hh * HG_K * HG_V * 4 + (16 << 20)
    return pl.pallas_call(
        functools.partial(_hgrn_kernel, dims, nblk, tb, has_init),
        grid=(n_seq, 2 * nblk),
        in_specs=in_specs,
        out_specs=(pl.BlockSpec((tb, d), lambda b, s: (otok(b, s), 0)),
                   pl.BlockSpec((1, 1, hh, HG_K, HG_V), lambda b, s: (b, s // nblk, 0, 0, 0))),
        out_shape=(jax.ShapeDtypeStruct((n_seq * seq_len, d), F32),
                   jax.ShapeDtypeStruct((n_seq, 2, hh, HG_K, HG_V), F32)),
        scratch_shapes=[pltpu.VMEM((hh, HG_V, HG_K), F32), pltpu.VMEM((seq_len, d), F32)],
        compiler_params=_cparams(("arbitrary", "arbitrary"), vmem),
        name="hgrn_scan",
    )(*args)


def _merge_kernel(dims, tm, yc_ref, yl_ref, oc_ref, ol_ref, g1_ref, g2_ref, x_ref, mod_ref,
                  wso_ref, who_ref, wo_ref, out_ref):
    d = dims.d
    is_ctx = pl.program_id(0) < dims.t_ctx // tm
    y = jnp.where(is_ctx, yc_ref[...], yl_ref[...])
    o = jnp.where(is_ctx, oc_ref[...], ol_ref[...])
    y_ssd = _dot(y.astype(BF16), wso_ref[...])
    y_hg = _dot(o.astype(BF16), who_ref[...])
    merged = _sigmoid(g1_ref[...]) * y_ssd + _sigmoid(g2_ref[...]) * y_hg
    mix = _dot(merged.astype(BF16), wo_ref[...])
    out_ref[...] = x_ref[...] + mod_ref[0][:, 2 * d:3 * d] * mix


def _merge_call(dims, y_ctx, y_lat, o_ctx, o_lat, proj, x, mod_l, wso, who, wo):
    d, ds_, t = dims.d, dims.d_ssd, dims.t
    tm = min(512, dims.row_tile)
    nctx = dims.t_ctx // tm
    gcol = proj.shape[1] // d - 2

    def ctx_blk(i):
        return jnp.minimum(i, nctx - 1)

    def lat_blk(i):
        return jnp.maximum(i - nctx, 0)

    const = lambda i: (0, 0)
    return pl.pallas_call(
        functools.partial(_merge_kernel, dims, tm),
        grid=(t // tm,),
        in_specs=[pl.BlockSpec((tm, ds_), lambda i: (ctx_blk(i), 0)),
                  pl.BlockSpec((tm, ds_), lambda i: (lat_blk(i), 0)),
                  pl.BlockSpec((tm, d), lambda i: (ctx_blk(i), 0)),
                  pl.BlockSpec((tm, d), lambda i: (lat_blk(i), 0)),
                  pl.BlockSpec((tm, d), lambda i: (i, gcol)),
                  pl.BlockSpec((tm, d), lambda i: (i, gcol + 1)),
                  pl.BlockSpec((tm, d), lambda i: (i, 0)),
                  pl.BlockSpec((1, 1, 6 * d), lambda i: (_mod_row(i, tm, dims), 0, 0)),
                  pl.BlockSpec((ds_, d), const), pl.BlockSpec((d, d), const), pl.BlockSpec((d, d), const)],
        out_specs=pl.BlockSpec((tm, d), lambda i: (i, 0)),
        out_shape=jax.ShapeDtypeStruct((t, d), F32),
        compiler_params=_cparams(("arbitrary",), 2 * (2 * tm * ds_ + 6 * tm * d) * 4 + 2 * (ds_ + 2 * d) * d * 2 + (12 << 20)),
        name="merge_outproj",
    )(y_ctx, y_lat, o_ctx, o_lat, proj, proj, x, mod_l, wso, who, wo)


def _ffn_kernel(dims, x_ref, mod_ref, nw_ref, w1_ref, w3_ref, w2_ref, out_ref, h_s, acc_s):
    d = dims.d
    j = pl.program_id(1)

    @pl.when(j == 0)
    def _():
        x = x_ref[...]
        y = x * lax.rsqrt(jnp.mean(x * x, axis=-1, keepdims=True) + RMS_EPS) * nw_ref[...]
        mod = mod_ref[0]
        h_s[...] = (y * (1.0 + mod[:, 4 * d:5 * d]) + mod[:, 3 * d:4 * d]).astype(BF16)
        acc_s[...] = jnp.zeros(acc_s.shape, F32)

    hb = h_s[...]
    act = _silu(_dot(hb, w1_ref[...])) * _dot(hb, w3_ref[...])
    acc_s[...] += _dot(act.astype(BF16), w2_ref[...])

    @pl.when(j == pl.num_programs(1) - 1)
    def _():
        out_ref[...] = x_ref[...] + mod_ref[0][:, 5 * d:6 * d] * acc_s[...]


def _ffn_call(dims, x, mod_l, nw, w1, w3, w2):
    d, t, f = dims.d, dims.t, dims.ffn
    tm = min(512, dims.row_tile)
    th = f // 2 if (f // 2) % LANE == 0 and f % 2 == 0 else f
    return pl.pallas_call(
        functools.partial(_ffn_kernel, dims),
        grid=(t // tm, f // th),
        in_specs=[pl.BlockSpec((tm, d), lambda i, j: (i, 0)),
                  pl.BlockSpec((1, 1, 6 * d), lambda i, j: (_mod_row(i, tm, dims), 0, 0)),
                  pl.BlockSpec((1, d), lambda i, j: (0, 0)),
                  pl.BlockSpec((d, th), lambda i, j: (0, j)),
                  pl.BlockSpec((d, th), lambda i, j: (0, j)),
                  pl.BlockSpec((th, d), lambda i, j: (j, 0))],
        out_specs=pl.BlockSpec((tm, d), lambda i, j: (i, 0)),
        out_shape=jax.ShapeDtypeStruct((t, d), F32),
        scratch_shapes=[pltpu.VMEM((tm, d), BF16), pltpu.VMEM((tm, d), F32)],
        compiler_params=_cparams(("arbitrary", "arbitrary"), 4 * tm * d * 4 + 2 * 3 * d * th * 2 + 4 * tm * th * 4 + (12 << 20)),
        name="ffn",
    )(x, mod_l, nw, w1, w3, w2)


def _final_norm_kernel(x_ref, w_ref, o_ref):
    x = x_ref[...]
    o_ref[...] = x * lax.rsqrt(jnp.mean(x * x, axis=-1, keepdims=True) + RMS_EPS) * w_ref[...]


def _final_norm_call(dims, x, w):
    d, t = dims.d, dims.t
    tm = dims.row_tile
    return pl.pallas_call(
        _final_norm_kernel,
        grid=(t // tm,),
        in_specs=[pl.BlockSpec((tm, d), lambda i: (i, 0)), pl.BlockSpec((1, d), lambda i: (0, 0))],
        out_specs=pl.BlockSpec((tm, d), lambda i: (i, 0)),
        out_shape=jax.ShapeDtypeStruct((t, d), F32),
        compiler_params=_cparams(("arbitrary",), 4 * tm * d * 4 + (8 << 20)),
        name="final_norm",
    )(x, w)


def _forward(dims, x_prompt, x_sample, c, state_ssm, state_hgrn, c_ctx, norm_mix_w, norm_ffn_w, ada_w, ada_b, w_in,
             ssd_conv_w, ssd_conv_b, ssd_dt_bias, ssd_a_log, ssd_d, ssd_norm_w, hgrn_lower_bounds, hgrn_norm_w,
             w_ssd_out, w_hgrn_out, w_out, ffn_w1, ffn_w3, ffn_w2, norm_final_w):
    d, ds_, h = dims.d, dims.d_ssd, dims.h_ssd
    assert dims.db + 1 <= MOD_ROWS and 2 * h <= LANE and dims.nl % HG_C == 0 and dims.dl % 512 == 0
    conv_dim = ds_ + 2 * dims.groups * SSD_N
    x = jnp.concatenate([x_prompt.reshape(dims.t_ctx, d), x_sample.reshape(dims.t_lat, d)], axis=0)
    cvec = jnp.zeros((MOD_ROWS, d), F32).at[0].set(c_ctx).at[1:1 + dims.db].set(c)
    mod = _mod_call(dims, cvec, ada_w, ada_b)
    lb_all = jnp.cumsum(jax.nn.softmax(hgrn_lower_bounds.astype(F32), axis=0), axis=0)
    lb_all = lb_all - lb_all[:1]
    dt_cols = slice(conv_dim + ds_, conv_dim + ds_ + 2 * h)

    ssm_states, hgrn_states = [], []
    for l in range(dims.depth):
        mod_l = mod[l].reshape(MOD_ROWS, 1, 6 * d)
        w_l = w_in[l]
        w_main = jnp.concatenate([w_l[:, :dt_cols.start], w_l[:, dt_cols.stop:]], axis=1).astype(BF16)
        wdt_t = jnp.zeros((LANE, d), F32).at[:2 * h].set(w_l[:, dt_cols].T).astype(BF16)
        dtb = jnp.zeros((LANE, 1), F32).at[:2 * h, 0].set(ssd_dt_bias[l].reshape(-1))
        alog = jnp.zeros((LANE, 1), F32).at[:2 * h, 0].set(ssd_a_log[l].reshape(-1))
        proj, dt_t, da_t = _inproj_call(dims, x, mod_l, norm_mix_w[l].reshape(1, d), w_main,
                                        ssd_conv_w[l], ssd_conv_b[l].reshape(1, conv_dim), wdt_t, dtb, alog)
        dskip = jnp.broadcast_to(jnp.repeat(ssd_d[l], SSD_P)[:, None], (ds_, SSD_Q))
        ssd_nw = jnp.broadcast_to(ssd_norm_w[l][:, None], (ds_, SSD_Q))
        y_ctx, s_ssm = _ssd_call(dims, proj, dt_t, da_t, dskip, ssd_nw, None, dims.nb, dims.nl, 0)
        y_lat, _ = _ssd_call(dims, proj, dt_t, da_t, dskip, ssd_nw,
                             state_ssm[:, l].reshape(dims.db, 2, ds_, SSD_N), dims.db, dims.dl, dims.t_ctx)
        lb = jnp.zeros((8, d), F32).at[:2].set(lb_all[l])
        hg_nw = hgrn_norm_w[l].reshape(1, d)
        o_ctx, s_hg = _hgrn_call(dims, proj, lb, hg_nw, None, dims.nb, dims.nl, 0)
        o_lat, _ = _hgrn_call(dims, proj, lb, hg_nw, state_hgrn[:, l], dims.db, dims.dl, dims.t_ctx)
        x = _merge_call(dims, y_ctx, y_lat, o_ctx, o_lat, proj, x, mod_l,
                        w_ssd_out[l].astype(BF16), w_hgrn_out[l].astype(BF16), w_out[l].astype(BF16))
        x = _ffn_call(dims, x, mod_l, norm_ffn_w[l].reshape(1, d),
                      ffn_w1[l].astype(BF16), ffn_w3[l].astype(BF16), ffn_w2[l].astype(BF16))
        ssm_states.append(s_ssm.reshape(dims.nb, 2, h, SSD_P, SSD_N))
        hgrn_states.append(s_hg)
    xn = _final_norm_call(dims, x, norm_final_w.reshape(1, d))
    y_prompt = xn[:dims.t_ctx].reshape(dims.nb, dims.nl, d)
    y_sample = xn[dims.t_ctx:].reshape(dims.db, dims.dl, d)
    return y_prompt, y_sample, jnp.stack(ssm_states, axis=1), jnp.stack(hgrn_states, axis=1)


def kernel(x_prompt, x_sample, c, state_ssm, state_hgrn, c_ctx, norm_mix_w, norm_ffn_w, ada_w, ada_b, w_in,
           ssd_conv_w, ssd_conv_b, ssd_dt_bias, ssd_a_log, ssd_d, ssd_norm_w, hgrn_lower_bounds, hgrn_norm_w,
           w_ssd_out, w_hgrn_out, w_out, ffn_w1, ffn_w3, ffn_w2, norm_final_w):
    dims = Dims(d=x_prompt.shape[2], nb=x_prompt.shape[0], nl=x_prompt.shape[1], db=x_sample.shape[0],
                dl=x_sample.shape[1], depth=w_in.shape[0], ffn=ffn_w1.shape[2])
    return _forward(dims, x_prompt, x_sample, c, state_ssm, state_hgrn, c_ctx, norm_mix_w, norm_ffn_w, ada_w, ada_b,
                    w_in, ssd_conv_w, ssd_conv_b, ssd_dt_bias, ssd_a_log, ssd_d, ssd_norm_w, hgrn_lower_bounds,
                    hgrn_norm_w, w_ssd_out, w_hgrn_out, w_out, ffn_w1, ffn_w3, ffn_w2, norm_final_w)
```

```python
import functools
from typing import NamedTuple

import jax
import jax.numpy as jnp
from jax import lax
from jax.experimental import pallas as pl
from jax.experimental.pallas import tpu as pltpu

F32 = jnp.float32
BF16 = jnp.bfloat16

LANE = 128
MXU_COLS = 256
VMEM_BYTES = 64 * 1024 * 1024
RMS_EPS = 1e-6
D_CONV = 5
GRID_W = 64
SSD_P = 64
SSD_N = 128
SSD_Q = 128
SSD_R = 4
HG_K = 128
HG_V = 128
HG_C = 128
HG_LEVELS = (64, 32, 16)
HG_DIAG = 16
MOD_ROWS = 8


class Dims(NamedTuple):
    d: int
    nb: int
    nl: int
    db: int
    dl: int
    depth: int
    ffn: int

    @property
    def t_ctx(self):
        return self.nb * self.nl

    @property
    def t_lat(self):
        return self.db * self.dl

    @property
    def t(self):
        return self.t_ctx + self.t_lat

    @property
    def d_ssd(self):
        return 2 * self.d

    @property
    def h_ssd(self):
        return self.d_ssd // SSD_P

    @property
    def groups(self):
        return self.h_ssd // SSD_R

    @property
    def hh(self):
        return self.d // HG_K

    @property
    def row_tile(self):
        tm = min(1024, self.dl, self.t_ctx)
        assert self.dl % tm == 0 and self.t_ctx % tm == 0 and tm % self.nl == 0 and tm % GRID_W == 0
        return tm


def _cparams(semantics, vmem_bytes):
    return pltpu.CompilerParams(dimension_semantics=semantics, vmem_limit_bytes=min(vmem_bytes, VMEM_BYTES - (4 << 20)))


def _dot(a, b):
    return jnp.dot(a, b, preferred_element_type=F32)


def _dot_nt(a, b):
    return lax.dot_general(a, b, (((1,), (1,)), ((), ())), preferred_element_type=F32)


def _dot_tn(a, b):
    return lax.dot_general(a, b, (((0,), (0,)), ((), ())), preferred_element_type=F32)


def _split3(x):
    hi = x.astype(BF16)
    r = x - hi.astype(F32)
    mid = r.astype(BF16)
    lo = (r - mid.astype(F32)).astype(BF16)
    return hi, mid, lo


def _sum01_right(x, m01):
    hi, mid, lo = _split3(x)
    return _dot(hi, m01) + _dot(mid, m01) + _dot(lo, m01)


def _sigmoid(x):
    return 0.5 * jnp.tanh(0.5 * x) + 0.5


def _silu(x):
    return x * _sigmoid(x)


def _softplus(x):
    return jnp.maximum(x, 0.0) + jnp.log1p(jnp.exp(-jnp.abs(x)))


def _mod_row(i, tm, dims):
    nctx = dims.t_ctx // tm
    return jnp.where(i < nctx, 0, 1 + ((i - nctx) * tm) // dims.dl)


def _mod_kernel(c_ref, w_ref, b_ref, o_ref):
    s = _silu(c_ref[...])
    o_ref[0] = _dot(s.astype(BF16), w_ref[0].astype(BF16)) + b_ref[0]


def _mod_call(dims, cvec, ada_w, ada_b):
    d, n = dims.d, 6 * dims.d
    tn = n // 4
    return pl.pallas_call(
        _mod_kernel,
        grid=(dims.depth, n // tn),
        in_specs=[pl.BlockSpec((MOD_ROWS, d), lambda l, j: (0, 0)),
                  pl.BlockSpec((1, d, tn), lambda l, j: (l, 0, j)),
                  pl.BlockSpec((1, 1, tn), lambda l, j: (l, 0, j))],
        out_specs=pl.BlockSpec((1, MOD_ROWS, tn), lambda l, j: (l, 0, j)),
        out_shape=jax.ShapeDtypeStruct((dims.depth, MOD_ROWS, n), F32),
        compiler_params=_cparams(("arbitrary", "arbitrary"), 2 * d * tn * 4 + (8 << 20)),
        name="adaln_mod",
    )(cvec, ada_w, ada_b.reshape(dims.depth, 1, n))


def _inproj_kernel(dims, tm, x_ref, mod_ref, nw_ref, w_ref, cw_ref, cb_ref, wdt_ref, dtb_ref, alog_ref,
                   proj_ref, dt_ref, da_ref, h_s):
    d = dims.d
    i = pl.program_id(0)
    j = pl.program_id(1)
    n_conv = (dims.d_ssd + 2 * dims.groups * SSD_N) // d

    @pl.when(j == 0)
    def _():
        x = x_ref[...]
        y = x * lax.rsqrt(jnp.mean(x * x, axis=-1, keepdims=True) + RMS_EPS) * nw_ref[...]
        mod = mod_ref[0]
        hb = (y * (1.0 + mod[:, d:2 * d]) + mod[:, 0:d]).astype(BF16)
        h_s[...] = hb
        dt = _softplus(_dot_nt(wdt_ref[...], hb) + dtb_ref[...])
        da = dt * -jnp.exp(alog_ref[...])
        for k in range(tm // SSD_Q):
            dt_ref[k] = dt[:, k * SSD_Q:(k + 1) * SSD_Q]
            da_ref[k] = da[:, k * SSD_Q:(k + 1) * SSD_Q]

    @pl.when(j >= n_conv)
    def _():
        proj_ref[...] = _dot(h_s[...], w_ref[...])

    def conv_silu(seg):
        ns, half, sub = tm // seg, D_CONV // 2, 8
        n8 = ns * sub
        r8 = lax.broadcasted_iota(jnp.int32, (n8, LANE), 0) & (sub - 1)
        res = None
        for sl in range(d // LANE):
            cols = slice(sl * LANE, (sl + 1) * LANE)
            if sl % (MXU_COLS // LANE) == 0:
                res = _dot(h_s[...], w_ref[:, sl * LANE:sl * LANE + MXU_COLS])
            u = res[:, (sl * LANE) % MXU_COLS:(sl * LANE) % MXU_COLS + LANE]
            w = [cw_ref[k:k + 1, cols] for k in range(D_CONV)]
            acc = cb_ref[:, cols] + w[half] * u
            for k in range(D_CONV):
                if k != half:
                    acc = acc + w[k] * pltpu.roll(u, (half - k) % tm, 0)
            u3 = u.reshape(ns, seg, LANE)
            a3 = acc.reshape(ns, seg, LANE)
            edge_lo = u3[:, 0:sub, :].reshape(n8, LANE)
            edge_hi = u3[:, seg - sub:seg, :].reshape(n8, LANE)
            leak_lo = (w[0] * jnp.where(r8 < 2, pltpu.roll(edge_hi, 2, 0), 0.0)
                       + w[1] * jnp.where(r8 < 1, pltpu.roll(edge_hi, 1, 0), 0.0))
            leak_hi = (w[4] * jnp.where(r8 >= sub - 2, pltpu.roll(edge_lo, n8 - 2, 0), 0.0)
                       + w[3] * jnp.where(r8 >= sub - 1, pltpu.roll(edge_lo, n8 - 1, 0), 0.0))
            a3 = jnp.concatenate([a3[:, 0:sub, :] - leak_lo.reshape(ns, sub, LANE),
                                  a3[:, sub:seg - sub, :],
                                  a3[:, seg - sub:seg, :] - leak_hi.reshape(ns, sub, LANE)], axis=1)
            proj_ref[:, cols] = _silu(a3.reshape(tm, LANE))

    is_ctx = i < dims.t_ctx // tm

    @pl.when((j < n_conv) & is_ctx)
    def _():
        conv_silu(dims.nl)

    @pl.when((j < n_conv) & jnp.logical_not(is_ctx))
    def _():
        conv_silu(GRID_W)


def _inproj_call(dims, x, mod_l, nw, w, cw, cb, wdt_t, dtb, alog):
    d, t, tm = dims.d, dims.t, dims.row_tile
    ncol = w.shape[1]
    n_conv = (dims.d_ssd + 2 * dims.groups * SSD_N) // d
    nchunk = t // SSD_Q
    out_shapes = (jax.ShapeDtypeStruct((t, ncol), F32),
                  jax.ShapeDtypeStruct((nchunk, LANE, SSD_Q), F32),
                  jax.ShapeDtypeStruct((nchunk, LANE, SSD_Q), F32))
    kc = tm // SSD_Q
    return pl.pallas_call(
        functools.partial(_inproj_kernel, dims, tm),
        grid=(t // tm, ncol // d),
        in_specs=[pl.BlockSpec((tm, d), lambda i, j: (i, 0)),
                  pl.BlockSpec((1, 1, 6 * d), lambda i, j: (_mod_row(i, tm, dims), 0, 0)),
                  pl.BlockSpec((1, d), lambda i, j: (0, 0)),
                  pl.BlockSpec((d, d), lambda i, j: (0, j)),
                  pl.BlockSpec((D_CONV, d), lambda i, j: (0, jnp.minimum(j, n_conv - 1))),
                  pl.BlockSpec((1, d), lambda i, j: (0, jnp.minimum(j, n_conv - 1))),
                  pl.BlockSpec((LANE, d), lambda i, j: (0, 0)),
                  pl.BlockSpec((LANE, 1), lambda i, j: (0, 0)),
                  pl.BlockSpec((LANE, 1), lambda i, j: (0, 0))],
        out_specs=(pl.BlockSpec((tm, d), lambda i, j: (i, j)),
                   pl.BlockSpec((kc, LANE, SSD_Q), lambda i, j: (i, 0, 0)),
                   pl.BlockSpec((kc, LANE, SSD_Q), lambda i, j: (i, 0, 0))),
        out_shape=out_shapes,
        scratch_shapes=[pltpu.VMEM((tm, d), BF16)],
        compiler_params=_cparams(("arbitrary", "arbitrary"), 6 * tm * d * 4 + 2 * d * d * 2 + tm * d * 2 + (8 << 20)),
        name="inproj",
    )(x, mod_l, nw, w, cw, cb, wdt_t, dtb, alog)


def _ssd_kernel(dims, nchunk, has_init, *refs):
    if has_init:
        (x_ref, b_ref, c_ref, z_ref, dt_ref, da_ref, dsk_ref, nw_ref, s0_ref,
         y_ref, sfin_ref, st, yf_s, xt_s, yb_s, bg_s, cg_s, cs_s, ecs_s, wend_s, dec_s, dtr_s) = refs
    else:
        (x_ref, b_ref, c_ref, z_ref, dt_ref, da_ref, dsk_ref, nw_ref,
         y_ref, sfin_ref, st, yf_s, xt_s, yb_s, bg_s, cg_s, cs_s, ecs_s, wend_s, dec_s, dtr_s) = refs
    h, g_n = dims.h_ssd, dims.groups
    q = SSD_Q
    s = pl.program_id(1)
    is_bwd = s >= nchunk
    ck = jnp.where(is_bwd, 2 * nchunk - 1 - s, s)
    first = (s == 0) | (s == nchunk)
    last = (s == nchunk - 1) | (s == 2 * nchunk - 1)

    @pl.when(first)
    def _():
        st[...] = s0_ref[0, 0] if has_init else jnp.zeros(st.shape, F32)

    row0 = pl.multiple_of(jnp.where(is_bwd, h, 0), 8)
    dtr = dt_ref[0, pl.ds(row0, h), :]
    dar = da_ref[0, pl.ds(row0, h), :]
    jj = lax.broadcasted_iota(jnp.int32, (q, q), 0)
    ii = lax.broadcasted_iota(jnp.int32, (q, q), 1)
    seen = jnp.where(is_bwd, jj - ii, ii - jj) >= 0
    cs = _sum01_right(dar, seen.astype(BF16))
    tot = _sum01_right(dar, jnp.ones((q, q), BF16))
    cs_s[...] = cs
    ecs_s[...] = jnp.exp(cs)
    wend_s[...] = jnp.exp(tot - cs) * dtr
    dec_s[...] = jnp.exp(tot)
    dtr_s[...] = dtr
    xt_s[...] = x_ref[...].T
    for g in range(g_n):
        bg_s[g] = b_ref[:, g * SSD_N:(g + 1) * SSD_N].astype(BF16)
        cg_s[g] = c_ref[:, g * SSD_N:(g + 1) * SSD_N].astype(BF16)

    gp = SSD_R * SSD_P

    def group_body(g, carry):
        bg = bg_s[g]
        cg = cg_s[g]
        cbt = _dot_nt(bg, cg)
        r0 = pl.multiple_of(g * gp, gp)
        sg = st[pl.ds(r0, gp), :]
        sc = _dot_nt(sg.astype(BF16), cg)
        xg = xt_s[pl.ds(r0, gp), :]
        xw, decs = [], []
        for r in range(SSD_R):
            hd = g * SSD_R + r
            xh = xg[r * SSD_P:(r + 1) * SSD_P]
            cs_i = cs_s[pl.ds(hd, 1), :]
            cs_j = jnp.broadcast_to(cs_i, (q, q)).T
            decay = jnp.exp(jnp.where(seen, cs_i - cs_j, -jnp.inf))
            mt = (cbt * decay).astype(BF16)
            xdt = (xh * dtr_s[pl.ds(hd, 1), :]).astype(BF16)
            yh = _dot(xdt, mt) + sc[r * SSD_P:(r + 1) * SSD_P] * ecs_s[pl.ds(hd, 1), :]
            yb_s[pl.ds(pl.multiple_of(r0 + r * SSD_P, SSD_P), SSD_P), :] = yh
            xw.append((xh * wend_s[pl.ds(hd, 1), :]).astype(BF16))
            decs.append(jnp.broadcast_to(dec_s[pl.ds(hd, 1), :], (SSD_P, SSD_N)))
        upd = _dot(jnp.concatenate(xw, axis=0), bg)
        st[pl.ds(r0, gp), :] = sg * jnp.concatenate(decs, axis=0) + upd
        return carry

    lax.fori_loop(0, g_n, group_body, 0, unroll=True)

    @pl.when(jnp.logical_not(is_bwd))
    def _():
        yf_s[ck] = yb_s[...]

    @pl.when(is_bwd)
    def _():
        yt = yb_s[...] + yf_s[ck] + dsk_ref[...] * xt_s[...]
        yt = yt * _silu(z_ref[...].T)
        y3 = yt.reshape(g_n, gp, q)
        yn = y3 * lax.rsqrt(jnp.mean(y3 * y3, axis=1, keepdims=True) + RMS_EPS)
        y_ref[...] = (yn.reshape(dims.d_ssd, q) * nw_ref[...]).T

    @pl.when(last)
    def _():
        sfin_ref[0, 0] = st[...]


def _ssd_call(dims, proj, dt_t, da_t, dskip, nw, s0, n_seq, seq_len, tok_off):
    d, ds_ = dims.d, dims.d_ssd
    q = SSD_Q
    nchunk = seq_len // q
    off = tok_off // q
    has_init = s0 is not None

    def ck(s):
        return jnp.where(s < nchunk, s, 2 * nchunk - 1 - s)

    def tok(b, s):
        return off + b * nchunk + ck(s)

    def ytok(b, s):
        return b * nchunk + jnp.where(s < nchunk, nchunk - 1, 2 * nchunk - 1 - s)

    in_specs = [pl.BlockSpec((q, ds_), lambda b, s: (tok(b, s), 0)),
                pl.BlockSpec((q, d), lambda b, s: (tok(b, s), ds_ // d)),
                pl.BlockSpec((q, d), lambda b, s: (tok(b, s), ds_ // d + 1)),
                pl.BlockSpec((q, ds_), lambda b, s: (tok(b, s), (ds_ + 2 * d) // ds_)),
                pl.BlockSpec((1, LANE, q), lambda b, s: (tok(b, s), 0, 0)),
                pl.BlockSpec((1, LANE, q), lambda b, s: (tok(b, s), 0, 0)),
                pl.BlockSpec((ds_, q), lambda b, s: (0, 0)),
                pl.BlockSpec((ds_, q), lambda b, s: (0, 0))]
    args = [proj, proj, proj, proj, dt_t, da_t, dskip, nw]
    if has_init:
        in_specs.append(pl.BlockSpec((1, 1, ds_, SSD_N), lambda b, s: (b, s // nchunk, 0, 0)))
        args.append(s0)
    h = dims.h_ssd
    scratch = [pltpu.VMEM((ds_, SSD_N), F32),
               pltpu.VMEM((nchunk, ds_, q), F32),
               pltpu.VMEM((ds_, q), F32),
               pltpu.VMEM((ds_, q), F32),
               pltpu.VMEM((dims.groups, q, SSD_N), BF16),
               pltpu.VMEM((dims.groups, q, SSD_N), BF16),
               pltpu.VMEM((h, q), F32), pltpu.VMEM((h, q), F32), pltpu.VMEM((h, q), F32),
               pltpu.VMEM((h, q), F32), pltpu.VMEM((h, q), F32)]
    vmem = (2 * (3 * q * ds_ + 2 * q * d + 2 * LANE * q + 2 * ds_ * q + 2 * ds_ * SSD_N) * 4
            + (nchunk + 3) * ds_ * q * 4 + (12 << 20))
    return pl.pallas_call(
        functools.partial(_ssd_kernel, dims, nchunk, has_init),
        grid=(n_seq, 2 * nchunk),
        in_specs=in_specs,
        out_specs=(pl.BlockSpec((q, ds_), lambda b, s: (ytok(b, s), 0)),
                   pl.BlockSpec((1, 1, ds_, SSD_N), lambda b, s: (b, s // nchunk, 0, 0))),
        out_shape=(jax.ShapeDtypeStruct((n_seq * seq_len, ds_), F32),
                   jax.ShapeDtypeStruct((n_seq, 2, ds_, SSD_N), F32)),
        scratch_shapes=scratch,
        compiler_params=_cparams(("arbitrary", "arbitrary"), vmem),
        name="ssd_scan",
    )(*args)


def _hgrn_gates(u, lb):
    t = jnp.exp(-jnp.abs(u))
    pos = u >= 0.0
    den = 1.0 + t
    num = jnp.where(pos, 1.0 + lb * t, lb + t)
    log_num = jnp.where(pos | (jnp.broadcast_to(lb, u.shape) > 0.0), jnp.log(num), u)
    return log_num - jnp.log(den), (1.0 - lb) * jnp.where(pos, t, 1.0) / den


def _hg_dir_operands(lam, kk, qq, blockmat, fwd):
    c, w = lam.shape
    bs, nb = HG_DIAG, lam.shape[0] // HG_DIAG
    lam_hi = lam.astype(BF16)
    lam_lo = (lam - lam_hi.astype(F32)).astype(BF16)
    p = _dot(blockmat, lam_hi) + _dot(blockmat, lam_lo)
    edge, mid = (bs - 1, bs // 2 - 1) if fwd else (0, bs // 2)
    tot = [p[b * bs + edge:b * bs + edge + 1] for b in range(nb)]
    p_mid = [p[b * bs + mid:b * bs + mid + 1] for b in range(nb)]

    def spread(rows):
        return jnp.concatenate([jnp.broadcast_to(r, (bs, w)) for r in rows], axis=0)

    s = spread(tot) - p
    qp = qq * jnp.exp(p)
    ks = kk * jnp.exp(s)
    zero = jnp.zeros((1, w), F32)

    def run_sums(lo_b, hi_b):
        before, acc = {}, zero
        order = range(lo_b, hi_b) if fwd else range(hi_b - 1, lo_b - 1, -1)
        for b in order:
            before[b] = acc
            acc = acc + tot[b]
        after, acc = {}, zero
        for b in reversed(order):
            after[b] = acc
            acc = acc + tot[b]
        return before, after, acc

    def scaled(x, logs):
        zblk = jnp.zeros((bs, w), BF16)
        parts = []
        for b in range(nb):
            xb = x[b * bs:(b + 1) * bs]
            if logs[b] is None:
                parts.append(zblk)
            elif logs[b] is zero:
                parts.append(xb.astype(BF16))
            else:
                parts.append((xb * jnp.exp(logs[b])).astype(BF16))
        return jnp.concatenate(parts, axis=0)

    level_q, level_k = [], []
    for hf in HG_LEVELS:
        nh = hf // bs
        ql, kl = [None] * nb, [None] * nb
        for par in range(nb // (2 * nh)):
            halves = (par * 2 * nh, par * 2 * nh + nh), (par * 2 * nh + nh, (par + 1) * 2 * nh)
            key_half, query_half = halves if fwd else halves[::-1]
            _, after, _ = run_sums(*key_half)
            before, _, _ = run_sums(*query_half)
            for b in range(*key_half):
                kl[b] = after[b]
            for b in range(*query_half):
                ql[b] = before[b]
        level_q.append(scaled(qp, ql))
        level_k.append(scaled(ks, kl))
    s_mid = [t - pm for t, pm in zip(tot, p_mid)]
    diag_q = (qp * spread([jnp.exp(-pm) for pm in p_mid])).astype(BF16)
    diag_k = (ks * spread([jnp.exp(-sm) for sm in s_mid])).astype(BF16)
    before, after, total = run_sums(0, nb)
    state_q = scaled(qp, [before[b] for b in range(nb)])
    state_k = scaled(ks, [after[b] for b in range(nb)])
    return level_q, level_k, diag_q, diag_k, state_q, state_k, jnp.exp(total)


def _hgrn_kernel(dims, nblk, tb, has_init, *refs):
    if has_init:
        (q_ref, ff_ref, fb_ref, v_ref, g_ref, lb_ref, nw_ref, s0_ref, o_ref, sfin_ref,
         st, o_s, qb_s, kb_s, decb_s) = refs
    else:
        (q_ref, ff_ref, fb_ref, v_ref, g_ref, lb_ref, nw_ref, o_ref, sfin_ref, st, o_s, qb_s, kb_s, decb_s) = refs
    hh = dims.hh
    c = HG_C
    nck = tb // c
    s = pl.program_id(1)
    is_bwd = s >= nblk
    blk = jnp.where(is_bwd, 2 * nblk - 1 - s, s)
    first = (s == 0) | (s == nblk)
    last = (s == nblk - 1) | (s == 2 * nblk - 1)

    @pl.when(first)
    def _():
        for hd in range(hh):
            st[hd] = s0_ref[0, 0, hd].T if has_init else jnp.zeros((HG_V, HG_K), F32)

    ti = lax.broadcasted_iota(jnp.int32, (c, c), 0)
    tj = lax.broadcasted_iota(jnp.int32, (c, c), 1)
    same_diag = (ti // HG_DIAG) == (tj // HG_DIAG)
    diag_f = same_diag & (tj <= ti)
    diag_b = same_diag & (tj >= ti)
    mask_df, mask_db = diag_f.astype(F32), diag_b.astype(F32)
    parent = {hf: ((ti // (2 * hf)) == (tj // (2 * hf))).astype(F32) for hf in HG_LEVELS if 2 * hf < c}
    lb_f = lb_ref[0:1, :]
    lb_b = lb_ref[1:2, :]

    def heads(x):
        return [x[:, hd * HG_K:(hd + 1) * HG_K] for hd in range(hh)]

    def state_step(hd, qs_h, ke_h, v_h, dec_h):
        sh = st[hd]
        off = _dot_nt(qs_h, sh.astype(BF16))
        st[hd] = sh * dec_h + _dot_tn(v_h, ke_h)
        return off

    def fwd_chunk(idx, carry):
        r0 = pl.multiple_of(idx * c, c)
        rows = pl.ds(r0, c)
        g0 = pl.multiple_of(blk * tb + r0, c)
        qq = _silu(q_ref[rows, :])
        vh = heads(v_ref[rows, :].astype(BF16))
        lam_f, k_f = _hgrn_gates(ff_ref[rows, :], lb_f)
        lam_b, k_b = _hgrn_gates(fb_ref[rows, :], lb_b)
        lq_f, lk_f, dq_f, dk_f, sq_f, sk_f, dec_f = _hg_dir_operands(lam_f, k_f, qq, diag_f.astype(BF16), True)
        lq_b, lk_b, dq_b, dk_b, sq_b, sk_b, dec_b = _hg_dir_operands(lam_b, k_b, qq, diag_b.astype(BF16), False)
        qb_s[pl.ds(g0, c), :] = sq_b
        kb_s[pl.ds(g0, c), :] = sk_b
        decb_s[g0 // c] = dec_b
        lq_f, lk_f, lq_b, lk_b = ([heads(x) for x in xs] for xs in (lq_f, lk_f, lq_b, lk_b))
        dq_f, dk_f, dq_b, dk_b, sq_f, sk_f, dec_f = (heads(x) for x in (dq_f, dk_f, dq_b, dk_b, sq_f, sk_f, dec_f))
        outs = []
        for hd in range(hh):
            att = _dot_nt(dq_f[hd], dk_f[hd]) * mask_df + _dot_nt(dq_b[hd], dk_b[hd]) * mask_db
            for li, hf in enumerate(HG_LEVELS):
                p = _dot_nt(jnp.concatenate([lq_f[li][hd], lq_b[li][hd]], axis=1),
                            jnp.concatenate([lk_f[li][hd], lk_b[li][hd]], axis=1))
                att = att + (p * parent[hf] if hf in parent else p)
            o_h = _dot(att.astype(BF16), vh[hd])
            outs.append(o_h + state_step(hd, sq_f[hd], sk_f[hd], vh[hd], dec_f[hd]))
        o_s[pl.ds(g0, c), :] = jnp.concatenate(outs, axis=1)
        return carry

    def bwd_chunk(idx, carry):
        r0 = pl.multiple_of((nck - 1 - idx) * c, c)
        rows = pl.ds(r0, c)
        g0 = pl.multiple_of(blk * tb + r0, c)
        vh = heads(v_ref[rows, :].astype(BF16))
        qs = heads(qb_s[pl.ds(g0, c), :])
        ke = heads(kb_s[pl.ds(g0, c), :])
        dec = heads(decb_s[g0 // c])
        o_prev = heads(o_s[pl.ds(g0, c), :])
        gate = heads(_silu(g_ref[rows, :]))
        nw = heads(nw_ref[...])
        outs = []
        for hd in range(hh):
            o_h = o_prev[hd] + state_step(hd, qs[hd], ke[hd], vh[hd], dec[hd])
            o_h = o_h * lax.rsqrt(jnp.mean(o_h * o_h, axis=-1, keepdims=True) + RMS_EPS) * nw[hd]
            outs.append(o_h * gate[hd])
        o_ref[rows, :] = jnp.concatenate(outs, axis=1)
        return carry

    @pl.when(jnp.logical_not(is_bwd))
    def _():
        lax.fori_loop(0, nck, fwd_chunk, 0)

    @pl.when(is_bwd)
    def _():
        lax.fori_loop(0, nck, bwd_chunk, 0)

    @pl.when(last)
    def _():
        for hd in range(hh):
            sfin_ref[0, 0, hd] = st[hd].T


def _hgrn_call(dims, proj, lb, nw, s0, n_seq, seq_len, tok_off):
    d, hh = dims.d, dims.hh
    tb = min(512, seq_len)
    nblk = seq_len // tb
    off = tok_off // tb
    has_init = s0 is not None
    col0 = (dims.d_ssd + 2 * d + dims.d_ssd) // d

    def tok(b, s):
        return off + b * nblk + jnp.where(s < nblk, s, 2 * nblk - 1 - s)

    def otok(b, s):
        return b * nblk + jnp.where(s < nblk, nblk - 1, 2 * nblk - 1 - s)

    in_specs = [pl.BlockSpec((tb, d), functools.partial(lambda cc, b, s: (tok(b, s), cc), col0 + k)) for k in range(5)]
    in_specs += [pl.BlockSpec((8, d), lambda b, s: (0, 0)), pl.BlockSpec((1, d), lambda b, s: (0, 0))]
    args = [proj] * 5 + [lb, nw]
    if has_init:
        in_specs.append(pl.BlockSpec((1, 1, hh, HG_K, HG_V), lambda b, s: (b, s // nblk, 0, 0, 0)))
        args.append(s0)
    vmem = 2 * 6 * tb * d * 4 + 2 * seq_len * d * 4 + 4 * hh * HG_K * HG_V * 4 + (16 << 20)
    return pl.pallas_call(
        functools.partial(_hgrn_kernel, dims, nblk, tb, has_init),
        grid=(n_seq, 2 * nblk),
        in_specs=in_specs,
        out_specs=(pl.BlockSpec((tb, d), lambda b, s: (otok(b, s), 0)),
                   pl.BlockSpec((1, 1, hh, HG_K, HG_V), lambda b, s: (b, s // nblk, 0, 0, 0))),
        out_shape=(jax.ShapeDtypeStruct((n_seq * seq_len, d), F32),
                   jax.ShapeDtypeStruct((n_seq, 2, hh, HG_K, HG_V), F32)),
        scratch_shapes=[pltpu.VMEM((hh, HG_V, HG_K), F32),
                        pltpu.VMEM((seq_len, d), F32),
                        pltpu.VMEM((seq_len, d), BF16),
                        pltpu.VMEM((seq_len, d), BF16),
                        pltpu.VMEM((seq_len // HG_C, 1, d), F32)],
        compiler_params=_cparams(("arbitrary", "arbitrary"), vmem),
        name="hgrn_scan",
    )(*args)


def _merge_kernel(dims, tm, yc_ref, yl_ref, oc_ref, ol_ref, g1_ref, g2_ref, x_ref, mod_ref,
                  wso_ref, who_ref, wo_ref, out_ref):
    d = dims.d
    is_ctx = pl.program_id(0) < dims.t_ctx // tm
    y = jnp.where(is_ctx, yc_ref[...], yl_ref[...])
    o = jnp.where(is_ctx, oc_ref[...], ol_ref[...])
    y_ssd = _dot(y.astype(BF16), wso_ref[...])
    y_hg = _dot(o.astype(BF16), who_ref[...])
    merged = _sigmoid(g1_ref[...]) * y_ssd + _sigmoid(g2_ref[...]) * y_hg
    mix = _dot(merged.astype(BF16), wo_ref[...])
    out_ref[...] = x_ref[...] + mod_ref[0][:, 2 * d:3 * d] * mix


def _merge_call(dims, y_ctx, y_lat, o_ctx, o_lat, proj, x, mod_l, wso, who, wo):
    d, ds_, t = dims.d, dims.d_ssd, dims.t
    tm = min(512, dims.row_tile)
    nctx = dims.t_ctx // tm
    gcol = proj.shape[1] // d - 2

    def ctx_blk(i):
        return jnp.minimum(i, nctx - 1)

    def lat_blk(i):
        return jnp.maximum(i - nctx, 0)

    const = lambda i: (0, 0)
    return pl.pallas_call(
        functools.partial(_merge_kernel, dims, tm),
        grid=(t // tm,),
        in_specs=[pl.BlockSpec((tm, ds_), lambda i: (ctx_blk(i), 0)),
                  pl.BlockSpec((tm, ds_), lambda i: (lat_blk(i), 0)),
                  pl.BlockSpec((tm, d), lambda i: (ctx_blk(i), 0)),
                  pl.BlockSpec((tm, d), lambda i: (lat_blk(i), 0)),
                  pl.BlockSpec((tm, d), lambda i: (i, gcol)),
                  pl.BlockSpec((tm, d), lambda i: (i, gcol + 1)),
                  pl.BlockSpec((tm, d), lambda i: (i, 0)),
                  pl.BlockSpec((1, 1, 6 * d), lambda i: (_mod_row(i, tm, dims), 0, 0)),
                  pl.BlockSpec((ds_, d), const), pl.BlockSpec((d, d), const), pl.BlockSpec((d, d), const)],
        out_specs=pl.BlockSpec((tm, d), lambda i: (i, 0)),
        out_shape=jax.ShapeDtypeStruct((t, d), F32),
        compiler_params=_cparams(("arbitrary",), 2 * (2 * tm * ds_ + 6 * tm * d) * 4 + 2 * (ds_ + 2 * d) * d * 2 + (12 << 20)),
        name="merge_outproj",
    )(y_ctx, y_lat, o_ctx, o_lat, proj, proj, x, mod_l, wso, who, wo)


def _ffn_kernel(dims, x_ref, mod_ref, nw_ref, w1_ref, w3_ref, w2_ref, out_ref, h_s, acc_s):
    d = dims.d
    j = pl.program_id(1)

    @pl.when(j == 0)
    def _():
        x = x_ref[...]
        y = x * lax.rsqrt(jnp.mean(x * x, axis=-1, keepdims=True) + RMS_EPS) * nw_ref[...]
        mod = mod_ref[0]
        h_s[...] = (y * (1.0 + mod[:, 4 * d:5 * d]) + mod[:, 3 * d:4 * d]).astype(BF16)
        acc_s[...] = jnp.zeros(acc_s.shape, F32)

    hb = h_s[...]
    act = _silu(_dot(hb, w1_ref[...])) * _dot(hb, w3_ref[...])
    acc_s[...] += _dot(act.astype(BF16), w2_ref[...])

    @pl.when(j == pl.num_programs(1) - 1)
    def _():
        out_ref[...] = x_ref[...] + mod_ref[0][:, 5 * d:6 * d] * acc_s[...]


def _ffn_call(dims, x, mod_l, nw, w1, w3, w2):
    d, t, f = dims.d, dims.t, dims.ffn
    tm = min(512, dims.row_tile)
    th = f // 2 if (f // 2) % LANE == 0 and f % 2 == 0 else f
    nh = f // th

    def hid(i, j):
        return jnp.where(i % 2 == 0, j, nh - 1 - j)

    return pl.pallas_call(
        functools.partial(_ffn_kernel, dims),
        grid=(t // tm, nh),
        in_specs=[pl.BlockSpec((tm, d), lambda i, j: (i, 0)),
                  pl.BlockSpec((1, 1, 6 * d), lambda i, j: (_mod_row(i, tm, dims), 0, 0)),
                  pl.BlockSpec((1, d), lambda i, j: (0, 0)),
                  pl.BlockSpec((d, th), lambda i, j: (0, hid(i, j))),
                  pl.BlockSpec((d, th), lambda i, j: (0, hid(i, j))),
                  pl.BlockSpec((th, d), lambda i, j: (hid(i, j), 0))],
        out_specs=pl.BlockSpec((tm, d), lambda i, j: (i, 0)),
        out_shape=jax.ShapeDtypeStruct((t, d), F32),
        scratch_shapes=[pltpu.VMEM((tm, d), BF16), pltpu.VMEM((tm, d), F32)],
        compiler_params=_cparams(("arbitrary", "arbitrary"), 4 * tm * d * 4 + 2 * 3 * d * th * 2 + 4 * tm * th * 4 + (12 << 20)),
        name="ffn",
    )(x, mod_l, nw, w1, w3, w2)


def _final_norm_kernel(x_ref, w_ref, o_ref):
    x = x_ref[...]
    o_ref[...] = x * lax.rsqrt(jnp.mean(x * x, axis=-1, keepdims=True) + RMS_EPS) * w_ref[...]


def _final_norm_call(dims, x, w):
    d, t = dims.d, dims.t
    tm = dims.row_tile
    return pl.pallas_call(
        _final_norm_kernel,
        grid=(t // tm,),
        in_specs=[pl.BlockSpec((tm, d), lambda i: (i, 0)), pl.BlockSpec((1, d), lambda i: (0, 0))],
        out_specs=pl.BlockSpec((tm, d), lambda i: (i, 0)),
        out_shape=jax.ShapeDtypeStruct((t, d), F32),
        compiler_params=_cparams(("arbitrary",), 4 * tm * d * 4 + (8 << 20)),
        name="final_norm",
    )(x, w)


def _forward(dims, x_prompt, x_sample, c, state_ssm, state_hgrn, c_ctx, norm_mix_w, norm_ffn_w, ada_w, ada_b, w_in,
             ssd_conv_w, ssd_conv_b, ssd_dt_bias, ssd_a_log, ssd_d, ssd_norm_w, hgrn_lower_bounds, hgrn_norm_w,
             w_ssd_out, w_hgrn_out, w_out, ffn_w1, ffn_w3, ffn_w2, norm_final_w):
    d, ds_, h = dims.d, dims.d_ssd, dims.h_ssd
    assert dims.db + 1 <= MOD_ROWS and 2 * h <= LANE and dims.nl % HG_C == 0 and dims.dl % 512 == 0
    conv_dim = ds_ + 2 * dims.groups * SSD_N
    x = jnp.concatenate([x_prompt.reshape(dims.t_ctx, d), x_sample.reshape(dims.t_lat, d)], axis=0)
    cvec = jnp.zeros((MOD_ROWS, d), F32).at[0].set(c_ctx).at[1:1 + dims.db].set(c)
    mod = _mod_call(dims, cvec, ada_w, ada_b)
    lb_all = jnp.cumsum(jax.nn.softmax(hgrn_lower_bounds.astype(F32), axis=0), axis=0)
    lb_all = lb_all - lb_all[:1]
    dt_cols = slice(conv_dim + ds_, conv_dim + ds_ + 2 * h)

    ssm_states, hgrn_states = [], []
    for l in range(dims.depth):
        mod_l = mod[l].reshape(MOD_ROWS, 1, 6 * d)
        w_l = w_in[l]
        w_main = jnp.concatenate([w_l[:, :dt_cols.start], w_l[:, dt_cols.stop:]], axis=1).astype(BF16)
        wdt_t = jnp.zeros((LANE, d), F32).at[:2 * h].set(w_l[:, dt_cols].T).astype(BF16)
        dtb = jnp.zeros((LANE, 1), F32).at[:2 * h, 0].set(ssd_dt_bias[l].reshape(-1))
        alog = jnp.zeros((LANE, 1), F32).at[:2 * h, 0].set(ssd_a_log[l].reshape(-1))
        proj, dt_t, da_t = _inproj_call(dims, x, mod_l, norm_mix_w[l].reshape(1, d), w_main,
                                        ssd_conv_w[l], ssd_conv_b[l].reshape(1, conv_dim), wdt_t, dtb, alog)
        dskip = jnp.broadcast_to(jnp.repeat(ssd_d[l], SSD_P)[:, None], (ds_, SSD_Q))
        ssd_nw = jnp.broadcast_to(ssd_norm_w[l][:, None], (ds_, SSD_Q))
        y_ctx, s_ssm = _ssd_call(dims, proj, dt_t, da_t, dskip, ssd_nw, None, dims.nb, dims.nl, 0)
        y_lat, _ = _ssd_call(dims, proj, dt_t, da_t, dskip, ssd_nw,
                             state_ssm[:, l].reshape(dims.db, 2, ds_, SSD_N), dims.db, dims.dl, dims.t_ctx)
        lb = jnp.zeros((8, d), F32).at[:2].set(lb_all[l])
        hg_nw = hgrn_norm_w[l].reshape(1, d)
        o_ctx, s_hg = _hgrn_call(dims, proj, lb, hg_nw, None, dims.nb, dims.nl, 0)
        o_lat, _ = _hgrn_call(dims, proj, lb, hg_nw, state_hgrn[:, l], dims.db, dims.dl, dims.t_ctx)
        x = _merge_call(dims, y_ctx, y_lat, o_ctx, o_lat, proj, x, mod_l,
                        w_ssd_out[l].astype(BF16), w_hgrn_out[l].astype(BF16), w_out[l].astype(BF16))
        x = _ffn_call(dims, x, mod_l, norm_ffn_w[l].reshape(1, d),
                      ffn_w1[l].astype(BF16), ffn_w3[l].astype(BF16), ffn_w2[l].astype(BF16))
        ssm_states.append(s_ssm.reshape(dims.nb, 2, h, SSD_P, SSD_N))
        hgrn_states.append(s_hg)
    xn = _final_norm_call(dims, x, norm_final_w.reshape(1, d))
    y_prompt = xn[:dims.t_ctx].reshape(dims.nb, dims.nl, d)
    y_sample = xn[dims.t_ctx:].reshape(dims.db, dims.dl, d)
    return y_prompt, y_sample, jnp.stack(ssm_states, axis=1), jnp.stack(hgrn_states, axis=1)


def kernel(x_prompt, x_sample, c, state_ssm, state_hgrn, c_ctx, norm_mix_w, norm_ffn_w, ada_w, ada_b, w_in,
           ssd_conv_w, ssd_conv_b, ssd_dt_bias, ssd_a_log, ssd_d, ssd_norm_w, hgrn_lower_bounds, hgrn_norm_w,
           w_ssd_out, w_hgrn_out, w_out, ffn_w1, ffn_w3, ffn_w2, norm_final_w):
    dims = Dims(d=x_prompt.shape[2], nb=x_prompt.shape[0], nl=x_prompt.shape[1], db=x_sample.shape[0],
                dl=x_sample.shape[1], depth=w_in.shape[0], ffn=ffn_w1.shape[2])
    return _forward(dims, x_prompt, x_sample, c, state_ssm, state_hgrn, c_ctx, norm_mix_w, norm_ffn_w, ada_w, ada_b,
                    w_in, ssd_conv_w, ssd_conv_b, ssd_dt_bias, ssd_a_log, ssd_d, ssd_norm_w, hgrn_lower_bounds,
                    hgrn_norm_w, w_ssd_out, w_hgrn_out, w_out, ffn_w1, ffn_w3, ffn_w2, norm_final_w)
```

```python
import functools
from typing import NamedTuple

import jax
import jax.numpy as jnp
from jax import lax
from jax.experimental import pallas as pl
from jax.experimental.pallas import tpu as pltpu

F32 = jnp.float32
BF16 = jnp.bfloat16

LANE = 128
MXU_COLS = 256
VMEM_BYTES = 64 * 1024 * 1024
RMS_EPS = 1e-6
D_CONV = 5
GRID_W = 64
SSD_P = 64
SSD_N = 128
SSD_Q = 128
SSD_R = 4
HG_K = 128
HG_V = 128
HG_C = 128
HG_LEVELS = (64, 32, 16)
HG_DIAG = 16
HF_BLOCKS = 2
MOD_ROWS = 8


class Dims(NamedTuple):
    d: int
    nb: int
    nl: int
    db: int
    dl: int
    depth: int
    ffn: int

    @property
    def t_ctx(self):
        return self.nb * self.nl

    @property
    def t_lat(self):
        return self.db * self.dl

    @property
    def t(self):
        return self.t_ctx + self.t_lat

    @property
    def d_ssd(self):
        return 2 * self.d

    @property
    def h_ssd(self):
        return self.d_ssd // SSD_P

    @property
    def groups(self):
        return self.h_ssd // SSD_R

    @property
    def hh(self):
        return self.d // HG_K

    @property
    def row_tile(self):
        tm = min(1024, self.dl, self.t_ctx)
        assert self.dl % tm == 0 and self.t_ctx % tm == 0 and tm % self.nl == 0 and tm % GRID_W == 0
        return tm


def _cparams(semantics, vmem_bytes):
    return pltpu.CompilerParams(dimension_semantics=semantics, vmem_limit_bytes=min(vmem_bytes, VMEM_BYTES - (4 << 20)))


def _dot(a, b):
    return jnp.dot(a, b, preferred_element_type=F32)


def _dot_nt(a, b):
    return lax.dot_general(a, b, (((1,), (1,)), ((), ())), preferred_element_type=F32)


def _dot_tn(a, b):
    return lax.dot_general(a, b, (((0,), (0,)), ((), ())), preferred_element_type=F32)


def _split3(x):
    hi = x.astype(BF16)
    r = x - hi.astype(F32)
    mid = r.astype(BF16)
    lo = (r - mid.astype(F32)).astype(BF16)
    return hi, mid, lo


def _sum01_right(x, m01):
    hi, mid, lo = _split3(x)
    return _dot(hi, m01) + _dot(mid, m01) + _dot(lo, m01)


def _sigmoid(x):
    return 0.5 * jnp.tanh(0.5 * x) + 0.5


def _silu(x):
    return x * _sigmoid(x)


def _softplus(x):
    return jnp.maximum(x, 0.0) + jnp.log1p(jnp.exp(-jnp.abs(x)))


def _mod_row(i, tm, dims):
    nctx = dims.t_ctx // tm
    return jnp.where(i < nctx, 0, 1 + ((i - nctx) * tm) // dims.dl)


def _mod_kernel(c_ref, w_ref, b_ref, o_ref):
    s = _silu(c_ref[...])
    o_ref[0] = _dot(s.astype(BF16), w_ref[0].astype(BF16)) + b_ref[0]


def _mod_call(dims, cvec, ada_w, ada_b):
    d, n = dims.d, 6 * dims.d
    tn = n // 4
    return pl.pallas_call(
        _mod_kernel,
        grid=(dims.depth, n // tn),
        in_specs=[pl.BlockSpec((MOD_ROWS, d), lambda l, j: (0, 0)),
                  pl.BlockSpec((1, d, tn), lambda l, j: (l, 0, j)),
                  pl.BlockSpec((1, 1, tn), lambda l, j: (l, 0, j))],
        out_specs=pl.BlockSpec((1, MOD_ROWS, tn), lambda l, j: (l, 0, j)),
        out_shape=jax.ShapeDtypeStruct((dims.depth, MOD_ROWS, n), F32),
        compiler_params=_cparams(("arbitrary", "arbitrary"), 2 * d * tn * 4 + (8 << 20)),
        name="adaln_mod",
    )(cvec, ada_w, ada_b.reshape(dims.depth, 1, n))


def _inproj_kernel(dims, tm, x_ref, mod_ref, nw_ref, w_ref, cw_ref, cb_ref, wdt_ref, dtb_ref, alog_ref,
                   proj_ref, hf_ref, dt_ref, da_ref, h_s):
    d = dims.d
    i = pl.program_id(0)
    j = pl.program_id(1)
    n_conv = (dims.d_ssd + 2 * dims.groups * SSD_N) // d
    n_bf = pl.num_programs(1) - HF_BLOCKS

    @pl.when(j == 0)
    def _():
        x = x_ref[...]
        y = x * lax.rsqrt(jnp.mean(x * x, axis=-1, keepdims=True) + RMS_EPS) * nw_ref[...]
        mod = mod_ref[0]
        hb = (y * (1.0 + mod[:, d:2 * d]) + mod[:, 0:d]).astype(BF16)
        h_s[...] = hb
        dt = _softplus(_dot_nt(wdt_ref[...], hb) + dtb_ref[...])
        da = dt * -jnp.exp(alog_ref[...])
        for k in range(tm // SSD_Q):
            dt_ref[k] = dt[:, k * SSD_Q:(k + 1) * SSD_Q]
            da_ref[k] = da[:, k * SSD_Q:(k + 1) * SSD_Q]

    @pl.when(j >= n_bf)
    def _():
        hf_ref[...] = _dot(h_s[...], w_ref[...])

    @pl.when((j >= n_conv) & (j < n_bf))
    def _():
        proj_ref[...] = _dot(h_s[...], w_ref[...]).astype(BF16)

    def conv_silu(seg):
        ns, half, sub = tm // seg, D_CONV // 2, 8
        n8 = ns * sub
        r8 = lax.broadcasted_iota(jnp.int32, (n8, LANE), 0) & (sub - 1)
        res = None
        for sl in range(d // LANE):
            cols = slice(sl * LANE, (sl + 1) * LANE)
            if sl % (MXU_COLS // LANE) == 0:
                res = _dot(h_s[...], w_ref[:, sl * LANE:sl * LANE + MXU_COLS])
            u = res[:, (sl * LANE) % MXU_COLS:(sl * LANE) % MXU_COLS + LANE]
            w = [cw_ref[k:k + 1, cols] for k in range(D_CONV)]
            acc = cb_ref[:, cols] + w[half] * u
            for k in range(D_CONV):
                if k != half:
                    acc = acc + w[k] * pltpu.roll(u, (half - k) % tm, 0)
            u3 = u.reshape(ns, seg, LANE)
            a3 = acc.reshape(ns, seg, LANE)
            edge_lo = u3[:, 0:sub, :].reshape(n8, LANE)
            edge_hi = u3[:, seg - sub:seg, :].reshape(n8, LANE)
            leak_lo = (w[0] * jnp.where(r8 < 2, pltpu.roll(edge_hi, 2, 0), 0.0)
                       + w[1] * jnp.where(r8 < 1, pltpu.roll(edge_hi, 1, 0), 0.0))
            leak_hi = (w[4] * jnp.where(r8 >= sub - 2, pltpu.roll(edge_lo, n8 - 2, 0), 0.0)
                       + w[3] * jnp.where(r8 >= sub - 1, pltpu.roll(edge_lo, n8 - 1, 0), 0.0))
            a3 = jnp.concatenate([a3[:, 0:sub, :] - leak_lo.reshape(ns, sub, LANE),
                                  a3[:, sub:seg - sub, :],
                                  a3[:, seg - sub:seg, :] - leak_hi.reshape(ns, sub, LANE)], axis=1)
            proj_ref[:, cols] = _silu(a3.reshape(tm, LANE)).astype(BF16)

    is_ctx = i < dims.t_ctx // tm

    @pl.when((j < n_conv) & is_ctx)
    def _():
        conv_silu(dims.nl)

    @pl.when((j < n_conv) & jnp.logical_not(is_ctx))
    def _():
        conv_silu(GRID_W)


def _inproj_call(dims, x, mod_l, nw, w, cw, cb, wdt_t, dtb, alog):
    d, t, tm = dims.d, dims.t, dims.row_tile
    ncol = w.shape[1]
    n_conv = (dims.d_ssd + 2 * dims.groups * SSD_N) // d
    nchunk = t // SSD_Q
    n_bf = ncol // d - HF_BLOCKS
    out_shapes = (jax.ShapeDtypeStruct((t, n_bf * d), BF16),
                  jax.ShapeDtypeStruct((t, HF_BLOCKS * d), F32),
                  jax.ShapeDtypeStruct((nchunk, LANE, SSD_Q), F32),
                  jax.ShapeDtypeStruct((nchunk, LANE, SSD_Q), F32))
    kc = tm // SSD_Q
    return pl.pallas_call(
        functools.partial(_inproj_kernel, dims, tm),
        grid=(t // tm, ncol // d),
        in_specs=[pl.BlockSpec((tm, d), lambda i, j: (i, 0)),
                  pl.BlockSpec((1, 1, 6 * d), lambda i, j: (_mod_row(i, tm, dims), 0, 0)),
                  pl.BlockSpec((1, d), lambda i, j: (0, 0)),
                  pl.BlockSpec((d, d), lambda i, j: (0, j)),
                  pl.BlockSpec((D_CONV, d), lambda i, j: (0, jnp.minimum(j, n_conv - 1))),
                  pl.BlockSpec((1, d), lambda i, j: (0, jnp.minimum(j, n_conv - 1))),
                  pl.BlockSpec((LANE, d), lambda i, j: (0, 0)),
                  pl.BlockSpec((LANE, 1), lambda i, j: (0, 0)),
                  pl.BlockSpec((LANE, 1), lambda i, j: (0, 0))],
        out_specs=(pl.BlockSpec((tm, d), lambda i, j: (i, jnp.minimum(j, n_bf - 1))),
                   pl.BlockSpec((tm, d), lambda i, j: (i, jnp.maximum(j - n_bf, 0))),
                   pl.BlockSpec((kc, LANE, SSD_Q), lambda i, j: (i, 0, 0)),
                   pl.BlockSpec((kc, LANE, SSD_Q), lambda i, j: (i, 0, 0))),
        out_shape=out_shapes,
        scratch_shapes=[pltpu.VMEM((tm, d), BF16)],
        compiler_params=_cparams(("arbitrary", "arbitrary"), 6 * tm * d * 4 + 2 * d * d * 2 + tm * d * 2 + (8 << 20)),
        name="inproj",
    )(x, mod_l, nw, w, cw, cb, wdt_t, dtb, alog)


def _ssd_kernel(dims, nchunk, has_init, *refs):
    if has_init:
        (x_ref, b_ref, c_ref, z_ref, dt_ref, da_ref, dsk_ref, nw_ref, s0_ref,
         y_ref, sfin_ref, st, yf_s, xt_s, yb_s, bg_s, cg_s, cs_s, ecs_s, wend_s, dec_s, dtr_s) = refs
    else:
        (x_ref, b_ref, c_ref, z_ref, dt_ref, da_ref, dsk_ref, nw_ref,
         y_ref, sfin_ref, st, yf_s, xt_s, yb_s, bg_s, cg_s, cs_s, ecs_s, wend_s, dec_s, dtr_s) = refs
    h, g_n = dims.h_ssd, dims.groups
    q = SSD_Q
    s = pl.program_id(1)
    is_bwd = s >= nchunk
    ck = jnp.where(is_bwd, 2 * nchunk - 1 - s, s)
    first = (s == 0) | (s == nchunk)
    last = (s == nchunk - 1) | (s == 2 * nchunk - 1)

    @pl.when(first)
    def _():
        st[...] = s0_ref[0, 0] if has_init else jnp.zeros(st.shape, F32)

    row0 = pl.multiple_of(jnp.where(is_bwd, h, 0), 8)
    dtr = dt_ref[0, pl.ds(row0, h), :]
    dar = da_ref[0, pl.ds(row0, h), :]
    jj = lax.broadcasted_iota(jnp.int32, (q, q), 0)
    ii = lax.broadcasted_iota(jnp.int32, (q, q), 1)
    seen = jnp.where(is_bwd, jj - ii, ii - jj) >= 0
    cs = _sum01_right(dar, seen.astype(BF16))
    tot = _sum01_right(dar, jnp.ones((q, q), BF16))
    cs_s[...] = cs
    ecs_s[...] = jnp.exp(cs)
    wend_s[...] = jnp.exp(tot - cs) * dtr
    dec_s[...] = jnp.exp(tot)
    dtr_s[...] = dtr
    xt_s[...] = x_ref[...].astype(F32).T
    for g in range(g_n):
        bg_s[g] = b_ref[:, g * SSD_N:(g + 1) * SSD_N]
        cg_s[g] = c_ref[:, g * SSD_N:(g + 1) * SSD_N]

    gp = SSD_R * SSD_P

    def group_body(g, carry):
        bg = bg_s[g]
        cg = cg_s[g]
        cbt = _dot_nt(bg, cg)
        r0 = pl.multiple_of(g * gp, gp)
        sg = st[pl.ds(r0, gp), :]
        sc = _dot_nt(sg.astype(BF16), cg)
        xg = xt_s[pl.ds(r0, gp), :]
        xw, decs = [], []
        for r in range(SSD_R):
            hd = g * SSD_R + r
            xh = xg[r * SSD_P:(r + 1) * SSD_P]
            cs_i = cs_s[pl.ds(hd, 1), :]
            cs_j = jnp.broadcast_to(cs_i, (q, q)).T
            decay = jnp.exp(jnp.where(seen, cs_i - cs_j, -jnp.inf))
            mt = (cbt * decay).astype(BF16)
            xdt = (xh * dtr_s[pl.ds(hd, 1), :]).astype(BF16)
            yh = _dot(xdt, mt) + sc[r * SSD_P:(r + 1) * SSD_P] * ecs_s[pl.ds(hd, 1), :]
            yb_s[pl.ds(pl.multiple_of(r0 + r * SSD_P, SSD_P), SSD_P), :] = yh
            xw.append((xh * wend_s[pl.ds(hd, 1), :]).astype(BF16))
            decs.append(jnp.broadcast_to(dec_s[pl.ds(hd, 1), :], (SSD_P, SSD_N)))
        upd = _dot(jnp.concatenate(xw, axis=0), bg)
        st[pl.ds(r0, gp), :] = sg * jnp.concatenate(decs, axis=0) + upd
        return carry

    lax.fori_loop(0, g_n, group_body, 0, unroll=True)

    @pl.when(jnp.logical_not(is_bwd))
    def _():
        yf_s[ck] = yb_s[...]

    @pl.when(is_bwd)
    def _():
        yt = yb_s[...] + yf_s[ck] + dsk_ref[...] * xt_s[...]
        yt = yt * _silu(z_ref[...].astype(F32).T)
        y3 = yt.reshape(g_n, gp, q)
        yn = y3 * lax.rsqrt(jnp.mean(y3 * y3, axis=1, keepdims=True) + RMS_EPS)
        y_ref[...] = (yn.reshape(dims.d_ssd, q) * nw_ref[...]).T

    @pl.when(last)
    def _():
        sfin_ref[0, 0] = st[...]


def _ssd_call(dims, proj, dt_t, da_t, dskip, nw, s0, n_seq, seq_len, tok_off):
    d, ds_ = dims.d, dims.d_ssd
    q = SSD_Q
    nchunk = seq_len // q
    off = tok_off // q
    has_init = s0 is not None

    def ck(s):
        return jnp.where(s < nchunk, s, 2 * nchunk - 1 - s)

    def tok(b, s):
        return off + b * nchunk + ck(s)

    def ytok(b, s):
        return b * nchunk + jnp.where(s < nchunk, nchunk - 1, 2 * nchunk - 1 - s)

    in_specs = [pl.BlockSpec((q, ds_), lambda b, s: (tok(b, s), 0)),
                pl.BlockSpec((q, d), lambda b, s: (tok(b, s), ds_ // d)),
                pl.BlockSpec((q, d), lambda b, s: (tok(b, s), ds_ // d + 1)),
                pl.BlockSpec((q, ds_), lambda b, s: (off + ytok(b, s), (ds_ + 2 * d) // ds_)),
                pl.BlockSpec((1, LANE, q), lambda b, s: (tok(b, s), 0, 0)),
                pl.BlockSpec((1, LANE, q), lambda b, s: (tok(b, s), 0, 0)),
                pl.BlockSpec((ds_, q), lambda b, s: (0, 0)),
                pl.BlockSpec((ds_, q), lambda b, s: (0, 0))]
    args = [proj, proj, proj, proj, dt_t, da_t, dskip, nw]
    if has_init:
        in_specs.append(pl.BlockSpec((1, 1, ds_, SSD_N), lambda b, s: (b, s // nchunk, 0, 0)))
        args.append(s0)
    h = dims.h_ssd
    scratch = [pltpu.VMEM((ds_, SSD_N), F32),
               pltpu.VMEM((nchunk, ds_, q), F32),
               pltpu.VMEM((ds_, q), F32),
               pltpu.VMEM((ds_, q), F32),
               pltpu.VMEM((dims.groups, q, SSD_N), BF16),
               pltpu.VMEM((dims.groups, q, SSD_N), BF16),
               pltpu.VMEM((h, q), F32), pltpu.VMEM((h, q), F32), pltpu.VMEM((h, q), F32),
               pltpu.VMEM((h, q), F32), pltpu.VMEM((h, q), F32)]
    vmem = (2 * (3 * q * ds_ + 2 * q * d + 2 * LANE * q + 2 * ds_ * q + 2 * ds_ * SSD_N) * 4
            + (nchunk + 3) * ds_ * q * 4 + (12 << 20))
    return pl.pallas_call(
        functools.partial(_ssd_kernel, dims, nchunk, has_init),
        grid=(n_seq, 2 * nchunk),
        in_specs=in_specs,
        out_specs=(pl.BlockSpec((q, ds_), lambda b, s: (ytok(b, s), 0)),
                   pl.BlockSpec((1, 1, ds_, SSD_N), lambda b, s: (b, s // nchunk, 0, 0))),
        out_shape=(jax.ShapeDtypeStruct((n_seq * seq_len, ds_), F32),
                   jax.ShapeDtypeStruct((n_seq, 2, ds_, SSD_N), F32)),
        scratch_shapes=scratch,
        compiler_params=_cparams(("arbitrary", "arbitrary"), vmem),
        name="ssd_scan",
    )(*args)


def _hgrn_gates(u, lb):
    t = jnp.exp(-jnp.abs(u))
    pos = u >= 0.0
    den = 1.0 + t
    num = jnp.where(pos, 1.0 + lb * t, lb + t)
    log_num = jnp.where(pos | (jnp.broadcast_to(lb, u.shape) > 0.0), jnp.log(num), u)
    return log_num - jnp.log(den), (1.0 - lb) * jnp.where(pos, t, 1.0) / den


def _hg_dir_operands(lam, kk, qq, blockmat, fwd):
    c, w = lam.shape
    bs, nb = HG_DIAG, lam.shape[0] // HG_DIAG
    lam_hi = lam.astype(BF16)
    lam_lo = (lam - lam_hi.astype(F32)).astype(BF16)
    p = _dot(blockmat, lam_hi) + _dot(blockmat, lam_lo)
    edge, mid = (bs - 1, bs // 2 - 1) if fwd else (0, bs // 2)
    tot = [p[b * bs + edge:b * bs + edge + 1] for b in range(nb)]
    p_mid = [p[b * bs + mid:b * bs + mid + 1] for b in range(nb)]

    def spread(rows):
        return jnp.concatenate([jnp.broadcast_to(r, (bs, w)) for r in rows], axis=0)

    s = spread(tot) - p
    qp = qq * jnp.exp(p)
    ks = kk * jnp.exp(s)
    zero = jnp.zeros((1, w), F32)

    def run_sums(lo_b, hi_b):
        before, acc = {}, zero
        order = range(lo_b, hi_b) if fwd else range(hi_b - 1, lo_b - 1, -1)
        for b in order:
            before[b] = acc
            acc = acc + tot[b]
        after, acc = {}, zero
        for b in reversed(order):
            after[b] = acc
            acc = acc + tot[b]
        return before, after, acc

    def scaled(x, logs):
        zblk = jnp.zeros((bs, w), BF16)
        parts = []
        for b in range(nb):
            xb = x[b * bs:(b + 1) * bs]
            if logs[b] is None:
                parts.append(zblk)
            elif logs[b] is zero:
                parts.append(xb.astype(BF16))
            else:
                parts.append((xb * jnp.exp(logs[b])).astype(BF16))
        return jnp.concatenate(parts, axis=0)

    level_q, level_k = [], []
    for hf in HG_LEVELS:
        nh = hf // bs
        ql, kl = [None] * nb, [None] * nb
        for par in range(nb // (2 * nh)):
            halves = (par * 2 * nh, par * 2 * nh + nh), (par * 2 * nh + nh, (par + 1) * 2 * nh)
            key_half, query_half = halves if fwd else halves[::-1]
            _, after, _ = run_sums(*key_half)
            before, _, _ = run_sums(*query_half)
            for b in range(*key_half):
                kl[b] = after[b]
            for b in range(*query_half):
                ql[b] = before[b]
        level_q.append(scaled(qp, ql))
        level_k.append(scaled(ks, kl))
    s_mid = [t - pm for t, pm in zip(tot, p_mid)]
    diag_q = (qp * spread([jnp.exp(-pm) for pm in p_mid])).astype(BF16)
    diag_k = (ks * spread([jnp.exp(-sm) for sm in s_mid])).astype(BF16)
    before, after, total = run_sums(0, nb)
    state_q = scaled(qp, [before[b] for b in range(nb)])
    state_k = scaled(ks, [after[b] for b in range(nb)])
    return level_q, level_k, diag_q, diag_k, state_q, state_k, jnp.exp(total)


def _hgrn_kernel(dims, nblk, tb, has_init, *refs):
    if has_init:
        (q_ref, ff_ref, fb_ref, v_ref, g_ref, lb_ref, nw_ref, s0_ref, o_ref, sfin_ref,
         st, o_s, qb_s, kb_s, decb_s) = refs
    else:
        (q_ref, ff_ref, fb_ref, v_ref, g_ref, lb_ref, nw_ref, o_ref, sfin_ref, st, o_s, qb_s, kb_s, decb_s) = refs
    hh = dims.hh
    c = HG_C
    nck = tb // c
    s = pl.program_id(1)
    is_bwd = s >= nblk
    blk = jnp.where(is_bwd, 2 * nblk - 1 - s, s)
    first = (s == 0) | (s == nblk)
    last = (s == nblk - 1) | (s == 2 * nblk - 1)

    @pl.when(first)
    def _():
        for hd in range(hh):
            st[hd] = s0_ref[0, 0, hd].T if has_init else jnp.zeros((HG_V, HG_K), F32)

    ti = lax.broadcasted_iota(jnp.int32, (c, c), 0)
    tj = lax.broadcasted_iota(jnp.int32, (c, c), 1)
    same_diag = (ti // HG_DIAG) == (tj // HG_DIAG)
    diag_f = same_diag & (tj <= ti)
    diag_b = same_diag & (tj >= ti)
    mask_df, mask_db = diag_f.astype(F32), diag_b.astype(F32)
    parent = {hf: ((ti // (2 * hf)) == (tj // (2 * hf))).astype(F32) for hf in HG_LEVELS if 2 * hf < c}
    lb_f = lb_ref[0:1, :]
    lb_b = lb_ref[1:2, :]

    def heads(x):
        return [x[:, hd * HG_K:(hd + 1) * HG_K] for hd in range(hh)]

    def state_step(hd, qs_h, ke_h, v_h, dec_h):
        sh = st[hd]
        off = _dot_nt(qs_h, sh.astype(BF16))
        st[hd] = sh * dec_h + _dot_tn(v_h, ke_h)
        return off

    def fwd_chunk(idx, carry):
        r0 = pl.multiple_of(idx * c, c)
        rows = pl.ds(r0, c)
        g0 = pl.multiple_of(blk * tb + r0, c)
        qq = _silu(q_ref[rows, :].astype(F32))
        vh = heads(v_ref[rows, :])
        lam_f, k_f = _hgrn_gates(ff_ref[rows, :], lb_f)
        lam_b, k_b = _hgrn_gates(fb_ref[rows, :], lb_b)
        lq_f, lk_f, dq_f, dk_f, sq_f, sk_f, dec_f = _hg_dir_operands(lam_f, k_f, qq, diag_f.astype(BF16), True)
        lq_b, lk_b, dq_b, dk_b, sq_b, sk_b, dec_b = _hg_dir_operands(lam_b, k_b, qq, diag_b.astype(BF16), False)
        qb_s[pl.ds(g0, c), :] = sq_b
        kb_s[pl.ds(g0, c), :] = sk_b
        decb_s[g0 // c] = dec_b
        lq_f, lk_f, lq_b, lk_b = ([heads(x) for x in xs] for xs in (lq_f, lk_f, lq_b, lk_b))
        dq_f, dk_f, dq_b, dk_b, sq_f, sk_f, dec_f = (heads(x) for x in (dq_f, dk_f, dq_b, dk_b, sq_f, sk_f, dec_f))
        outs = []
        for hd in range(hh):
            att = _dot_nt(dq_f[hd], dk_f[hd]) * mask_df + _dot_nt(dq_b[hd], dk_b[hd]) * mask_db
            for li, hf in enumerate(HG_LEVELS):
                p = _dot_nt(jnp.concatenate([lq_f[li][hd], lq_b[li][hd]], axis=1),
                            jnp.concatenate([lk_f[li][hd], lk_b[li][hd]], axis=1))
                att = att + (p * parent[hf] if hf in parent else p)
            o_h = _dot(att.astype(BF16), vh[hd])
            outs.append(o_h + state_step(hd, sq_f[hd], sk_f[hd], vh[hd], dec_f[hd]))
        o_s[pl.ds(g0, c), :] = jnp.concatenate(outs, axis=1)
        return carry

    def bwd_chunk(idx, carry):
        r0 = pl.multiple_of((nck - 1 - idx) * c, c)
        rows = pl.ds(r0, c)
        g0 = pl.multiple_of(blk * tb + r0, c)
        vh = heads(v_ref[rows, :])
        qs = heads(qb_s[pl.ds(g0, c), :])
        ke = heads(kb_s[pl.ds(g0, c), :])
        dec = heads(decb_s[g0 // c])
        o_prev = heads(o_s[pl.ds(g0, c), :])
        gate = heads(_silu(g_ref[rows, :].astype(F32)))
        nw = heads(nw_ref[...])
        outs = []
        for hd in range(hh):
            o_h = o_prev[hd] + state_step(hd, qs[hd], ke[hd], vh[hd], dec[hd])
            o_h = o_h * lax.rsqrt(jnp.mean(o_h * o_h, axis=-1, keepdims=True) + RMS_EPS) * nw[hd]
            outs.append(o_h * gate[hd])
        o_ref[rows, :] = jnp.concatenate(outs, axis=1)
        return carry

    @pl.when(jnp.logical_not(is_bwd))
    def _():
        lax.fori_loop(0, nck, fwd_chunk, 0)

    @pl.when(is_bwd)
    def _():
        lax.fori_loop(0, nck, bwd_chunk, 0)

    @pl.when(last)
    def _():
        for hd in range(hh):
            sfin_ref[0, 0, hd] = st[hd].T


def _hgrn_call(dims, proj, hf, lb, nw, s0, n_seq, seq_len, tok_off):
    d, hh = dims.d, dims.hh
    tb = min(512, seq_len)
    nblk = seq_len // tb
    off = tok_off // tb
    has_init = s0 is not None
    col0 = (dims.d_ssd + 2 * d + dims.d_ssd) // d

    def tok(b, s):
        return off + b * nblk + jnp.where(s < nblk, s, 2 * nblk - 1 - s)

    def tok_pass1(b, s):
        return off + b * nblk + jnp.minimum(s, nblk - 1)

    def otok(b, s):
        return b * nblk + jnp.where(s < nblk, nblk - 1, 2 * nblk - 1 - s)

    def tok_pass2(b, s):
        return off + otok(b, s)

    srcs = ((proj, col0, tok_pass1), (hf, 0, tok_pass1), (hf, 1, tok_pass1), (proj, col0 + 1, tok),
            (proj, col0 + 2, tok_pass2))
    in_specs = [pl.BlockSpec((tb, d), functools.partial(lambda cc, tk, b, s: (tk(b, s), cc), col, tk))
                for _, col, tk in srcs]
    in_specs += [pl.BlockSpec((8, d), lambda b, s: (0, 0)), pl.BlockSpec((1, d), lambda b, s: (0, 0))]
    args = [a for a, _, _ in srcs] + [lb, nw]
    if has_init:
        in_specs.append(pl.BlockSpec((1, 1, hh, HG_K, HG_V), lambda b, s: (b, s // nblk, 0, 0, 0)))
        args.append(s0)
    vmem = 2 * 6 * tb * d * 4 + 2 * seq_len * d * 4 + 4 * hh * HG_K * HG_V * 4 + (16 << 20)
    return pl.pallas_call(
        functools.partial(_hgrn_kernel, dims, nblk, tb, has_init),
        grid=(n_seq, 2 * nblk),
        in_specs=in_specs,
        out_specs=(pl.BlockSpec((tb, d), lambda b, s: (otok(b, s), 0)),
                   pl.BlockSpec((1, 1, hh, HG_K, HG_V), lambda b, s: (b, s // nblk, 0, 0, 0))),
        out_shape=(jax.ShapeDtypeStruct((n_seq * seq_len, d), F32),
                   jax.ShapeDtypeStruct((n_seq, 2, hh, HG_K, HG_V), F32)),
        scratch_shapes=[pltpu.VMEM((hh, HG_V, HG_K), F32),
                        pltpu.VMEM((seq_len, d), F32),
                        pltpu.VMEM((seq_len, d), BF16),
                        pltpu.VMEM((seq_len, d), BF16),
                        pltpu.VMEM((seq_len // HG_C, 1, d), F32)],
        compiler_params=_cparams(("arbitrary", "arbitrary"), vmem),
        name="hgrn_scan",
    )(*args)


def _merge_kernel(dims, tm, yc_ref, yl_ref, oc_ref, ol_ref, g1_ref, g2_ref, x_ref, mod_ref,
                  wso_ref, who_ref, wo_ref, out_ref):
    d = dims.d
    is_ctx = pl.program_id(0) < dims.t_ctx // tm
    y = jnp.where(is_ctx, yc_ref[...], yl_ref[...])
    o = jnp.where(is_ctx, oc_ref[...], ol_ref[...])
    y_ssd = _dot(y.astype(BF16), wso_ref[...])
    y_hg = _dot(o.astype(BF16), who_ref[...])
    merged = _sigmoid(g1_ref[...].astype(F32)) * y_ssd + _sigmoid(g2_ref[...].astype(F32)) * y_hg
    mix = _dot(merged.astype(BF16), wo_ref[...])
    out_ref[...] = x_ref[...] + mod_ref[0][:, 2 * d:3 * d] * mix


def _merge_call(dims, y_ctx, y_lat, o_ctx, o_lat, proj, x, mod_l, wso, who, wo):
    d, ds_, t = dims.d, dims.d_ssd, dims.t
    tm = min(512, dims.row_tile)
    nctx = dims.t_ctx // tm
    gcol = proj.shape[1] // d - 2

    def ctx_blk(i):
        return jnp.minimum(i, nctx - 1)

    def lat_blk(i):
        return jnp.maximum(i - nctx, 0)

    const = lambda i: (0, 0)
    return pl.pallas_call(
        functools.partial(_merge_kernel, dims, tm),
        grid=(t // tm,),
        in_specs=[pl.BlockSpec((tm, ds_), lambda i: (ctx_blk(i), 0)),
                  pl.BlockSpec((tm, ds_), lambda i: (lat_blk(i), 0)),
                  pl.BlockSpec((tm, d), lambda i: (ctx_blk(i), 0)),
                  pl.BlockSpec((tm, d), lambda i: (lat_blk(i), 0)),
                  pl.BlockSpec((tm, d), lambda i: (i, gcol)),
                  pl.BlockSpec((tm, d), lambda i: (i, gcol + 1)),
                  pl.BlockSpec((tm, d), lambda i: (i, 0)),
                  pl.BlockSpec((1, 1, 6 * d), lambda i: (_mod_row(i, tm, dims), 0, 0)),
                  pl.BlockSpec((ds_, d), const), pl.BlockSpec((d, d), const), pl.BlockSpec((d, d), const)],
        out_specs=pl.BlockSpec((tm, d), lambda i: (i, 0)),
        out_shape=jax.ShapeDtypeStruct((t, d), F32),
        compiler_params=_cparams(("arbitrary",), 2 * (2 * tm * ds_ + 6 * tm * d) * 4 + 2 * (ds_ + 2 * d) * d * 2 + (12 << 20)),
        name="merge_outproj",
    )(y_ctx, y_lat, o_ctx, o_lat, proj, proj, x, mod_l, wso, who, wo)


def _ffn_kernel(dims, x_ref, mod_ref, nw_ref, w1_ref, w3_ref, w2_ref, out_ref, h_s, acc_s):
    d = dims.d
    j = pl.program_id(1)

    @pl.when(j == 0)
    def _():
        x = x_ref[...]
        y = x * lax.rsqrt(jnp.mean(x * x, axis=-1, keepdims=True) + RMS_EPS) * nw_ref[...]
        mod = mod_ref[0]
        h_s[...] = (y * (1.0 + mod[:, 4 * d:5 * d]) + mod[:, 3 * d:4 * d]).astype(BF16)
        acc_s[...] = jnp.zeros(acc_s.shape, F32)

    hb = h_s[...]
    act = _silu(_dot(hb, w1_ref[...])) * _dot(hb, w3_ref[...])
    acc_s[...] += _dot(act.astype(BF16), w2_ref[...])

    @pl.when(j == pl.num_programs(1) - 1)
    def _():
        out_ref[...] = x_ref[...] + mod_ref[0][:, 5 * d:6 * d] * acc_s[...]


def _ffn_call(dims, x, mod_l, nw, w1, w3, w2):
    d, t, f = dims.d, dims.t, dims.ffn
    tm = min(512, dims.row_tile)
    th = f // 2 if (f // 2) % LANE == 0 and f % 2 == 0 else f
    nh = f // th

    def hid(i, j):
        return jnp.where(i % 2 == 0, j, nh - 1 - j)

    return pl.pallas_call(
        functools.partial(_ffn_kernel, dims),
        grid=(t // tm, nh),
        in_specs=[pl.BlockSpec((tm, d), lambda i, j: (i, 0)),
                  pl.BlockSpec((1, 1, 6 * d), lambda i, j: (_mod_row(i, tm, dims), 0, 0)),
                  pl.BlockSpec((1, d), lambda i, j: (0, 0)),
                  pl.BlockSpec((d, th), lambda i, j: (0, hid(i, j))),
                  pl.BlockSpec((d, th), lambda i, j: (0, hid(i, j))),
                  pl.BlockSpec((th, d), lambda i, j: (hid(i, j), 0))],
        out_specs=pl.BlockSpec((tm, d), lambda i, j: (i, 0)),
        out_shape=jax.ShapeDtypeStruct((t, d), F32),
        scratch_shapes=[pltpu.VMEM((tm, d), BF16), pltpu.VMEM((tm, d), F32)],
        compiler_params=_cparams(("arbitrary", "arbitrary"), 4 * tm * d * 4 + 2 * 3 * d * th * 2 + 4 * tm * th * 4 + (12 << 20)),
        name="ffn",
    )(x, mod_l, nw, w1, w3, w2)


def _final_norm_kernel(x_ref, w_ref, o_ref):
    x = x_ref[...]
    o_ref[...] = x * lax.rsqrt(jnp.mean(x * x, axis=-1, keepdims=True) + RMS_EPS) * w_ref[...]


def _final_norm_call(dims, x, w):
    d, t = dims.d, dims.t
    tm = dims.row_tile
    return pl.pallas_call(
        _final_norm_kernel,
        grid=(t // tm,),
        in_specs=[pl.BlockSpec((tm, d), lambda i: (i, 0)), pl.BlockSpec((1, d), lambda i: (0, 0))],
        out_specs=pl.BlockSpec((tm, d), lambda i: (i, 0)),
        out_shape=jax.ShapeDtypeStruct((t, d), F32),
        compiler_params=_cparams(("arbitrary",), 4 * tm * d * 4 + (8 << 20)),
        name="final_norm",
    )(x, w)


def _forward(dims, x_prompt, x_sample, c, state_ssm, state_hgrn, c_ctx, norm_mix_w, norm_ffn_w, ada_w, ada_b, w_in,
             ssd_conv_w, ssd_conv_b, ssd_dt_bias, ssd_a_log, ssd_d, ssd_norm_w, hgrn_lower_bounds, hgrn_norm_w,
             w_ssd_out, w_hgrn_out, w_out, ffn_w1, ffn_w3, ffn_w2, norm_final_w):
    d, ds_, h = dims.d, dims.d_ssd, dims.h_ssd
    assert dims.db + 1 <= MOD_ROWS and 2 * h <= LANE and dims.nl % HG_C == 0 and dims.dl % 512 == 0
    conv_dim = ds_ + 2 * dims.groups * SSD_N
    x = jnp.concatenate([x_prompt.reshape(dims.t_ctx, d), x_sample.reshape(dims.t_lat, d)], axis=0)
    cvec = jnp.zeros((MOD_ROWS, d), F32).at[0].set(c_ctx).at[1:1 + dims.db].set(c)
    mod = _mod_call(dims, cvec, ada_w, ada_b)
    lb_all = jnp.cumsum(jax.nn.softmax(hgrn_lower_bounds.astype(F32), axis=0), axis=0)
    lb_all = lb_all - lb_all[:1]
    dt_cols = slice(conv_dim + ds_, conv_dim + ds_ + 2 * h)

    ssm_states, hgrn_states = [], []
    for l in range(dims.depth):
        mod_l = mod[l].reshape(MOD_ROWS, 1, 6 * d)
        w_l = w_in[l]
        o = dt_cols.stop
        w_main = jnp.concatenate([w_l[:, :dt_cols.start], w_l[:, o:o + d], w_l[:, o + 3 * d:o + 7 * d],
                                  w_l[:, o + d:o + 3 * d]], axis=1).astype(BF16)
        wdt_t = jnp.zeros((LANE, d), F32).at[:2 * h].set(w_l[:, dt_cols].T).astype(BF16)
        dtb = jnp.zeros((LANE, 1), F32).at[:2 * h, 0].set(ssd_dt_bias[l].reshape(-1))
        alog = jnp.zeros((LANE, 1), F32).at[:2 * h, 0].set(ssd_a_log[l].reshape(-1))
        proj, hf, dt_t, da_t = _inproj_call(dims, x, mod_l, norm_mix_w[l].reshape(1, d), w_main,
                                            ssd_conv_w[l], ssd_conv_b[l].reshape(1, conv_dim), wdt_t, dtb, alog)
        dskip = jnp.broadcast_to(jnp.repeat(ssd_d[l], SSD_P)[:, None], (ds_, SSD_Q))
        ssd_nw = jnp.broadcast_to(ssd_norm_w[l][:, None], (ds_, SSD_Q))
        y_ctx, s_ssm = _ssd_call(dims, proj, dt_t, da_t, dskip, ssd_nw, None, dims.nb, dims.nl, 0)
        y_lat, _ = _ssd_call(dims, proj, dt_t, da_t, dskip, ssd_nw,
                             state_ssm[:, l].reshape(dims.db, 2, ds_, SSD_N), dims.db, dims.dl, dims.t_ctx)
        lb = jnp.zeros((8, d), F32).at[:2].set(lb_all[l])
        hg_nw = hgrn_norm_w[l].reshape(1, d)
        o_ctx, s_hg = _hgrn_call(dims, proj, hf, lb, hg_nw, None, dims.nb, dims.nl, 0)
        o_lat, _ = _hgrn_call(dims, proj, hf, lb, hg_nw, state_hgrn[:, l], dims.db, dims.dl, dims.t_ctx)
        x = _merge_call(dims, y_ctx, y_lat, o_ctx, o_lat, proj, x, mod_l,
                        w_ssd_out[l].astype(BF16), w_hgrn_out[l].astype(BF16), w_out[l].astype(BF16))
        x = _ffn_call(dims, x, mod_l, norm_ffn_w[l].reshape(1, d),
                      ffn_w1[l].astype(BF16), ffn_w3[l].astype(BF16), ffn_w2[l].astype(BF16))
        ssm_states.append(s_ssm.reshape(dims.nb, 2, h, SSD_P, SSD_N))
        hgrn_states.append(s_hg)
    xn = _final_norm_call(dims, x, norm_final_w.reshape(1, d))
    y_prompt = xn[:dims.t_ctx].reshape(dims.nb, dims.nl, d)
    y_sample = xn[dims.t_ctx:].reshape(dims.db, dims.dl, d)
    return y_prompt, y_sample, jnp.stack(ssm_states, axis=1), jnp.stack(hgrn_states, axis=1)


def kernel(x_prompt, x_sample, c, state_ssm, state_hgrn, c_ctx, norm_mix_w, norm_ffn_w, ada_w, ada_b, w_in,
           ssd_conv_w, ssd_conv_b, ssd_dt_bias, ssd_a_log, ssd_d, ssd_norm_w, hgrn_lower_bounds, hgrn_norm_w,
           w_ssd_out, w_hgrn_out, w_out, ffn_w1, ffn_w3, ffn_w2, norm_final_w):
    dims = Dims(d=x_prompt.shape[2], nb=x_prompt.shape[0], nl=x_prompt.shape[1], db=x_sample.shape[0],
                dl=x_sample.shape[1], depth=w_in.shape[0], ffn=ffn_w1.shape[2])
    return _forward(dims, x_prompt, x_sample, c, state_ssm, state_hgrn, c_ctx, norm_mix_w, norm_ffn_w, ada_w, ada_b,
                    w_in, ssd_conv_w, ssd_conv_b, ssd_dt_bias, ssd_a_log, ssd_d, ssd_norm_w, hgrn_lower_bounds,
                    hgrn_norm_w, w_ssd_out, w_hgrn_out, w_out, ffn_w1, ffn_w3, ffn_w2, norm_final_w)
```

```python
import functools
from typing import NamedTuple

import jax
import jax.numpy as jnp
from jax import lax
from jax.experimental import pallas as pl
from jax.experimental.pallas import tpu as pltpu

F32 = jnp.float32
BF16 = jnp.bfloat16

LANE = 128
MXU_COLS = 256
VMEM_BYTES = 64 * 1024 * 1024
RMS_EPS = 1e-6
D_CONV = 5
GRID_W = 64
SSD_P = 64
SSD_N = 128
SSD_Q = 128
SSD_R = 4
HG_K = 128
HG_V = 128
HG_C = 128
HG_LEVELS = (64, 32, 16)
HG_DIAG = 16
HF_BLOCKS = 2
MOD_ROWS = 8


class Dims(NamedTuple):
    d: int
    nb: int
    nl: int
    db: int
    dl: int
    depth: int
    ffn: int

    @property
    def t_ctx(self):
        return self.nb * self.nl

    @property
    def t_lat(self):
        return self.db * self.dl

    @property
    def t(self):
        return self.t_ctx + self.t_lat

    @property
    def d_ssd(self):
        return 2 * self.d

    @property
    def h_ssd(self):
        return self.d_ssd // SSD_P

    @property
    def groups(self):
        return self.h_ssd // SSD_R

    @property
    def hh(self):
        return self.d // HG_K

    @property
    def row_tile(self):
        tm = min(1024, self.dl, self.t_ctx)
        assert self.dl % tm == 0 and self.t_ctx % tm == 0 and tm % self.nl == 0 and tm % GRID_W == 0
        return tm


def _cparams(semantics, vmem_bytes):
    return pltpu.CompilerParams(dimension_semantics=semantics, vmem_limit_bytes=min(vmem_bytes, VMEM_BYTES - (4 << 20)))


def _dot(a, b):
    return jnp.dot(a, b, preferred_element_type=F32)


def _dot_nt(a, b):
    return lax.dot_general(a, b, (((1,), (1,)), ((), ())), preferred_element_type=F32)


def _dot_tn(a, b):
    return lax.dot_general(a, b, (((0,), (0,)), ((), ())), preferred_element_type=F32)


def _split3(x):
    hi = x.astype(BF16)
    r = x - hi.astype(F32)
    mid = r.astype(BF16)
    lo = (r - mid.astype(F32)).astype(BF16)
    return hi, mid, lo


def _sum01_right(x, m01):
    hi, mid, lo = _split3(x)
    return _dot(hi, m01) + _dot(mid, m01) + _dot(lo, m01)


def _sigmoid(x):
    return 0.5 * jnp.tanh(0.5 * x) + 0.5


def _silu(x):
    return x * _sigmoid(x)


def _softplus(x):
    return jnp.maximum(x, 0.0) + jnp.log1p(jnp.exp(-jnp.abs(x)))


def _mod_row(i, tm, dims):
    nctx = dims.t_ctx // tm
    return jnp.where(i < nctx, 0, 1 + ((i - nctx) * tm) // dims.dl)


def _mod_kernel(c_ref, w_ref, b_ref, o_ref):
    s = _silu(c_ref[...])
    o_ref[0] = _dot(s.astype(BF16), w_ref[0].astype(BF16)) + b_ref[0]


def _mod_call(dims, cvec, ada_w, ada_b):
    d, n = dims.d, 6 * dims.d
    tn = n // 4
    return pl.pallas_call(
        _mod_kernel,
        grid=(dims.depth, n // tn),
        in_specs=[pl.BlockSpec((MOD_ROWS, d), lambda l, j: (0, 0)),
                  pl.BlockSpec((1, d, tn), lambda l, j: (l, 0, j)),
                  pl.BlockSpec((1, 1, tn), lambda l, j: (l, 0, j))],
        out_specs=pl.BlockSpec((1, MOD_ROWS, tn), lambda l, j: (l, 0, j)),
        out_shape=jax.ShapeDtypeStruct((dims.depth, MOD_ROWS, n), F32),
        compiler_params=_cparams(("arbitrary", "arbitrary"), 2 * d * tn * 4 + (8 << 20)),
        name="adaln_mod",
    )(cvec, ada_w, ada_b.reshape(dims.depth, 1, n))


def _inproj_kernel(dims, tm, x_ref, mod_ref, nw_ref, w_ref, cw_ref, cb_ref, wdt_ref, dtb_ref, alog_ref,
                   proj_ref, hf_ref, dt_ref, da_ref, h_s):
    d = dims.d
    i = pl.program_id(0)
    j = pl.program_id(1)
    n_conv = (dims.d_ssd + 2 * dims.groups * SSD_N) // d
    n_bf = pl.num_programs(1) - HF_BLOCKS

    @pl.when(j == 0)
    def _():
        x = x_ref[...]
        y = x * lax.rsqrt(jnp.mean(x * x, axis=-1, keepdims=True) + RMS_EPS) * nw_ref[...]
        mod = mod_ref[0]
        hb = (y * (1.0 + mod[:, d:2 * d]) + mod[:, 0:d]).astype(BF16)
        h_s[...] = hb
        dt = _softplus(_dot_nt(wdt_ref[...], hb) + dtb_ref[...])
        da = dt * -jnp.exp(alog_ref[...])
        for k in range(tm // SSD_Q):
            dt_ref[k] = dt[:, k * SSD_Q:(k + 1) * SSD_Q]
            da_ref[k] = da[:, k * SSD_Q:(k + 1) * SSD_Q]

    @pl.when(j >= n_bf)
    def _():
        hf_ref[...] = _dot(h_s[...], w_ref[...])

    @pl.when((j >= n_conv) & (j < n_bf))
    def _():
        proj_ref[...] = _dot(h_s[...], w_ref[...]).astype(BF16)

    def conv_silu(seg):
        ns, half, sub = tm // seg, D_CONV // 2, 8
        n8 = ns * sub
        r8 = lax.broadcasted_iota(jnp.int32, (n8, LANE), 0) & (sub - 1)
        res = None
        for sl in range(d // LANE):
            cols = slice(sl * LANE, (sl + 1) * LANE)
            if sl % (MXU_COLS // LANE) == 0:
                res = _dot(h_s[...], w_ref[:, sl * LANE:sl * LANE + MXU_COLS])
            u = res[:, (sl * LANE) % MXU_COLS:(sl * LANE) % MXU_COLS + LANE]
            w = [cw_ref[k:k + 1, cols] for k in range(D_CONV)]
            acc = cb_ref[:, cols] + w[half] * u
            for k in range(D_CONV):
                if k != half:
                    acc = acc + w[k] * pltpu.roll(u, (half - k) % tm, 0)
            u3 = u.reshape(ns, seg, LANE)
            a3 = acc.reshape(ns, seg, LANE)
            edge_lo = u3[:, 0:sub, :].reshape(n8, LANE)
            edge_hi = u3[:, seg - sub:seg, :].reshape(n8, LANE)
            leak_lo = (w[0] * jnp.where(r8 < 2, pltpu.roll(edge_hi, 2, 0), 0.0)
                       + w[1] * jnp.where(r8 < 1, pltpu.roll(edge_hi, 1, 0), 0.0))
            leak_hi = (w[4] * jnp.where(r8 >= sub - 2, pltpu.roll(edge_lo, n8 - 2, 0), 0.0)
                       + w[3] * jnp.where(r8 >= sub - 1, pltpu.roll(edge_lo, n8 - 1, 0), 0.0))
            a3 = jnp.concatenate([a3[:, 0:sub, :] - leak_lo.reshape(ns, sub, LANE),
                                  a3[:, sub:seg - sub, :],
                                  a3[:, seg - sub:seg, :] - leak_hi.reshape(ns, sub, LANE)], axis=1)
            proj_ref[:, cols] = _silu(a3.reshape(tm, LANE)).astype(BF16)

    is_ctx = i < dims.t_ctx // tm

    @pl.when((j < n_conv) & is_ctx)
    def _():
        conv_silu(dims.nl)

    @pl.when((j < n_conv) & jnp.logical_not(is_ctx))
    def _():
        conv_silu(GRID_W)


def _inproj_call(dims, x, mod_l, nw, w, cw, cb, wdt_t, dtb, alog):
    d, t, tm = dims.d, dims.t, dims.row_tile
    ncol = w.shape[1]
    n_conv = (dims.d_ssd + 2 * dims.groups * SSD_N) // d
    nchunk = t // SSD_Q
    n_bf = ncol // d - HF_BLOCKS
    out_shapes = (jax.ShapeDtypeStruct((t, n_bf * d), BF16),
                  jax.ShapeDtypeStruct((t, HF_BLOCKS * d), F32),
                  jax.ShapeDtypeStruct((nchunk, LANE, SSD_Q), F32),
                  jax.ShapeDtypeStruct((nchunk, LANE, SSD_Q), F32))
    kc = tm // SSD_Q
    return pl.pallas_call(
        functools.partial(_inproj_kernel, dims, tm),
        grid=(t // tm, ncol // d),
        in_specs=[pl.BlockSpec((tm, d), lambda i, j: (i, 0)),
                  pl.BlockSpec((1, 1, 6 * d), lambda i, j: (_mod_row(i, tm, dims), 0, 0)),
                  pl.BlockSpec((1, d), lambda i, j: (0, 0)),
                  pl.BlockSpec((d, d), lambda i, j: (0, j)),
                  pl.BlockSpec((D_CONV, d), lambda i, j: (0, jnp.minimum(j, n_conv - 1))),
                  pl.BlockSpec((1, d), lambda i, j: (0, jnp.minimum(j, n_conv - 1))),
                  pl.BlockSpec((LANE, d), lambda i, j: (0, 0)),
                  pl.BlockSpec((LANE, 1), lambda i, j: (0, 0)),
                  pl.BlockSpec((LANE, 1), lambda i, j: (0, 0))],
        out_specs=(pl.BlockSpec((tm, d), lambda i, j: (i, jnp.minimum(j, n_bf - 1))),
                   pl.BlockSpec((tm, d), lambda i, j: (i, jnp.maximum(j - n_bf, 0))),
                   pl.BlockSpec((kc, LANE, SSD_Q), lambda i, j: (i, 0, 0)),
                   pl.BlockSpec((kc, LANE, SSD_Q), lambda i, j: (i, 0, 0))),
        out_shape=out_shapes,
        scratch_shapes=[pltpu.VMEM((tm, d), BF16)],
        compiler_params=_cparams(("arbitrary", "arbitrary"), 6 * tm * d * 4 + 2 * d * d * 2 + tm * d * 2 + (8 << 20)),
        name="inproj",
    )(x, mod_l, nw, w, cw, cb, wdt_t, dtb, alog)


def _ssd_kernel(dims, nchunk, has_init, has_prev, *refs):
    x_ref, b_ref, c_ref, z_ref, dt_ref, da_ref, dsk_ref, nw_ref = refs[:8]
    s0_ref = refs[8] if has_init else None
    (y_ref, sfin_ref, st, yf_s, xt_s, yb_s, bg_s, cg_s, cs_s, ecs_s, wend_s, dec_s,
     dtr_s) = refs[8 + has_init + has_prev:]
    h, g_n = dims.h_ssd, dims.groups
    q = SSD_Q
    s = pl.program_id(1)
    is_bwd = s >= nchunk
    ck = jnp.where(is_bwd, 2 * nchunk - 1 - s, s)
    first = (s == 0) | (s == nchunk)
    last = (s == nchunk - 1) | (s == 2 * nchunk - 1)

    @pl.when(first)
    def _():
        st[...] = s0_ref[0, 0] if has_init else jnp.zeros(st.shape, F32)

    row0 = pl.multiple_of(jnp.where(is_bwd, h, 0), 8)
    dtr = dt_ref[0, pl.ds(row0, h), :]
    dar = da_ref[0, pl.ds(row0, h), :]
    jj = lax.broadcasted_iota(jnp.int32, (q, q), 0)
    ii = lax.broadcasted_iota(jnp.int32, (q, q), 1)
    seen = jnp.where(is_bwd, jj - ii, ii - jj) >= 0
    cs = _sum01_right(dar, seen.astype(BF16))
    tot = _sum01_right(dar, jnp.ones((q, q), BF16))
    cs_s[...] = cs
    ecs_s[...] = jnp.exp(cs)
    wend_s[...] = jnp.exp(tot - cs) * dtr
    dec_s[...] = jnp.exp(tot)
    dtr_s[...] = dtr
    xt_s[...] = x_ref[...].astype(F32).T
    for g in range(g_n):
        bg_s[g] = b_ref[:, g * SSD_N:(g + 1) * SSD_N]
        cg_s[g] = c_ref[:, g * SSD_N:(g + 1) * SSD_N]

    gp = SSD_R * SSD_P

    def group_body(g, carry):
        bg = bg_s[g]
        cg = cg_s[g]
        cbt = _dot_nt(bg, cg)
        r0 = pl.multiple_of(g * gp, gp)
        sg = st[pl.ds(r0, gp), :]
        sc = _dot_nt(sg.astype(BF16), cg)
        xg = xt_s[pl.ds(r0, gp), :]
        xw, decs = [], []
        for r in range(SSD_R):
            hd = g * SSD_R + r
            xh = xg[r * SSD_P:(r + 1) * SSD_P]
            cs_i = cs_s[pl.ds(hd, 1), :]
            cs_j = jnp.broadcast_to(cs_i, (q, q)).T
            decay = jnp.exp(jnp.where(seen, cs_i - cs_j, -jnp.inf))
            mt = (cbt * decay).astype(BF16)
            xdt = (xh * dtr_s[pl.ds(hd, 1), :]).astype(BF16)
            yh = _dot(xdt, mt) + sc[r * SSD_P:(r + 1) * SSD_P] * ecs_s[pl.ds(hd, 1), :]
            yb_s[pl.ds(pl.multiple_of(r0 + r * SSD_P, SSD_P), SSD_P), :] = yh
            xw.append((xh * wend_s[pl.ds(hd, 1), :]).astype(BF16))
            decs.append(jnp.broadcast_to(dec_s[pl.ds(hd, 1), :], (SSD_P, SSD_N)))
        upd = _dot(jnp.concatenate(xw, axis=0), bg)
        st[pl.ds(r0, gp), :] = sg * jnp.concatenate(decs, axis=0) + upd
        return carry

    lax.fori_loop(0, g_n, group_body, 0, unroll=True)

    @pl.when(jnp.logical_not(is_bwd))
    def _():
        yf_s[ck] = yb_s[...]

    @pl.when(is_bwd)
    def _():
        yt = yb_s[...] + yf_s[ck] + dsk_ref[...] * xt_s[...]
        yt = yt * _silu(z_ref[...].astype(F32).T)
        y3 = yt.reshape(g_n, gp, q)
        yn = y3 * lax.rsqrt(jnp.mean(y3 * y3, axis=1, keepdims=True) + RMS_EPS)
        y_ref[...] = (yn.reshape(dims.d_ssd, q) * nw_ref[...]).T

    @pl.when(last)
    def _():
        sfin_ref[0, 0, 0] = st[...]


def _ssd_call(dims, proj, dt_t, da_t, dskip, nw, s0, n_seq, seq_len, tok_off, layer=None, fin_prev=None):
    d, ds_ = dims.d, dims.d_ssd
    n_slab, slab = (1, 0) if layer is None else (dims.depth, layer)
    q = SSD_Q
    nchunk = seq_len // q
    off = tok_off // q
    has_init = s0 is not None

    def ck(s):
        return jnp.where(s < nchunk, s, 2 * nchunk - 1 - s)

    def tok(b, s):
        return off + b * nchunk + ck(s)

    def ytok(b, s):
        return b * nchunk + jnp.where(s < nchunk, nchunk - 1, 2 * nchunk - 1 - s)

    in_specs = [pl.BlockSpec((q, ds_), lambda b, s: (tok(b, s), 0)),
                pl.BlockSpec((q, d), lambda b, s: (tok(b, s), ds_ // d)),
                pl.BlockSpec((q, d), lambda b, s: (tok(b, s), ds_ // d + 1)),
                pl.BlockSpec((q, ds_), lambda b, s: (off + ytok(b, s), (ds_ + 2 * d) // ds_)),
                pl.BlockSpec((1, LANE, q), lambda b, s: (tok(b, s), 0, 0)),
                pl.BlockSpec((1, LANE, q), lambda b, s: (tok(b, s), 0, 0)),
                pl.BlockSpec((ds_, q), lambda b, s: (0, 0)),
                pl.BlockSpec((ds_, q), lambda b, s: (0, 0))]
    args = [proj, proj, proj, proj, dt_t, da_t, dskip, nw]
    if has_init:
        in_specs.append(pl.BlockSpec((1, 1, ds_, SSD_N), lambda b, s: (b, s // nchunk, 0, 0)))
        args.append(s0)
    aliases = {}
    if fin_prev is not None:
        in_specs.append(pl.BlockSpec(memory_space=pl.ANY))
        aliases = {len(args): 1}
        args.append(fin_prev)
    h = dims.h_ssd
    scratch = [pltpu.VMEM((ds_, SSD_N), F32),
               pltpu.VMEM((nchunk, ds_, q), F32),
               pltpu.VMEM((ds_, q), F32),
               pltpu.VMEM((ds_, q), F32),
               pltpu.VMEM((dims.groups, q, SSD_N), BF16),
               pltpu.VMEM((dims.groups, q, SSD_N), BF16),
               pltpu.VMEM((h, q), F32), pltpu.VMEM((h, q), F32), pltpu.VMEM((h, q), F32),
               pltpu.VMEM((h, q), F32), pltpu.VMEM((h, q), F32)]
    vmem = (2 * (3 * q * ds_ + 2 * q * d + 2 * LANE * q + 2 * ds_ * q + 2 * ds_ * SSD_N) * 4
            + (nchunk + 3) * ds_ * q * 4 + (12 << 20))
    return pl.pallas_call(
        functools.partial(_ssd_kernel, dims, nchunk, has_init, fin_prev is not None),
        grid=(n_seq, 2 * nchunk),
        in_specs=in_specs,
        out_specs=(pl.BlockSpec((q, ds_), lambda b, s: (ytok(b, s), 0)),
                   pl.BlockSpec((1, 1, 1, ds_, SSD_N), lambda b, s: (b, slab, s // nchunk, 0, 0))),
        out_shape=(jax.ShapeDtypeStruct((n_seq * seq_len, ds_), F32),
                   jax.ShapeDtypeStruct((n_seq, n_slab, 2, ds_, SSD_N), F32)),
        input_output_aliases=aliases,
        scratch_shapes=scratch,
        compiler_params=_cparams(("arbitrary", "arbitrary"), vmem),
        name="ssd_scan",
    )(*args)


def _hgrn_gates(u, lb):
    t = jnp.exp(-jnp.abs(u))
    pos = u >= 0.0
    den = 1.0 + t
    num = jnp.where(pos, 1.0 + lb * t, lb + t)
    log_num = jnp.where(pos | (jnp.broadcast_to(lb, u.shape) > 0.0), jnp.log(num), u)
    return log_num - jnp.log(den), (1.0 - lb) * jnp.where(pos, t, 1.0) / den


def _hg_dir_operands(lam, kk, qq, blockmat, fwd):
    c, w = lam.shape
    bs, nb = HG_DIAG, lam.shape[0] // HG_DIAG
    lam_hi = lam.astype(BF16)
    lam_lo = (lam - lam_hi.astype(F32)).astype(BF16)
    p = _dot(blockmat, lam_hi) + _dot(blockmat, lam_lo)
    edge, mid = (bs - 1, bs // 2 - 1) if fwd else (0, bs // 2)
    tot = [p[b * bs + edge:b * bs + edge + 1] for b in range(nb)]
    p_mid = [p[b * bs + mid:b * bs + mid + 1] for b in range(nb)]

    def spread(rows):
        return jnp.concatenate([jnp.broadcast_to(r, (bs, w)) for r in rows], axis=0)

    s = spread(tot) - p
    qp = qq * jnp.exp(p)
    ks = kk * jnp.exp(s)
    zero = jnp.zeros((1, w), F32)

    def run_sums(lo_b, hi_b):
        before, acc = {}, zero
        order = range(lo_b, hi_b) if fwd else range(hi_b - 1, lo_b - 1, -1)
        for b in order:
            before[b] = acc
            acc = acc + tot[b]
        after, acc = {}, zero
        for b in reversed(order):
            after[b] = acc
            acc = acc + tot[b]
        return before, after, acc

    def scaled(x, logs):
        zblk = jnp.zeros((bs, w), BF16)
        parts = []
        for b in range(nb):
            xb = x[b * bs:(b + 1) * bs]
            if logs[b] is None:
                parts.append(zblk)
            elif logs[b] is zero:
                parts.append(xb.astype(BF16))
            else:
                parts.append((xb * jnp.exp(logs[b])).astype(BF16))
        return jnp.concatenate(parts, axis=0)

    level_q, level_k = [], []
    for hf in HG_LEVELS:
        nh = hf // bs
        ql, kl = [None] * nb, [None] * nb
        for par in range(nb // (2 * nh)):
            halves = (par * 2 * nh, par * 2 * nh + nh), (par * 2 * nh + nh, (par + 1) * 2 * nh)
            key_half, query_half = halves if fwd else halves[::-1]
            _, after, _ = run_sums(*key_half)
            before, _, _ = run_sums(*query_half)
            for b in range(*key_half):
                kl[b] = after[b]
            for b in range(*query_half):
                ql[b] = before[b]
        level_q.append(scaled(qp, ql))
        level_k.append(scaled(ks, kl))
    s_mid = [t - pm for t, pm in zip(tot, p_mid)]
    diag_q = (qp * spread([jnp.exp(-pm) for pm in p_mid])).astype(BF16)
    diag_k = (ks * spread([jnp.exp(-sm) for sm in s_mid])).astype(BF16)
    before, after, total = run_sums(0, nb)
    state_q = scaled(qp, [before[b] for b in range(nb)])
    state_k = scaled(ks, [after[b] for b in range(nb)])
    return level_q, level_k, diag_q, diag_k, state_q, state_k, jnp.exp(total)


def _hgrn_kernel(dims, nblk, tb, has_init, has_prev, *refs):
    q_ref, ff_ref, fb_ref, v_ref, g_ref, lb_ref, nw_ref = refs[:7]
    s0_ref = refs[7] if has_init else None
    o_ref, sfin_ref, st, o_s, qb_s, kb_s, decb_s = refs[7 + has_init + has_prev:]
    hh = dims.hh
    c = HG_C
    nck = tb // c
    s = pl.program_id(1)
    is_bwd = s >= nblk
    blk = jnp.where(is_bwd, 2 * nblk - 1 - s, s)
    first = (s == 0) | (s == nblk)
    last = (s == nblk - 1) | (s == 2 * nblk - 1)

    @pl.when(first)
    def _():
        for hd in range(hh):
            st[hd] = s0_ref[0, 0, hd].T if has_init else jnp.zeros((HG_V, HG_K), F32)

    ti = lax.broadcasted_iota(jnp.int32, (c, c), 0)
    tj = lax.broadcasted_iota(jnp.int32, (c, c), 1)
    same_diag = (ti // HG_DIAG) == (tj // HG_DIAG)
    diag_f = same_diag & (tj <= ti)
    diag_b = same_diag & (tj >= ti)
    mask_df, mask_db = diag_f.astype(F32), diag_b.astype(F32)
    parent = {hf: ((ti // (2 * hf)) == (tj // (2 * hf))).astype(F32) for hf in HG_LEVELS if 2 * hf < c}
    lb_f = lb_ref[0:1, :]
    lb_b = lb_ref[1:2, :]

    def heads(x):
        return [x[:, hd * HG_K:(hd + 1) * HG_K] for hd in range(hh)]

    def state_step(hd, qs_h, ke_h, v_h, dec_h):
        sh = st[hd]
        off = _dot_nt(qs_h, sh.astype(BF16))
        st[hd] = sh * dec_h + _dot_tn(v_h, ke_h)
        return off

    def fwd_chunk(idx, carry):
        r0 = pl.multiple_of(idx * c, c)
        rows = pl.ds(r0, c)
        g0 = pl.multiple_of(blk * tb + r0, c)
        qq = _silu(q_ref[rows, :].astype(F32))
        vh = heads(v_ref[rows, :])
        lam_f, k_f = _hgrn_gates(ff_ref[rows, :], lb_f)
        lam_b, k_b = _hgrn_gates(fb_ref[rows, :], lb_b)
        lq_f, lk_f, dq_f, dk_f, sq_f, sk_f, dec_f = _hg_dir_operands(lam_f, k_f, qq, diag_f.astype(BF16), True)
        lq_b, lk_b, dq_b, dk_b, sq_b, sk_b, dec_b = _hg_dir_operands(lam_b, k_b, qq, diag_b.astype(BF16), False)
        qb_s[pl.ds(g0, c), :] = sq_b
        kb_s[pl.ds(g0, c), :] = sk_b
        decb_s[g0 // c] = dec_b
        lq_f, lk_f, lq_b, lk_b = ([heads(x) for x in xs] for xs in (lq_f, lk_f, lq_b, lk_b))
        dq_f, dk_f, dq_b, dk_b, sq_f, sk_f, dec_f = (heads(x) for x in (dq_f, dk_f, dq_b, dk_b, sq_f, sk_f, dec_f))
        outs = []
        for hd in range(hh):
            att = _dot_nt(dq_f[hd], dk_f[hd]) * mask_df + _dot_nt(dq_b[hd], dk_b[hd]) * mask_db
            for li, hf in enumerate(HG_LEVELS):
                p = _dot_nt(jnp.concatenate([lq_f[li][hd], lq_b[li][hd]], axis=1),
                            jnp.concatenate([lk_f[li][hd], lk_b[li][hd]], axis=1))
                att = att + (p * parent[hf] if hf in parent else p)
            o_h = _dot(att.astype(BF16), vh[hd])
            outs.append(o_h + state_step(hd, sq_f[hd], sk_f[hd], vh[hd], dec_f[hd]))
        o_s[pl.ds(g0, c), :] = jnp.concatenate(outs, axis=1)
        return carry

    def bwd_chunk(idx, carry):
        r0 = pl.multiple_of((nck - 1 - idx) * c, c)
        rows = pl.ds(r0, c)
        g0 = pl.multiple_of(blk * tb + r0, c)
        vh = heads(v_ref[rows, :])
        qs = heads(qb_s[pl.ds(g0, c), :])
        ke = heads(kb_s[pl.ds(g0, c), :])
        dec = heads(decb_s[g0 // c])
        o_prev = heads(o_s[pl.ds(g0, c), :])
        gate = heads(_silu(g_ref[rows, :].astype(F32)))
        nw = heads(nw_ref[...])
        outs = []
        for hd in range(hh):
            o_h = o_prev[hd] + state_step(hd, qs[hd], ke[hd], vh[hd], dec[hd])
            o_h = o_h * lax.rsqrt(jnp.mean(o_h * o_h, axis=-1, keepdims=True) + RMS_EPS) * nw[hd]
            outs.append(o_h * gate[hd])
        o_ref[rows, :] = jnp.concatenate(outs, axis=1)
        return carry

    @pl.when(jnp.logical_not(is_bwd))
    def _():
        lax.fori_loop(0, nck, fwd_chunk, 0)

    @pl.when(is_bwd)
    def _():
        lax.fori_loop(0, nck, bwd_chunk, 0)

    @pl.when(last)
    def _():
        for hd in range(hh):
            sfin_ref[0, 0, 0, hd] = st[hd].T


def _hgrn_call(dims, proj, hf, lb, nw, s0, n_seq, seq_len, tok_off, layer=None, fin_prev=None):
    d, hh = dims.d, dims.hh
    n_slab, slab = (1, 0) if layer is None else (dims.depth, layer)
    tb = min(512, seq_len)
    nblk = seq_len // tb
    off = tok_off // tb
    has_init = s0 is not None
    col0 = (dims.d_ssd + 2 * d + dims.d_ssd) // d

    def tok(b, s):
        return off + b * nblk + jnp.where(s < nblk, s, 2 * nblk - 1 - s)

    def tok_pass1(b, s):
        return off + b * nblk + jnp.minimum(s, nblk - 1)

    def otok(b, s):
        return b * nblk + jnp.where(s < nblk, nblk - 1, 2 * nblk - 1 - s)

    def tok_pass2(b, s):
        return off + otok(b, s)

    srcs = ((proj, col0, tok_pass1), (hf, 0, tok_pass1), (hf, 1, tok_pass1), (proj, col0 + 1, tok),
            (proj, col0 + 2, tok_pass2))
    in_specs = [pl.BlockSpec((tb, d), functools.partial(lambda cc, tk, b, s: (tk(b, s), cc), col, tk))
                for _, col, tk in srcs]
    in_specs += [pl.BlockSpec((8, d), lambda b, s: (0, 0)), pl.BlockSpec((1, d), lambda b, s: (0, 0))]
    args = [a for a, _, _ in srcs] + [lb, nw]
    if has_init:
        in_specs.append(pl.BlockSpec((1, 1, hh, HG_K, HG_V), lambda b, s: (b, s // nblk, 0, 0, 0)))
        args.append(s0)
    aliases = {}
    if fin_prev is not None:
        in_specs.append(pl.BlockSpec(memory_space=pl.ANY))
        aliases = {len(args): 1}
        args.append(fin_prev)
    vmem = 2 * 6 * tb * d * 4 + 2 * seq_len * d * 4 + 4 * hh * HG_K * HG_V * 4 + (16 << 20)
    return pl.pallas_call(
        functools.partial(_hgrn_kernel, dims, nblk, tb, has_init, fin_prev is not None),
        grid=(n_seq, 2 * nblk),
        in_specs=in_specs,
        out_specs=(pl.BlockSpec((tb, d), lambda b, s: (otok(b, s), 0)),
                   pl.BlockSpec((1, 1, 1, hh, HG_K, HG_V), lambda b, s: (b, slab, s // nblk, 0, 0, 0))),
        out_shape=(jax.ShapeDtypeStruct((n_seq * seq_len, d), F32),
                   jax.ShapeDtypeStruct((n_seq, n_slab, 2, hh, HG_K, HG_V), F32)),
        input_output_aliases=aliases,
        scratch_shapes=[pltpu.VMEM((hh, HG_V, HG_K), F32),
                        pltpu.VMEM((seq_len, d), F32),
                        pltpu.VMEM((seq_len, d), BF16),
                        pltpu.VMEM((seq_len, d), BF16),
                        pltpu.VMEM((seq_len // HG_C, 1, d), F32)],
        compiler_params=_cparams(("arbitrary", "arbitrary"), vmem),
        name="hgrn_scan",
    )(*args)


def _merge_kernel(dims, tm, yc_ref, yl_ref, oc_ref, ol_ref, g1_ref, g2_ref, x_ref, mod_ref,
                  wso_ref, who_ref, wo_ref, out_ref):
    d = dims.d
    is_ctx = pl.program_id(0) < dims.t_ctx // tm
    y = jnp.where(is_ctx, yc_ref[...], yl_ref[...])
    o = jnp.where(is_ctx, oc_ref[...], ol_ref[...])
    y_ssd = _dot(y.astype(BF16), wso_ref[...])
    y_hg = _dot(o.astype(BF16), who_ref[...])
    merged = _sigmoid(g1_ref[...].astype(F32)) * y_ssd + _sigmoid(g2_ref[...].astype(F32)) * y_hg
    mix = _dot(merged.astype(BF16), wo_ref[...])
    out_ref[...] = x_ref[...] + mod_ref[0][:, 2 * d:3 * d] * mix


def _merge_call(dims, y_ctx, y_lat, o_ctx, o_lat, proj, x, mod_l, wso, who, wo):
    d, ds_, t = dims.d, dims.d_ssd, dims.t
    tm = min(512, dims.row_tile)
    nctx = dims.t_ctx // tm
    gcol = proj.shape[1] // d - 2

    def ctx_blk(i):
        return jnp.minimum(i, nctx - 1)

    def lat_blk(i):
        return jnp.maximum(i - nctx, 0)

    const = lambda i: (0, 0)
    return pl.pallas_call(
        functools.partial(_merge_kernel, dims, tm),
        grid=(t // tm,),
        in_specs=[pl.BlockSpec((tm, ds_), lambda i: (ctx_blk(i), 0)),
                  pl.BlockSpec((tm, ds_), lambda i: (lat_blk(i), 0)),
                  pl.BlockSpec((tm, d), lambda i: (ctx_blk(i), 0)),
                  pl.BlockSpec((tm, d), lambda i: (lat_blk(i), 0)),
                  pl.BlockSpec((tm, d), lambda i: (i, gcol)),
                  pl.BlockSpec((tm, d), lambda i: (i, gcol + 1)),
                  pl.BlockSpec((tm, d), lambda i: (i, 0)),
                  pl.BlockSpec((1, 1, 6 * d), lambda i: (_mod_row(i, tm, dims), 0, 0)),
                  pl.BlockSpec((ds_, d), const), pl.BlockSpec((d, d), const), pl.BlockSpec((d, d), const)],
        out_specs=pl.BlockSpec((tm, d), lambda i: (i, 0)),
        out_shape=jax.ShapeDtypeStruct((t, d), F32),
        compiler_params=_cparams(("arbitrary",), 2 * (2 * tm * ds_ + 6 * tm * d) * 4 + 2 * (ds_ + 2 * d) * d * 2 + (12 << 20)),
        name="merge_outproj",
    )(y_ctx, y_lat, o_ctx, o_lat, proj, proj, x, mod_l, wso, who, wo)


def _ffn_kernel(dims, final_norm, x_ref, mod_ref, nw_ref, w1_ref, w3_ref, w2_ref, *rest):
    d = dims.d
    fw_ref, out_ref = rest if final_norm else (None, rest[0])

    x = x_ref[...]
    y = x * lax.rsqrt(jnp.mean(x * x, axis=-1, keepdims=True) + RMS_EPS) * nw_ref[...]
    mod = mod_ref[0]
    hb = (y * (1.0 + mod[:, 4 * d:5 * d]) + mod[:, 3 * d:4 * d]).astype(BF16)
    acc = None
    for lo in range(0, dims.ffn, MXU_COLS):
        hi = min(lo + MXU_COLS, dims.ffn)
        act = _silu(_dot(hb, w1_ref[:, lo:hi])) * _dot(hb, w3_ref[:, lo:hi])
        part = _dot(act.astype(BF16), w2_ref[lo:hi, :])
        acc = part if acc is None else acc + part
    out = x + mod[:, 5 * d:6 * d] * acc
    if final_norm:
        out = out * lax.rsqrt(jnp.mean(out * out, axis=-1, keepdims=True) + RMS_EPS) * fw_ref[...]
    out_ref[...] = out


def _ffn_call(dims, x, mod_l, nw, w1, w3, w2, final_nw=None):
    d, t, f = dims.d, dims.t, dims.ffn
    tm = min(512, dims.row_tile)
    const = lambda i: (0, 0)
    resident = pl.Buffered(1)
    in_specs = [pl.BlockSpec((tm, d), lambda i: (i, 0)),
                pl.BlockSpec((1, 1, 6 * d), lambda i: (_mod_row(i, tm, dims), 0, 0)),
                pl.BlockSpec((1, d), const),
                pl.BlockSpec((d, f), const, pipeline_mode=resident),
                pl.BlockSpec((d, f), const, pipeline_mode=resident),
                pl.BlockSpec((f, d), const, pipeline_mode=resident)]
    args = [x, mod_l, nw, w1, w3, w2]
    if final_nw is not None:
        in_specs.append(pl.BlockSpec((1, d), const))
        args.append(final_nw)
    return pl.pallas_call(
        functools.partial(_ffn_kernel, dims, final_nw is not None),
        grid=(t // tm,),
        in_specs=in_specs,
        out_specs=pl.BlockSpec((tm, d), lambda i: (i, 0)),
        out_shape=jax.ShapeDtypeStruct((t, d), F32),
        compiler_params=_cparams(("arbitrary",), 4 * tm * d * 4 + 3 * d * f * 2 + 6 * tm * d * 4 + (12 << 20)),
        name="ffn",
    )(*args)


def _forward(dims, x_prompt, x_sample, c, state_ssm, state_hgrn, c_ctx, norm_mix_w, norm_ffn_w, ada_w, ada_b, w_in,
             ssd_conv_w, ssd_conv_b, ssd_dt_bias, ssd_a_log, ssd_d, ssd_norm_w, hgrn_lower_bounds, hgrn_norm_w,
             w_ssd_out, w_hgrn_out, w_out, ffn_w1, ffn_w3, ffn_w2, norm_final_w):
    d, ds_, h = dims.d, dims.d_ssd, dims.h_ssd
    assert dims.db + 1 <= MOD_ROWS and 2 * h <= LANE and dims.nl % HG_C == 0 and dims.dl % 512 == 0
    conv_dim = ds_ + 2 * dims.groups * SSD_N
    x = jnp.concatenate([x_prompt.reshape(dims.t_ctx, d), x_sample.reshape(dims.t_lat, d)], axis=0)
    cvec = jnp.zeros((MOD_ROWS, d), F32).at[0].set(c_ctx).at[1:1 + dims.db].set(c)
    mod = _mod_call(dims, cvec, ada_w, ada_b)
    lb_all = jnp.cumsum(jax.nn.softmax(hgrn_lower_bounds.astype(F32), axis=0), axis=0)
    lb_all = lb_all - lb_all[:1]
    dt_cols = slice(conv_dim + ds_, conv_dim + ds_ + 2 * h)

    s_ssm = s_hg = None
    for l in range(dims.depth):
        mod_l = mod[l].reshape(MOD_ROWS, 1, 6 * d)
        w_l = w_in[l]
        o = dt_cols.stop
        w_main = jnp.concatenate([w_l[:, :dt_cols.start], w_l[:, o:o + d], w_l[:, o + 3 * d:o + 7 * d],
                                  w_l[:, o + d:o + 3 * d]], axis=1).astype(BF16)
        wdt_t = jnp.zeros((LANE, d), F32).at[:2 * h].set(w_l[:, dt_cols].T).astype(BF16)
        dtb = jnp.zeros((LANE, 1), F32).at[:2 * h, 0].set(ssd_dt_bias[l].reshape(-1))
        alog = jnp.zeros((LANE, 1), F32).at[:2 * h, 0].set(ssd_a_log[l].reshape(-1))
        proj, hf, dt_t, da_t = _inproj_call(dims, x, mod_l, norm_mix_w[l].reshape(1, d), w_main,
                                            ssd_conv_w[l], ssd_conv_b[l].reshape(1, conv_dim), wdt_t, dtb, alog)
        dskip = jnp.broadcast_to(jnp.repeat(ssd_d[l], SSD_P)[:, None], (ds_, SSD_Q))
        ssd_nw = jnp.broadcast_to(ssd_norm_w[l][:, None], (ds_, SSD_Q))
        y_ctx, s_ssm = _ssd_call(dims, proj, dt_t, da_t, dskip, ssd_nw, None, dims.nb, dims.nl, 0, l, s_ssm)
        y_lat, _ = _ssd_call(dims, proj, dt_t, da_t, dskip, ssd_nw,
                             state_ssm[:, l].reshape(dims.db, 2, ds_, SSD_N), dims.db, dims.dl, dims.t_ctx)
        lb = jnp.zeros((8, d), F32).at[:2].set(lb_all[l])
        hg_nw = hgrn_norm_w[l].reshape(1, d)
        o_ctx, s_hg = _hgrn_call(dims, proj, hf, lb, hg_nw, None, dims.nb, dims.nl, 0, l, s_hg)
        o_lat, _ = _hgrn_call(dims, proj, hf, lb, hg_nw, state_hgrn[:, l], dims.db, dims.dl, dims.t_ctx)
        x = _merge_call(dims, y_ctx, y_lat, o_ctx, o_lat, proj, x, mod_l,
                        w_ssd_out[l].astype(BF16), w_hgrn_out[l].astype(BF16), w_out[l].astype(BF16))
        x = _ffn_call(dims, x, mod_l, norm_ffn_w[l].reshape(1, d),
                      ffn_w1[l].astype(BF16), ffn_w3[l].astype(BF16), ffn_w2[l].astype(BF16),
                      norm_final_w.reshape(1, d) if l == dims.depth - 1 else None)
    y_prompt = x[:dims.t_ctx].reshape(dims.nb, dims.nl, d)
    y_sample = x[dims.t_ctx:].reshape(dims.db, dims.dl, d)
    return (y_prompt, y_sample, s_ssm.reshape(dims.nb, dims.depth, 2, h, SSD_P, SSD_N), s_hg)


def kernel(x_prompt, x_sample, c, state_ssm, state_hgrn, c_ctx, norm_mix_w, norm_ffn_w, ada_w, ada_b, w_in,
           ssd_conv_w, ssd_conv_b, ssd_dt_bias, ssd_a_log, ssd_d, ssd_norm_w, hgrn_lower_bounds, hgrn_norm_w,
           w_ssd_out, w_hgrn_out, w_out, ffn_w1, ffn_w3, ffn_w2, norm_final_w):
    dims = Dims(d=x_prompt.shape[2], nb=x_prompt.shape[0], nl=x_prompt.shape[1], db=x_sample.shape[0],
                dl=x_sample.shape[1], depth=w_in.shape[0], ffn=ffn_w1.shape[2])
    return _forward(dims, x_prompt, x_sample, c, state_ssm, state_hgrn, c_ctx, norm_mix_w, norm_ffn_w, ada_w, ada_b,
                    w_in, ssd_conv_w, ssd_conv_b, ssd_dt_bias, ssd_a_log, ssd_d, ssd_norm_w, hgrn_lower_bounds,
                    hgrn_norm_w, w_ssd_out, w_hgrn_out, w_out, ffn_w1, ffn_w3, ffn_w2, norm_final_w)
```

```python
import functools
from typing import NamedTuple

import jax
import jax.numpy as jnp
from jax import lax
from jax.experimental import pallas as pl
from jax.experimental.pallas import tpu as pltpu

F32 = jnp.float32
BF16 = jnp.bfloat16

LANE = 128
MXU_COLS = 256
VMEM_BYTES = 64 * 1024 * 1024
RMS_EPS = 1e-6
D_CONV = 5
GRID_W = 64
SSD_P = 64
SSD_N = 128
SSD_Q = 128
SSD_R = 4
HG_K = 128
HG_V = 128
HG_C = 128
HG_LEVELS = (64, 32, 16)
HG_DIAG = 16
HF_BLOCKS = 2
MOD_ROWS = 8


class Dims(NamedTuple):
    d: int
    nb: int
    nl: int
    db: int
    dl: int
    depth: int
    ffn: int

    @property
    def t_ctx(self):
        return self.nb * self.nl

    @property
    def t_lat(self):
        return self.db * self.dl

    @property
    def t(self):
        return self.t_ctx + self.t_lat

    @property
    def d_ssd(self):
        return 2 * self.d

    @property
    def h_ssd(self):
        return self.d_ssd // SSD_P

    @property
    def groups(self):
        return self.h_ssd // SSD_R

    @property
    def hh(self):
        return self.d // HG_K

    @property
    def row_tile(self):
        tm = min(1024, self.dl, self.t_ctx)
        assert self.dl % tm == 0 and self.t_ctx % tm == 0 and tm % self.nl == 0 and tm % GRID_W == 0
        return tm


def _cparams(semantics, vmem_bytes):
    return pltpu.CompilerParams(dimension_semantics=semantics, vmem_limit_bytes=min(vmem_bytes, VMEM_BYTES - (4 << 20)))


def _dot(a, b):
    return jnp.dot(a, b, preferred_element_type=F32)


def _dot_nt(a, b):
    return lax.dot_general(a, b, (((1,), (1,)), ((), ())), preferred_element_type=F32)


def _dot_tn(a, b):
    return lax.dot_general(a, b, (((0,), (0,)), ((), ())), preferred_element_type=F32)


def _split3(x):
    hi = x.astype(BF16)
    r = x - hi.astype(F32)
    mid = r.astype(BF16)
    lo = (r - mid.astype(F32)).astype(BF16)
    return hi, mid, lo


def _sum01_right(x, m01):
    hi, mid, lo = _split3(x)
    return _dot(hi, m01) + _dot(mid, m01) + _dot(lo, m01)


def _sigmoid(x):
    return 0.5 * jnp.tanh(0.5 * x) + 0.5


def _silu(x):
    return x * _sigmoid(x)


def _softplus(x):
    return jnp.maximum(x, 0.0) + jnp.log1p(jnp.exp(-jnp.abs(x)))


def _mod_row(i, tm, dims):
    nctx = dims.t_ctx // tm
    return jnp.where(i < nctx, 0, 1 + ((i - nctx) * tm) // dims.dl)


def _mod_kernel(c_ref, w_ref, b_ref, o_ref):
    s = _silu(c_ref[...])
    o_ref[0] = _dot(s.astype(BF16), w_ref[0].astype(BF16)) + b_ref[0]


def _mod_call(dims, cvec, ada_w, ada_b):
    d, n = dims.d, 6 * dims.d
    tn = n // 4
    return pl.pallas_call(
        _mod_kernel,
        grid=(dims.depth, n // tn),
        in_specs=[pl.BlockSpec((MOD_ROWS, d), lambda l, j: (0, 0)),
                  pl.BlockSpec((1, d, tn), lambda l, j: (l, 0, j)),
                  pl.BlockSpec((1, 1, tn), lambda l, j: (l, 0, j))],
        out_specs=pl.BlockSpec((1, MOD_ROWS, tn), lambda l, j: (l, 0, j)),
        out_shape=jax.ShapeDtypeStruct((dims.depth, MOD_ROWS, n), F32),
        compiler_params=_cparams(("arbitrary", "arbitrary"), 2 * d * tn * 4 + (8 << 20)),
        name="adaln_mod",
    )(cvec, ada_w, ada_b.reshape(dims.depth, 1, n))


def _inproj_kernel(dims, tm, x_ref, mod_ref, nw_ref, wa_ref, wb_ref, cw_ref, cb_ref, wdt_ref, dtb_ref, alog_ref, lb_ref,
                   proj_ref, lam_ref, kk_ref, dt_ref, da_ref, h_s):
    d = dims.d
    i = pl.program_id(0)
    j = pl.program_id(1)
    n_conv = (dims.d_ssd + 2 * dims.groups * SSD_N) // d
    n_a = n_conv + dims.d_ssd // d
    n_bf = pl.num_programs(1) - HF_BLOCKS

    @pl.when(j == 0)
    def _():
        x = x_ref[...]
        y = x * lax.rsqrt(jnp.mean(x * x, axis=-1, keepdims=True) + RMS_EPS) * nw_ref[...]
        mod = mod_ref[0]
        hb = (y * (1.0 + mod[:, d:2 * d]) + mod[:, 0:d]).astype(BF16)
        h_s[...] = hb
        dt = _softplus(_dot_nt(wdt_ref[...], hb) + dtb_ref[...])
        da = dt * -jnp.exp(alog_ref[...])
        for k in range(tm // SSD_Q):
            dt_ref[k] = dt[:, k * SSD_Q:(k + 1) * SSD_Q]
            da_ref[k] = da[:, k * SSD_Q:(k + 1) * SSD_Q]

    @pl.when(j >= n_bf)
    def _():
        lam, kk = _hgrn_gates(_dot(h_s[...], wb_ref[...]), lb_ref[pl.ds(j - n_bf, 1), :])
        lam_ref[...] = lam
        kk_ref[...] = kk

    @pl.when((j >= n_conv) & (j < n_a))
    def _():
        proj_ref[...] = _dot(h_s[...], wa_ref[...]).astype(BF16)

    @pl.when((j >= n_a) & (j < n_bf))
    def _():
        proj_ref[...] = _dot(h_s[...], wb_ref[...]).astype(BF16)

    def conv_silu(seg):
        ns, half, sub = tm // seg, D_CONV // 2, 8
        n8 = ns * sub
        r8 = lax.broadcasted_iota(jnp.int32, (n8, LANE), 0) & (sub - 1)
        res = None
        for sl in range(d // LANE):
            cols = slice(sl * LANE, (sl + 1) * LANE)
            if sl % (MXU_COLS // LANE) == 0:
                res = _dot(h_s[...], wa_ref[:, sl * LANE:sl * LANE + MXU_COLS])
            u = res[:, (sl * LANE) % MXU_COLS:(sl * LANE) % MXU_COLS + LANE]
            w = [cw_ref[k:k + 1, cols] for k in range(D_CONV)]
            acc = cb_ref[:, cols] + w[half] * u
            for k in range(D_CONV):
                if k != half:
                    acc = acc + w[k] * pltpu.roll(u, (half - k) % tm, 0)
            u3 = u.reshape(ns, seg, LANE)
            a3 = acc.reshape(ns, seg, LANE)
            edge_lo = u3[:, 0:sub, :].reshape(n8, LANE)
            edge_hi = u3[:, seg - sub:seg, :].reshape(n8, LANE)
            leak_lo = (w[0] * jnp.where(r8 < 2, pltpu.roll(edge_hi, 2, 0), 0.0)
                       + w[1] * jnp.where(r8 < 1, pltpu.roll(edge_hi, 1, 0), 0.0))
            leak_hi = (w[4] * jnp.where(r8 >= sub - 2, pltpu.roll(edge_lo, n8 - 2, 0), 0.0)
                       + w[3] * jnp.where(r8 >= sub - 1, pltpu.roll(edge_lo, n8 - 1, 0), 0.0))
            a3 = jnp.concatenate([a3[:, 0:sub, :] - leak_lo.reshape(ns, sub, LANE),
                                  a3[:, sub:seg - sub, :],
                                  a3[:, seg - sub:seg, :] - leak_hi.reshape(ns, sub, LANE)], axis=1)
            proj_ref[:, cols] = _silu(a3.reshape(tm, LANE)).astype(BF16)

    is_ctx = i < dims.t_ctx // tm

    @pl.when((j < n_conv) & is_ctx)
    def _():
        conv_silu(dims.nl)

    @pl.when((j < n_conv) & jnp.logical_not(is_ctx))
    def _():
        conv_silu(GRID_W)


def _inproj_call(dims, x, mod_l, nw, wa, wb, cw, cb, wdt_t, dtb, alog, lb):
    d, t, tm = dims.d, dims.t, dims.row_tile
    n_conv = (dims.d_ssd + 2 * dims.groups * SSD_N) // d
    n_a = wa.shape[1] // d
    ncol = n_a + wb.shape[1] // d
    nchunk = t // SSD_Q
    n_bf = ncol - HF_BLOCKS

    def wb_block(j):
        return jnp.where(j <= n_a, 0, jnp.where(j < n_bf, j - n_a + HF_BLOCKS, j - n_bf + 1))

    out_shapes = (jax.ShapeDtypeStruct((t, n_bf * d), BF16),
                  jax.ShapeDtypeStruct((t, HF_BLOCKS * d), F32),
                  jax.ShapeDtypeStruct((t, HF_BLOCKS * d), F32),
                  jax.ShapeDtypeStruct((nchunk, LANE, SSD_Q), F32),
                  jax.ShapeDtypeStruct((nchunk, LANE, SSD_Q), F32))
    kc = tm // SSD_Q
    gate_spec = pl.BlockSpec((tm, d), lambda i, j: (i, jnp.maximum(j - n_bf, 0)))
    return pl.pallas_call(
        functools.partial(_inproj_kernel, dims, tm),
        grid=(t // tm, ncol),
        in_specs=[pl.BlockSpec((tm, d), lambda i, j: (i, 0)),
                  pl.BlockSpec((1, 1, 6 * d), lambda i, j: (_mod_row(i, tm, dims), 0, 0)),
                  pl.BlockSpec((1, d), lambda i, j: (0, 0)),
                  pl.BlockSpec((d, d), lambda i, j: (0, jnp.minimum(j, n_a - 1))),
                  pl.BlockSpec((d, d), lambda i, j: (0, wb_block(j))),
                  pl.BlockSpec((D_CONV, d), lambda i, j: (0, jnp.minimum(j, n_conv - 1))),
                  pl.BlockSpec((1, d), lambda i, j: (0, jnp.minimum(j, n_conv - 1))),
                  pl.BlockSpec((LANE, d), lambda i, j: (0, 0)),
                  pl.BlockSpec((LANE, 1), lambda i, j: (0, 0)),
                  pl.BlockSpec((LANE, 1), lambda i, j: (0, 0)),
                  pl.BlockSpec((8, d), lambda i, j: (0, 0))],
        out_specs=(pl.BlockSpec((tm, d), lambda i, j: (i, jnp.minimum(j, n_bf - 1))),
                   gate_spec, gate_spec,
                   pl.BlockSpec((kc, LANE, SSD_Q), lambda i, j: (i, 0, 0)),
                   pl.BlockSpec((kc, LANE, SSD_Q), lambda i, j: (i, 0, 0))),
        out_shape=out_shapes,
        scratch_shapes=[pltpu.VMEM((tm, d), BF16)],
        compiler_params=_cparams(("arbitrary", "arbitrary"), 8 * tm * d * 4 + 4 * d * d * 2 + tm * d * 2 + (8 << 20)),
        name="inproj",
    )(x, mod_l, nw, wa, wb, cw, cb, wdt_t, dtb, alog, lb)


def _ssd_kernel(dims, nchunk, has_init, has_prev, *refs):
    x_ref, b_ref, c_ref, z_ref, dt_ref, da_ref, dsk_ref, nw_ref = refs[:8]
    s0_ref = refs[8] if has_init else None
    (y_ref, sfin_ref, st, yf_s, xt_s, yb_s, bg_s, cg_s, cs_s, ecs_s, wend_s, dec_s,
     dtr_s) = refs[8 + has_init + has_prev:]
    h, g_n = dims.h_ssd, dims.groups
    q = SSD_Q
    s = pl.program_id(1)
    is_bwd = s >= nchunk
    ck = jnp.where(is_bwd, 2 * nchunk - 1 - s, s)
    first = (s == 0) | (s == nchunk)
    last = (s == nchunk - 1) | (s == 2 * nchunk - 1)

    @pl.when(first)
    def _():
        st[...] = s0_ref[0, 0] if has_init else jnp.zeros(st.shape, F32)

    row0 = pl.multiple_of(jnp.where(is_bwd, h, 0), 8)
    dtr = dt_ref[0, pl.ds(row0, h), :]
    dar = da_ref[0, pl.ds(row0, h), :]
    jj = lax.broadcasted_iota(jnp.int32, (q, q), 0)
    ii = lax.broadcasted_iota(jnp.int32, (q, q), 1)
    seen = jnp.where(is_bwd, jj - ii, ii - jj) >= 0
    cs = _sum01_right(dar, seen.astype(BF16))
    tot = _sum01_right(dar, jnp.ones((q, q), BF16))
    cs_s[...] = cs
    ecs_s[...] = jnp.exp(cs)
    wend_s[...] = jnp.exp(tot - cs) * dtr
    dec_s[...] = jnp.exp(tot)
    dtr_s[...] = dtr
    xt_s[...] = x_ref[...].astype(F32).T
    for g in range(g_n):
        bg_s[g] = b_ref[:, g * SSD_N:(g + 1) * SSD_N]
        cg_s[g] = c_ref[:, g * SSD_N:(g + 1) * SSD_N]

    gp = SSD_R * SSD_P

    def group_body(g, carry):
        bg = bg_s[g]
        cg = cg_s[g]
        cbt = _dot_nt(bg, cg)
        r0 = pl.multiple_of(g * gp, gp)
        sg = st[pl.ds(r0, gp), :]
        sc = _dot_nt(sg.astype(BF16), cg)
        xg = xt_s[pl.ds(r0, gp), :]
        xw, decs = [], []
        for r in range(SSD_R):
            hd = g * SSD_R + r
            xh = xg[r * SSD_P:(r + 1) * SSD_P]
            cs_i = cs_s[pl.ds(hd, 1), :]
            cs_j = jnp.broadcast_to(cs_i, (q, q)).T
            decay = jnp.exp(jnp.where(seen, cs_i - cs_j, -jnp.inf))
            mt = (cbt * decay).astype(BF16)
            xdt = (xh * dtr_s[pl.ds(hd, 1), :]).astype(BF16)
            yh = _dot(xdt, mt) + sc[r * SSD_P:(r + 1) * SSD_P] * ecs_s[pl.ds(hd, 1), :]
            yb_s[pl.ds(pl.multiple_of(r0 + r * SSD_P, SSD_P), SSD_P), :] = yh
            xw.append((xh * wend_s[pl.ds(hd, 1), :]).astype(BF16))
            decs.append(jnp.broadcast_to(dec_s[pl.ds(hd, 1), :], (SSD_P, SSD_N)))
        upd = _dot(jnp.concatenate(xw, axis=0), bg)
        st[pl.ds(r0, gp), :] = sg * jnp.concatenate(decs, axis=0) + upd
        return carry

    lax.fori_loop(0, g_n, group_body, 0, unroll=True)

    @pl.when(jnp.logical_not(is_bwd))
    def _():
        yf_s[ck] = yb_s[...]

    @pl.when(is_bwd)
    def _():
        yt = yb_s[...] + yf_s[ck] + dsk_ref[...] * xt_s[...]
        yt = yt * _silu(z_ref[...].astype(F32).T)
        y3 = yt.reshape(g_n, gp, q)
        yn = y3 * lax.rsqrt(jnp.mean(y3 * y3, axis=1, keepdims=True) + RMS_EPS)
        y_ref[...] = (yn.reshape(dims.d_ssd, q) * nw_ref[...]).T

    @pl.when(last)
    def _():
        sfin_ref[0, 0, 0] = st[...]


def _ssd_call(dims, proj, dt_t, da_t, dskip, nw, s0, n_seq, seq_len, tok_off, layer=None, fin_prev=None):
    d, ds_ = dims.d, dims.d_ssd
    n_slab, slab = (1, 0) if layer is None else (dims.depth, layer)
    q = SSD_Q
    nchunk = seq_len // q
    off = tok_off // q
    has_init = s0 is not None

    def ck(s):
        return jnp.where(s < nchunk, s, 2 * nchunk - 1 - s)

    def tok(b, s):
        return off + b * nchunk + ck(s)

    def ytok(b, s):
        return b * nchunk + jnp.where(s < nchunk, nchunk - 1, 2 * nchunk - 1 - s)

    in_specs = [pl.BlockSpec((q, ds_), lambda b, s: (tok(b, s), 0)),
                pl.BlockSpec((q, d), lambda b, s: (tok(b, s), ds_ // d)),
                pl.BlockSpec((q, d), lambda b, s: (tok(b, s), ds_ // d + 1)),
                pl.BlockSpec((q, ds_), lambda b, s: (off + ytok(b, s), (ds_ + 2 * d) // ds_)),
                pl.BlockSpec((1, LANE, q), lambda b, s: (tok(b, s), 0, 0)),
                pl.BlockSpec((1, LANE, q), lambda b, s: (tok(b, s), 0, 0)),
                pl.BlockSpec((ds_, q), lambda b, s: (0, 0)),
                pl.BlockSpec((ds_, q), lambda b, s: (0, 0))]
    args = [proj, proj, proj, proj, dt_t, da_t, dskip, nw]
    if has_init:
        in_specs.append(pl.BlockSpec((1, 1, ds_, SSD_N), lambda b, s: (b, s // nchunk, 0, 0)))
        args.append(s0)
    aliases = {}
    if fin_prev is not None:
        in_specs.append(pl.BlockSpec(memory_space=pl.ANY))
        aliases = {len(args): 1}
        args.append(fin_prev)
    h = dims.h_ssd
    scratch = [pltpu.VMEM((ds_, SSD_N), F32),
               pltpu.VMEM((nchunk, ds_, q), F32),
               pltpu.VMEM((ds_, q), F32),
               pltpu.VMEM((ds_, q), F32),
               pltpu.VMEM((dims.groups, q, SSD_N), BF16),
               pltpu.VMEM((dims.groups, q, SSD_N), BF16),
               pltpu.VMEM((h, q), F32), pltpu.VMEM((h, q), F32), pltpu.VMEM((h, q), F32),
               pltpu.VMEM((h, q), F32), pltpu.VMEM((h, q), F32)]
    vmem = (2 * (3 * q * ds_ + 2 * q * d + 2 * LANE * q + 2 * ds_ * q + 2 * ds_ * SSD_N) * 4
            + (nchunk + 3) * ds_ * q * 4 + (12 << 20))
    return pl.pallas_call(
        functools.partial(_ssd_kernel, dims, nchunk, has_init, fin_prev is not None),
        grid=(n_seq, 2 * nchunk),
        in_specs=in_specs,
        out_specs=(pl.BlockSpec((q, ds_), lambda b, s: (ytok(b, s), 0)),
                   pl.BlockSpec((1, 1, 1, ds_, SSD_N), lambda b, s: (b, slab, s // nchunk, 0, 0))),
        out_shape=(jax.ShapeDtypeStruct((n_seq * seq_len, ds_), F32),
                   jax.ShapeDtypeStruct((n_seq, n_slab, 2, ds_, SSD_N), F32)),
        input_output_aliases=aliases,
        scratch_shapes=scratch,
        compiler_params=_cparams(("arbitrary", "arbitrary"), vmem),
        name="ssd_scan",
    )(*args)


def _hgrn_gates(u, lb):
    t = jnp.exp(-jnp.abs(u))
    pos = u >= 0.0
    den = 1.0 + t
    num = jnp.where(pos, 1.0 + lb * t, lb + t)
    log_num = jnp.where(pos | (jnp.broadcast_to(lb, u.shape) > 0.0), jnp.log(num), u)
    return log_num - jnp.log(den), (1.0 - lb) * jnp.where(pos, t, 1.0) / den


def _hg_dir_operands(lam, kk, qq, blockmat, fwd):
    c, w = lam.shape
    bs, nb = HG_DIAG, lam.shape[0] // HG_DIAG
    lam_hi = lam.astype(BF16)
    lam_lo = (lam - lam_hi.astype(F32)).astype(BF16)
    p = _dot(blockmat, lam_hi) + _dot(blockmat, lam_lo)
    edge, mid = (bs - 1, bs // 2 - 1) if fwd else (0, bs // 2)
    tot = [p[b * bs + edge:b * bs + edge + 1] for b in range(nb)]
    p_mid = [p[b * bs + mid:b * bs + mid + 1] for b in range(nb)]

    def spread(rows):
        return jnp.concatenate([jnp.broadcast_to(r, (bs, w)) for r in rows], axis=0)

    s = spread(tot) - p
    qp = qq * jnp.exp(p)
    ks = kk * jnp.exp(s)
    zero = jnp.zeros((1, w), F32)

    def run_sums(lo_b, hi_b):
        before, acc = {}, zero
        order = range(lo_b, hi_b) if fwd else range(hi_b - 1, lo_b - 1, -1)
        for b in order:
            before[b] = acc
            acc = acc + tot[b]
        after, acc = {}, zero
        for b in reversed(order):
            after[b] = acc
            acc = acc + tot[b]
        return before, after, acc

    def scaled(x, logs):
        zblk = jnp.zeros((bs, w), BF16)
        parts = []
        for b in range(nb):
            xb = x[b * bs:(b + 1) * bs]
            if logs[b] is None:
                parts.append(zblk)
            elif logs[b] is zero:
                parts.append(xb.astype(BF16))
            else:
                parts.append((xb * jnp.exp(logs[b])).astype(BF16))
        return jnp.concatenate(parts, axis=0)

    level_q, level_k = [], []
    for hf in HG_LEVELS:
        nh = hf // bs
        ql, kl = [None] * nb, [None] * nb
        for par in range(nb // (2 * nh)):
            halves = (par * 2 * nh, par * 2 * nh + nh), (par * 2 * nh + nh, (par + 1) * 2 * nh)
            key_half, query_half = halves if fwd else halves[::-1]
            _, after, _ = run_sums(*key_half)
            before, _, _ = run_sums(*query_half)
            for b in range(*key_half):
                kl[b] = after[b]
            for b in range(*query_half):
                ql[b] = before[b]
        level_q.append(scaled(qp, ql))
        level_k.append(scaled(ks, kl))
    s_mid = [t - pm for t, pm in zip(tot, p_mid)]
    diag_q = (qp * spread([jnp.exp(-pm) for pm in p_mid])).astype(BF16)
    diag_k = (ks * spread([jnp.exp(-sm) for sm in s_mid])).astype(BF16)
    before, after, total = run_sums(0, nb)
    state_q = scaled(qp, [before[b] for b in range(nb)])
    state_k = scaled(ks, [after[b] for b in range(nb)])
    return level_q, level_k, diag_q, diag_k, state_q, state_k, jnp.exp(total)


def _hgrn_kernel(dims, nblk, tb, has_init, has_prev, *refs):
    q_ref, lamf_ref, lamb_ref, kf_ref, kb_ref, v_ref, g_ref, nw_ref = refs[:8]
    s0_ref = refs[8] if has_init else None
    o_ref, sfin_ref, st, o_s, qb_s, kb_s, decb_s = refs[8 + has_init + has_prev:]
    hh = dims.hh
    c = HG_C
    nck = tb // c
    s = pl.program_id(1)
    is_bwd = s >= nblk
    blk = jnp.where(is_bwd, 2 * nblk - 1 - s, s)
    first = (s == 0) | (s == nblk)
    last = (s == nblk - 1) | (s == 2 * nblk - 1)

    @pl.when(first)
    def _():
        for hd in range(hh):
            st[hd] = s0_ref[0, 0, hd].T if has_init else jnp.zeros((HG_V, HG_K), F32)

    ti = lax.broadcasted_iota(jnp.int32, (c, c), 0)
    tj = lax.broadcasted_iota(jnp.int32, (c, c), 1)
    same_diag = (ti // HG_DIAG) == (tj // HG_DIAG)
    diag_f = same_diag & (tj <= ti)
    diag_b = same_diag & (tj >= ti)
    mask_df, mask_db = diag_f.astype(F32), diag_b.astype(F32)
    parent = {hf: ((ti // (2 * hf)) == (tj // (2 * hf))).astype(F32) for hf in HG_LEVELS if 2 * hf < c}

    def heads(x):
        return [x[:, hd * HG_K:(hd + 1) * HG_K] for hd in range(hh)]

    def state_step(hd, qs_h, ke_h, v_h, dec_h):
        sh = st[hd]
        off = _dot_nt(qs_h, sh.astype(BF16))
        st[hd] = sh * dec_h + _dot_tn(v_h, ke_h)
        return off

    def fwd_chunk(idx, carry):
        r0 = pl.multiple_of(idx * c, c)
        rows = pl.ds(r0, c)
        g0 = pl.multiple_of(blk * tb + r0, c)
        qq = _silu(q_ref[rows, :].astype(F32))
        vh = heads(v_ref[rows, :])
        lq_f, lk_f, dq_f, dk_f, sq_f, sk_f, dec_f = _hg_dir_operands(lamf_ref[rows, :], kf_ref[rows, :], qq,
                                                                      diag_f.astype(BF16), True)
        lq_b, lk_b, dq_b, dk_b, sq_b, sk_b, dec_b = _hg_dir_operands(lamb_ref[rows, :], kb_ref[rows, :], qq,
                                                                      diag_b.astype(BF16), False)
        qb_s[pl.ds(g0, c), :] = sq_b
        kb_s[pl.ds(g0, c), :] = sk_b
        decb_s[g0 // c] = dec_b
        lq_f, lk_f, lq_b, lk_b = ([heads(x) for x in xs] for xs in (lq_f, lk_f, lq_b, lk_b))
        dq_f, dk_f, dq_b, dk_b, sq_f, sk_f, dec_f = (heads(x) for x in (dq_f, dk_f, dq_b, dk_b, sq_f, sk_f, dec_f))
        outs = []
        for hd in range(hh):
            att = _dot_nt(dq_f[hd], dk_f[hd]) * mask_df + _dot_nt(dq_b[hd], dk_b[hd]) * mask_db
            for li, hf in enumerate(HG_LEVELS):
                p = _dot_nt(jnp.concatenate([lq_f[li][hd], lq_b[li][hd]], axis=1),
                            jnp.concatenate([lk_f[li][hd], lk_b[li][hd]], axis=1))
                att = att + (p * parent[hf] if hf in parent else p)
            o_h = _dot(att.astype(BF16), vh[hd])
            outs.append(o_h + state_step(hd, sq_f[hd], sk_f[hd], vh[hd], dec_f[hd]))
        o_s[pl.ds(g0, c), :] = jnp.concatenate(outs, axis=1)
        return carry

    def bwd_chunk(idx, carry):
        r0 = pl.multiple_of((nck - 1 - idx) * c, c)
        rows = pl.ds(r0, c)
        g0 = pl.multiple_of(blk * tb + r0, c)
        vh = heads(v_ref[rows, :])
        qs = heads(qb_s[pl.ds(g0, c), :])
        ke = heads(kb_s[pl.ds(g0, c), :])
        dec = heads(decb_s[g0 // c])
        o_prev = heads(o_s[pl.ds(g0, c), :])
        gate = heads(_silu(g_ref[rows, :].astype(F32)))
        nw = heads(nw_ref[...])
        outs = []
        for hd in range(hh):
            o_h = o_prev[hd] + state_step(hd, qs[hd], ke[hd], vh[hd], dec[hd])
            o_h = o_h * lax.rsqrt(jnp.mean(o_h * o_h, axis=-1, keepdims=True) + RMS_EPS) * nw[hd]
            outs.append(o_h * gate[hd])
        o_ref[rows, :] = jnp.concatenate(outs, axis=1)
        return carry

    @pl.when(jnp.logical_not(is_bwd))
    def _():
        lax.fori_loop(0, nck, fwd_chunk, 0)

    @pl.when(is_bwd)
    def _():
        lax.fori_loop(0, nck, bwd_chunk, 0)

    @pl.when(last)
    def _():
        for hd in range(hh):
            sfin_ref[0, 0, 0, hd] = st[hd].T


def _hgrn_call(dims, proj, lam, kk, nw, s0, n_seq, seq_len, tok_off, layer=None, fin_prev=None):
    d, hh = dims.d, dims.hh
    n_slab, slab = (1, 0) if layer is None else (dims.depth, layer)
    tb = min(512, seq_len)
    nblk = seq_len // tb
    off = tok_off // tb
    has_init = s0 is not None
    col0 = (dims.d_ssd + 2 * d + dims.d_ssd) // d

    def tok(b, s):
        return off + b * nblk + jnp.where(s < nblk, s, 2 * nblk - 1 - s)

    def tok_pass1(b, s):
        return off + b * nblk + jnp.minimum(s, nblk - 1)

    def otok(b, s):
        return b * nblk + jnp.where(s < nblk, nblk - 1, 2 * nblk - 1 - s)

    def tok_pass2(b, s):
        return off + otok(b, s)

    srcs = ((proj, col0, tok_pass1), (lam, 0, tok_pass1), (lam, 1, tok_pass1), (kk, 0, tok_pass1), (kk, 1, tok_pass1),
            (proj, col0 + 1, tok), (proj, col0 + 2, tok_pass2))
    in_specs = [pl.BlockSpec((tb, d), functools.partial(lambda cc, tk, b, s: (tk(b, s), cc), col, tk))
                for _, col, tk in srcs]
    in_specs += [pl.BlockSpec((1, d), lambda b, s: (0, 0))]
    args = [a for a, _, _ in srcs] + [nw]
    if has_init:
        in_specs.append(pl.BlockSpec((1, 1, hh, HG_K, HG_V), lambda b, s: (b, s // nblk, 0, 0, 0)))
        args.append(s0)
    aliases = {}
    if fin_prev is not None:
        in_specs.append(pl.BlockSpec(memory_space=pl.ANY))
        aliases = {len(args): 1}
        args.append(fin_prev)
    vmem = 2 * 7 * tb * d * 4 + 2 * seq_len * d * 4 + 4 * hh * HG_K * HG_V * 4 + (16 << 20)
    return pl.pallas_call(
        functools.partial(_hgrn_kernel, dims, nblk, tb, has_init, fin_prev is not None),
        grid=(n_seq, 2 * nblk),
        in_specs=in_specs,
        out_specs=(pl.BlockSpec((tb, d), lambda b, s: (otok(b, s), 0)),
                   pl.BlockSpec((1, 1, 1, hh, HG_K, HG_V), lambda b, s: (b, slab, s // nblk, 0, 0, 0))),
        out_shape=(jax.ShapeDtypeStruct((n_seq * seq_len, d), F32),
                   jax.ShapeDtypeStruct((n_seq, n_slab, 2, hh, HG_K, HG_V), F32)),
        input_output_aliases=aliases,
        scratch_shapes=[pltpu.VMEM((hh, HG_V, HG_K), F32),
                        pltpu.VMEM((seq_len, d), F32),
                        pltpu.VMEM((seq_len, d), BF16),
                        pltpu.VMEM((seq_len, d), BF16),
                        pltpu.VMEM((seq_len // HG_C, 1, d), F32)],
        compiler_params=_cparams(("arbitrary", "arbitrary"), vmem),
        name="hgrn_scan",
    )(*args)


def _merge_kernel(dims, tm, yc_ref, yl_ref, oc_ref, ol_ref, g1_ref, g2_ref, x_ref, mod_ref,
                  wso_ref, who_ref, wo_ref, out_ref):
    d = dims.d
    is_ctx = pl.program_id(0) < dims.t_ctx // tm
    y = jnp.where(is_ctx, yc_ref[...], yl_ref[...])
    o = jnp.where(is_ctx, oc_ref[...], ol_ref[...])
    y_ssd = _dot(y.astype(BF16), wso_ref[...])
    y_hg = _dot(o.astype(BF16), who_ref[...])
    merged = _sigmoid(g1_ref[...].astype(F32)) * y_ssd + _sigmoid(g2_ref[...].astype(F32)) * y_hg
    mix = _dot(merged.astype(BF16), wo_ref[...])
    out_ref[...] = x_ref[...] + mod_ref[0][:, 2 * d:3 * d] * mix


def _merge_call(dims, y_ctx, y_lat, o_ctx, o_lat, proj, x, mod_l, wso, who, wo):
    d, ds_, t = dims.d, dims.d_ssd, dims.t
    tm = min(512, dims.row_tile)
    nctx = dims.t_ctx // tm
    gcol = proj.shape[1] // d - 2

    def ctx_blk(i):
        return jnp.minimum(i, nctx - 1)

    def lat_blk(i):
        return jnp.maximum(i - nctx, 0)

    const = lambda i: (0, 0)
    return pl.pallas_call(
        functools.partial(_merge_kernel, dims, tm),
        grid=(t // tm,),
        in_specs=[pl.BlockSpec((tm, ds_), lambda i: (ctx_blk(i), 0)),
                  pl.BlockSpec((tm, ds_), lambda i: (lat_blk(i), 0)),
                  pl.BlockSpec((tm, d), lambda i: (ctx_blk(i), 0)),
                  pl.BlockSpec((tm, d), lambda i: (lat_blk(i), 0)),
                  pl.BlockSpec((tm, d), lambda i: (i, gcol)),
                  pl.BlockSpec((tm, d), lambda i: (i, gcol + 1)),
                  pl.BlockSpec((tm, d), lambda i: (i, 0)),
                  pl.BlockSpec((1, 1, 6 * d), lambda i: (_mod_row(i, tm, dims), 0, 0)),
                  pl.BlockSpec((ds_, d), const), pl.BlockSpec((d, d), const), pl.BlockSpec((d, d), const)],
        out_specs=pl.BlockSpec((tm, d), lambda i: (i, 0)),
        out_shape=jax.ShapeDtypeStruct((t, d), F32),
        compiler_params=_cparams(("arbitrary",), 2 * (2 * tm * ds_ + 6 * tm * d) * 4 + 2 * (ds_ + 2 * d) * d * 2 + (12 << 20)),
        name="merge_outproj",
    )(y_ctx, y_lat, o_ctx, o_lat, proj, proj, x, mod_l, wso, who, wo)


def _ffn_kernel(dims, final_norm, x_ref, mod_ref, nw_ref, w1_ref, w3_ref, w2_ref, *rest):
    d = dims.d
    fw_ref, out_ref = rest if final_norm else (None, rest[0])

    x = x_ref[...]
    y = x * lax.rsqrt(jnp.mean(x * x, axis=-1, keepdims=True) + RMS_EPS) * nw_ref[...]
    mod = mod_ref[0]
    hb = (y * (1.0 + mod[:, 4 * d:5 * d]) + mod[:, 3 * d:4 * d]).astype(BF16)
    acc = None
    for lo in range(0, dims.ffn, MXU_COLS):
        hi = min(lo + MXU_COLS, dims.ffn)
        act = _silu(_dot(hb, w1_ref[:, lo:hi])) * _dot(hb, w3_ref[:, lo:hi])
        part = _dot(act.astype(BF16), w2_ref[lo:hi, :])
        acc = part if acc is None else acc + part
    out = x + mod[:, 5 * d:6 * d] * acc
    if final_norm:
        out = out * lax.rsqrt(jnp.mean(out * out, axis=-1, keepdims=True) + RMS_EPS) * fw_ref[...]
    out_ref[...] = out


def _ffn_call(dims, x, mod_l, nw, w1, w3, w2, final_nw=None):
    d, t, f = dims.d, dims.t, dims.ffn
    tm = min(512, dims.row_tile)
    const = lambda i: (0, 0)
    resident = pl.Buffered(1)
    in_specs = [pl.BlockSpec((tm, d), lambda i: (i, 0)),
                pl.BlockSpec((1, 1, 6 * d), lambda i: (_mod_row(i, tm, dims), 0, 0)),
                pl.BlockSpec((1, d), const),
                pl.BlockSpec((d, f), const, pipeline_mode=resident),
                pl.BlockSpec((d, f), const, pipeline_mode=resident),
                pl.BlockSpec((f, d), const, pipeline_mode=resident)]
    args = [x, mod_l, nw, w1, w3, w2]
    if final_nw is not None:
        in_specs.append(pl.BlockSpec((1, d), const))
        args.append(final_nw)
    return pl.pallas_call(
        functools.partial(_ffn_kernel, dims, final_nw is not None),
        grid=(t // tm,),
        in_specs=in_specs,
        out_specs=pl.BlockSpec((tm, d), lambda i: (i, 0)),
        out_shape=jax.ShapeDtypeStruct((t, d), F32),
        compiler_params=_cparams(("arbitrary",), 4 * tm * d * 4 + 3 * d * f * 2 + 6 * tm * d * 4 + (12 << 20)),
        name="ffn",
    )(*args)


def _forward(dims, x_prompt, x_sample, c, state_ssm, state_hgrn, c_ctx, norm_mix_w, norm_ffn_w, ada_w, ada_b, w_in,
             ssd_conv_w, ssd_conv_b, ssd_dt_bias, ssd_a_log, ssd_d, ssd_norm_w, hgrn_lower_bounds, hgrn_norm_w,
             w_ssd_out, w_hgrn_out, w_out, ffn_w1, ffn_w3, ffn_w2, norm_final_w):
    d, ds_, h = dims.d, dims.d_ssd, dims.h_ssd
    assert dims.db + 1 <= MOD_ROWS and 2 * h <= LANE and dims.nl % HG_C == 0 and dims.dl % 512 == 0
    conv_dim = ds_ + 2 * dims.groups * SSD_N
    x = jnp.concatenate([x_prompt.reshape(dims.t_ctx, d), x_sample.reshape(dims.t_lat, d)], axis=0)
    cvec = jnp.zeros((MOD_ROWS, d), F32).at[0].set(c_ctx).at[1:1 + dims.db].set(c)
    mod = _mod_call(dims, cvec, ada_w, ada_b)
    lb_all = jnp.cumsum(jax.nn.softmax(hgrn_lower_bounds.astype(F32), axis=0), axis=0)
    lb_all = lb_all - lb_all[:1]
    dt_cols = slice(conv_dim + ds_, conv_dim + ds_ + 2 * h)

    s_ssm = s_hg = None
    for l in range(dims.depth):
        mod_l = mod[l].reshape(MOD_ROWS, 1, 6 * d)
        w_l = w_in[l]
        w_a = w_l[:, :dt_cols.start].astype(BF16)
        w_b = w_l[:, dt_cols.stop:].astype(BF16)
        wdt_t = jnp.zeros((LANE, d), F32).at[:2 * h].set(w_l[:, dt_cols].T).astype(BF16)
        dtb = jnp.zeros((LANE, 1), F32).at[:2 * h, 0].set(ssd_dt_bias[l].reshape(-1))
        alog = jnp.zeros((LANE, 1), F32).at[:2 * h, 0].set(ssd_a_log[l].reshape(-1))
        lb = jnp.zeros((8, d), F32).at[:2].set(lb_all[l])
        proj, lam, kk, dt_t, da_t = _inproj_call(dims, x, mod_l, norm_mix_w[l].reshape(1, d), w_a, w_b, ssd_conv_w[l],
                                                 ssd_conv_b[l].reshape(1, conv_dim), wdt_t, dtb, alog, lb)
        dskip = jnp.broadcast_to(jnp.repeat(ssd_d[l], SSD_P)[:, None], (ds_, SSD_Q))
        ssd_nw = jnp.broadcast_to(ssd_norm_w[l][:, None], (ds_, SSD_Q))
        y_ctx, s_ssm = _ssd_call(dims, proj, dt_t, da_t, dskip, ssd_nw, None, dims.nb, dims.nl, 0, l, s_ssm)
        y_lat, _ = _ssd_call(dims, proj, dt_t, da_t, dskip, ssd_nw,
                             state_ssm[:, l].reshape(dims.db, 2, ds_, SSD_N), dims.db, dims.dl, dims.t_ctx)
        hg_nw = hgrn_norm_w[l].reshape(1, d)
        o_ctx, s_hg = _hgrn_call(dims, proj, lam, kk, hg_nw, None, dims.nb, dims.nl, 0, l, s_hg)
        o_lat, _ = _hgrn_call(dims, proj, lam, kk, hg_nw, state_hgrn[:, l], dims.db, dims.dl, dims.t_ctx)
        x = _merge_call(dims, y_ctx, y_lat, o_ctx, o_lat, proj, x, mod_l,
                        w_ssd_out[l].astype(BF16), w_hgrn_out[l].astype(BF16), w_out[l].astype(BF16))
        x = _ffn_call(dims, x, mod_l, norm_ffn_w[l].reshape(1, d),
                      ffn_w1[l].astype(BF16), ffn_w3[l].astype(BF16), ffn_w2[l].astype(BF16),
                      norm_final_w.reshape(1, d) if l == dims.depth - 1 else None)
    y_prompt = x[:dims.t_ctx].reshape(dims.nb, dims.nl, d)
    y_sample = x[dims.t_ctx:].reshape(dims.db, dims.dl, d)
    return (y_prompt, y_sample, s_ssm.reshape(dims.nb, dims.depth, 2, h, SSD_P, SSD_N), s_hg)


def kernel(x_prompt, x_sample, c, state_ssm, state_hgrn, c_ctx, norm_mix_w, norm_ffn_w, ada_w, ada_b, w_in,
           ssd_conv_w, ssd_conv_b, ssd_dt_bias, ssd_a_log, ssd_d, ssd_norm_w, hgrn_lower_bounds, hgrn_norm_w,
           w_ssd_out, w_hgrn_out, w_out, ffn_w1, ffn_w3, ffn_w2, norm_final_w):
    dims = Dims(d=x_prompt.shape[2], nb=x_prompt.shape[0], nl=x_prompt.shape[1], db=x_sample.shape[0],
                dl=x_sample.shape[1], depth=w_in.shape[0], ffn=ffn_w1.shape[2])
    return _forward(dims, x_prompt, x_sample, c, state_ssm, state_hgrn, c_ctx, norm_mix_w, norm_ffn_w, ada_w, ada_b,
                    w_in, ssd_conv_w, ssd_conv_b, ssd_dt_bias, ssd_a_log, ssd_d, ssd_norm_w, hgrn_lower_bounds,
                    hgrn_norm_w, w_ssd_out, w_hgrn_out, w_out, ffn_w1, ffn_w3, ffn_w2, norm_final_w)
```

```python
import functools
from typing import NamedTuple

import jax
import jax.numpy as jnp
from jax import lax
from jax.experimental import pallas as pl
from jax.experimental.pallas import tpu as pltpu

F32 = jnp.float32
BF16 = jnp.bfloat16

LANE = 128
MXU_COLS = 256
VMEM_BYTES = 64 * 1024 * 1024
RMS_EPS = 1e-6
D_CONV = 5
GRID_W = 64
SSD_P = 64
SSD_N = 128
SSD_Q = 128
SSD_R = 4
HG_K = 128
HG_V = 128
HG_C = 128
HG_LEVELS = (64, 32, 16)
HG_DIAG = 16
HF_BLOCKS = 2
MOD_ROWS = 8


class Dims(NamedTuple):
    d: int
    nb: int
    nl: int
    db: int
    dl: int
    depth: int
    ffn: int

    @property
    def t_ctx(self):
        return self.nb * self.nl

    @property
    def t_lat(self):
        return self.db * self.dl

    @property
    def t(self):
        return self.t_ctx + self.t_lat

    @property
    def d_ssd(self):
        return 2 * self.d

    @property
    def h_ssd(self):
        return self.d_ssd // SSD_P

    @property
    def groups(self):
        return self.h_ssd // SSD_R

    @property
    def hh(self):
        return self.d // HG_K

    @property
    def row_tile(self):
        tm = min(1024, self.dl, self.t_ctx)
        assert self.dl % tm == 0 and self.t_ctx % tm == 0 and tm % self.nl == 0 and tm % GRID_W == 0
        return tm


def _cparams(semantics, vmem_bytes):
    return pltpu.CompilerParams(dimension_semantics=semantics, vmem_limit_bytes=min(vmem_bytes, VMEM_BYTES - (4 << 20)))


def _dot(a, b):
    return jnp.dot(a, b, preferred_element_type=F32)


def _dot_nt(a, b):
    return lax.dot_general(a, b, (((1,), (1,)), ((), ())), preferred_element_type=F32)


def _dot_tn(a, b):
    return lax.dot_general(a, b, (((0,), (0,)), ((), ())), preferred_element_type=F32)


def _split3(x):
    hi = x.astype(BF16)
    r = x - hi.astype(F32)
    mid = r.astype(BF16)
    lo = (r - mid.astype(F32)).astype(BF16)
    return hi, mid, lo


def _sum01_right(x, m01):
    hi, mid, lo = _split3(x)
    return _dot(hi, m01) + _dot(mid, m01) + _dot(lo, m01)


def _sigmoid(x):
    return 0.5 * jnp.tanh(0.5 * x) + 0.5


def _silu(x):
    return x * _sigmoid(x)


def _softplus(x):
    return jnp.maximum(x, 0.0) + jnp.log1p(jnp.exp(-jnp.abs(x)))


def _mod_row(i, tm, dims):
    nctx = dims.t_ctx // tm
    return jnp.where(i < nctx, 0, 1 + ((i - nctx) * tm) // dims.dl)


def _mod_kernel(c_ref, w_ref, b_ref, o_ref):
    s = _silu(c_ref[...])
    o_ref[0] = _dot(s.astype(BF16), w_ref[0].astype(BF16)) + b_ref[0]


def _mod_call(dims, cvec, ada_w, ada_b):
    d, n = dims.d, 6 * dims.d
    tn = n // 4
    return pl.pallas_call(
        _mod_kernel,
        grid=(dims.depth, n // tn),
        in_specs=[pl.BlockSpec((MOD_ROWS, d), lambda l, j: (0, 0)),
                  pl.BlockSpec((1, d, tn), lambda l, j: (l, 0, j)),
                  pl.BlockSpec((1, 1, tn), lambda l, j: (l, 0, j))],
        out_specs=pl.BlockSpec((1, MOD_ROWS, tn), lambda l, j: (l, 0, j)),
        out_shape=jax.ShapeDtypeStruct((dims.depth, MOD_ROWS, n), F32),
        compiler_params=_cparams(("arbitrary", "arbitrary"), 2 * d * tn * 4 + (8 << 20)),
        name="adaln_mod",
    )(cvec, ada_w, ada_b.reshape(dims.depth, 1, n))


def _inproj_kernel(dims, tm, x_ref, mod_ref, nw_ref, wa_ref, wb_ref, cw_ref, cb_ref, wdt_ref, dtb_ref, alog_ref, lb_ref,
                   proj_ref, lam_ref, kk_ref, dt_ref, da_ref, h_s):
    d = dims.d
    i = pl.program_id(0)
    j = pl.program_id(1)
    n_conv = (dims.d_ssd + 2 * dims.groups * SSD_N) // d
    n_a = n_conv + dims.d_ssd // d
    n_bf = pl.num_programs(1) - HF_BLOCKS

    @pl.when(j == 0)
    def _():
        x = x_ref[...]
        y = x * lax.rsqrt(jnp.mean(x * x, axis=-1, keepdims=True) + RMS_EPS) * nw_ref[...]
        mod = mod_ref[0]
        hb = (y * (1.0 + mod[:, d:2 * d]) + mod[:, 0:d]).astype(BF16)
        h_s[...] = hb
        wdt_t = wdt_ref[...].T.astype(BF16)
        dt = _softplus(_dot_nt(wdt_t, hb) + dtb_ref[...])
        da = dt * -jnp.exp(alog_ref[...])
        for k in range(tm // SSD_Q):
            dt_ref[k] = dt[:, k * SSD_Q:(k + 1) * SSD_Q]
            da_ref[k] = da[:, k * SSD_Q:(k + 1) * SSD_Q]

    @pl.when(j >= n_bf)
    def _():
        lam, kk = _hgrn_gates(_dot(h_s[...], wb_ref[...]), lb_ref[pl.ds(j - n_bf, 1), :])
        lam_ref[...] = lam
        kk_ref[...] = kk.astype(BF16)

    @pl.when((j >= n_conv) & (j < n_a))
    def _():
        proj_ref[...] = _dot(h_s[...], wa_ref[...]).astype(BF16)

    @pl.when((j >= n_a) & (j < n_bf))
    def _():
        proj_ref[...] = _dot(h_s[...], wb_ref[...]).astype(BF16)

    def conv_silu(seg):
        ns, half, sub = tm // seg, D_CONV // 2, 8
        n8 = ns * sub
        r8 = lax.broadcasted_iota(jnp.int32, (n8, LANE), 0) & (sub - 1)
        res = None
        for sl in range(d // LANE):
            cols = slice(sl * LANE, (sl + 1) * LANE)
            if sl % (MXU_COLS // LANE) == 0:
                res = _dot(h_s[...], wa_ref[:, sl * LANE:sl * LANE + MXU_COLS])
            u = res[:, (sl * LANE) % MXU_COLS:(sl * LANE) % MXU_COLS + LANE]
            w = [cw_ref[k:k + 1, cols] for k in range(D_CONV)]
            acc = cb_ref[:, cols] + w[half] * u
            for k in range(D_CONV):
                if k != half:
                    acc = acc + w[k] * pltpu.roll(u, (half - k) % tm, 0)
            u3 = u.reshape(ns, seg, LANE)
            a3 = acc.reshape(ns, seg, LANE)
            edge_lo = u3[:, 0:sub, :].reshape(n8, LANE)
            edge_hi = u3[:, seg - sub:seg, :].reshape(n8, LANE)
            leak_lo = (w[0] * jnp.where(r8 < 2, pltpu.roll(edge_hi, 2, 0), 0.0)
                       + w[1] * jnp.where(r8 < 1, pltpu.roll(edge_hi, 1, 0), 0.0))
            leak_hi = (w[4] * jnp.where(r8 >= sub - 2, pltpu.roll(edge_lo, n8 - 2, 0), 0.0)
                       + w[3] * jnp.where(r8 >= sub - 1, pltpu.roll(edge_lo, n8 - 1, 0), 0.0))
            a3 = jnp.concatenate([a3[:, 0:sub, :] - leak_lo.reshape(ns, sub, LANE),
                                  a3[:, sub:seg - sub, :],
                                  a3[:, seg - sub:seg, :] - leak_hi.reshape(ns, sub, LANE)], axis=1)
            proj_ref[:, cols] = _silu(a3.reshape(tm, LANE)).astype(BF16)

    is_ctx = i < dims.t_ctx // tm

    @pl.when((j < n_conv) & is_ctx)
    def _():
        conv_silu(dims.nl)

    @pl.when((j < n_conv) & jnp.logical_not(is_ctx))
    def _():
        conv_silu(GRID_W)


def _inproj_call(dims, x, mod_l, nw, wa, wb, cw, cb, wdt, dtb, alog, lb):
    d, t, tm = dims.d, dims.t, dims.row_tile
    n_conv = (dims.d_ssd + 2 * dims.groups * SSD_N) // d
    n_a = wa.shape[1] // d
    ncol = n_a + wb.shape[1] // d
    nchunk = t // SSD_Q
    n_bf = ncol - HF_BLOCKS

    def wb_block(j):
        return jnp.where(j <= n_a, 0, jnp.where(j < n_bf, j - n_a + HF_BLOCKS, j - n_bf + 1))

    out_shapes = (jax.ShapeDtypeStruct((t, n_bf * d), BF16),
                  jax.ShapeDtypeStruct((t, HF_BLOCKS * d), F32),
                  jax.ShapeDtypeStruct((t, HF_BLOCKS * d), BF16),
                  jax.ShapeDtypeStruct((nchunk, LANE, SSD_Q), F32),
                  jax.ShapeDtypeStruct((nchunk, LANE, SSD_Q), F32))
    kc = tm // SSD_Q
    gate_spec = pl.BlockSpec((tm, d), lambda i, j: (i, jnp.maximum(j - n_bf, 0)))
    return pl.pallas_call(
        functools.partial(_inproj_kernel, dims, tm),
        grid=(t // tm, ncol),
        in_specs=[pl.BlockSpec((tm, d), lambda i, j: (i, 0)),
                  pl.BlockSpec((1, 1, 6 * d), lambda i, j: (_mod_row(i, tm, dims), 0, 0)),
                  pl.BlockSpec((1, d), lambda i, j: (0, 0)),
                  pl.BlockSpec((d, d), lambda i, j: (0, jnp.minimum(j, n_a - 1))),
                  pl.BlockSpec((d, d), lambda i, j: (0, wb_block(j))),
                  pl.BlockSpec((D_CONV, d), lambda i, j: (0, jnp.minimum(j, n_conv - 1))),
                  pl.BlockSpec((1, d), lambda i, j: (0, jnp.minimum(j, n_conv - 1))),
                  pl.BlockSpec((d, LANE), lambda i, j: (0, 0)),
                  pl.BlockSpec((LANE, 1), lambda i, j: (0, 0)),
                  pl.BlockSpec((LANE, 1), lambda i, j: (0, 0)),
                  pl.BlockSpec((8, d), lambda i, j: (0, 0))],
        out_specs=(pl.BlockSpec((tm, d), lambda i, j: (i, jnp.minimum(j, n_bf - 1))),
                   gate_spec, gate_spec,
                   pl.BlockSpec((kc, LANE, SSD_Q), lambda i, j: (i, 0, 0)),
                   pl.BlockSpec((kc, LANE, SSD_Q), lambda i, j: (i, 0, 0))),
        out_shape=out_shapes,
        scratch_shapes=[pltpu.VMEM((tm, d), BF16)],
        compiler_params=_cparams(("arbitrary", "arbitrary"), 8 * tm * d * 4 + 4 * d * d * 2 + tm * d * 2 + (8 << 20)),
        name="inproj",
    )(x, mod_l, nw, wa, wb, cw, cb, wdt, dtb, alog, lb)


def _ssd_kernel(dims, nchunk, has_init, has_prev, *refs):
    x_ref, b_ref, c_ref, z_ref, dt_ref, da_ref, dsk_ref, nw_ref = refs[:8]
    s0_ref = refs[8] if has_init else None
    (y_ref, sfin_ref, st, yf_s, xt_s, yb_s, bg_s, cg_s, cs_s, ecs_s, wend_s, dec_s,
     dtr_s) = refs[8 + has_init + has_prev:]
    h, g_n = dims.h_ssd, dims.groups
    q = SSD_Q
    s = pl.program_id(1)
    is_bwd = s >= nchunk
    ck = jnp.where(is_bwd, 2 * nchunk - 1 - s, s)
    first = (s == 0) | (s == nchunk)
    last = (s == nchunk - 1) | (s == 2 * nchunk - 1)

    @pl.when(first)
    def _():
        st[...] = s0_ref[0, 0] if has_init else jnp.zeros(st.shape, F32)

    row0 = pl.multiple_of(jnp.where(is_bwd, h, 0), 8)
    dtr = dt_ref[0, pl.ds(row0, h), :]
    dar = da_ref[0, pl.ds(row0, h), :]
    jj = lax.broadcasted_iota(jnp.int32, (q, q), 0)
    ii = lax.broadcasted_iota(jnp.int32, (q, q), 1)
    seen = jnp.where(is_bwd, jj - ii, ii - jj) >= 0
    cs = _sum01_right(dar, seen.astype(BF16))
    tot = _sum01_right(dar, jnp.ones((q, q), BF16))
    cs_s[...] = cs
    ecs_s[...] = jnp.exp(cs)
    wend_s[...] = jnp.exp(tot - cs) * dtr
    dec_s[...] = jnp.exp(tot)
    dtr_s[...] = dtr
    xt_s[...] = x_ref[...].astype(F32).T
    for g in range(g_n):
        bg_s[g] = b_ref[:, g * SSD_N:(g + 1) * SSD_N]
        cg_s[g] = c_ref[:, g * SSD_N:(g + 1) * SSD_N]

    gp = SSD_R * SSD_P

    def group_body(g, carry):
        bg = bg_s[g]
        cg = cg_s[g]
        cbt = _dot_nt(bg, cg)
        r0 = pl.multiple_of(g * gp, gp)
        sg = st[pl.ds(r0, gp), :]
        sc = _dot_nt(sg.astype(BF16), cg)
        xg = xt_s[pl.ds(r0, gp), :]
        xw, decs = [], []
        for r in range(SSD_R):
            hd = g * SSD_R + r
            xh = xg[r * SSD_P:(r + 1) * SSD_P]
            cs_i = cs_s[pl.ds(hd, 1), :]
            cs_j = jnp.broadcast_to(cs_i, (q, q)).T
            decay = jnp.exp(jnp.where(seen, cs_i - cs_j, -jnp.inf))
            mt = (cbt * decay).astype(BF16)
            xdt = (xh * dtr_s[pl.ds(hd, 1), :]).astype(BF16)
            yh = _dot(xdt, mt) + sc[r * SSD_P:(r + 1) * SSD_P] * ecs_s[pl.ds(hd, 1), :]
            yb_s[pl.ds(pl.multiple_of(r0 + r * SSD_P, SSD_P), SSD_P), :] = yh
            xw.append((xh * wend_s[pl.ds(hd, 1), :]).astype(BF16))
            decs.append(jnp.broadcast_to(dec_s[pl.ds(hd, 1), :], (SSD_P, SSD_N)))
        upd = _dot(jnp.concatenate(xw, axis=0), bg)
        st[pl.ds(r0, gp), :] = sg * jnp.concatenate(decs, axis=0) + upd
        return carry

    lax.fori_loop(0, g_n, group_body, 0, unroll=True)

    @pl.when(jnp.logical_not(is_bwd))
    def _():
        yf_s[ck] = yb_s[...]

    @pl.when(is_bwd)
    def _():
        yt = yb_s[...] + yf_s[ck] + dsk_ref[...] * xt_s[...]
        yt = yt * _silu(z_ref[...].astype(F32).T)
        y3 = yt.reshape(g_n, gp, q)
        yn = y3 * lax.rsqrt(jnp.mean(y3 * y3, axis=1, keepdims=True) + RMS_EPS)
        y_ref[...] = (yn.reshape(dims.d_ssd, q) * nw_ref[...]).T

    @pl.when(last)
    def _():
        sfin_ref[0, 0, 0] = st[...]


def _ssd_call(dims, proj, dt_t, da_t, dskip, nw, s0, n_seq, seq_len, tok_off, layer=None, fin_prev=None):
    d, ds_ = dims.d, dims.d_ssd
    n_slab, slab = (1, 0) if layer is None else (dims.depth, layer)
    q = SSD_Q
    nchunk = seq_len // q
    off = tok_off // q
    has_init = s0 is not None

    def ck(s):
        return jnp.where(s < nchunk, s, 2 * nchunk - 1 - s)

    def tok(b, s):
        return off + b * nchunk + ck(s)

    def ytok(b, s):
        return b * nchunk + jnp.where(s < nchunk, nchunk - 1, 2 * nchunk - 1 - s)

    in_specs = [pl.BlockSpec((q, ds_), lambda b, s: (tok(b, s), 0)),
                pl.BlockSpec((q, d), lambda b, s: (tok(b, s), ds_ // d)),
                pl.BlockSpec((q, d), lambda b, s: (tok(b, s), ds_ // d + 1)),
                pl.BlockSpec((q, ds_), lambda b, s: (off + ytok(b, s), (ds_ + 2 * d) // ds_)),
                pl.BlockSpec((1, LANE, q), lambda b, s: (tok(b, s), 0, 0)),
                pl.BlockSpec((1, LANE, q), lambda b, s: (tok(b, s), 0, 0)),
                pl.BlockSpec((ds_, q), lambda b, s: (0, 0)),
                pl.BlockSpec((ds_, q), lambda b, s: (0, 0))]
    args = [proj, proj, proj, proj, dt_t, da_t, dskip, nw]
    if has_init:
        in_specs.append(pl.BlockSpec((1, 1, ds_, SSD_N), lambda b, s: (b, s // nchunk, 0, 0)))
        args.append(s0)
    aliases = {}
    if fin_prev is not None:
        in_specs.append(pl.BlockSpec(memory_space=pl.ANY))
        aliases = {len(args): 1}
        args.append(fin_prev)
    h = dims.h_ssd
    scratch = [pltpu.VMEM((ds_, SSD_N), F32),
               pltpu.VMEM((nchunk, ds_, q), F32),
               pltpu.VMEM((ds_, q), F32),
               pltpu.VMEM((ds_, q), F32),
               pltpu.VMEM((dims.groups, q, SSD_N), BF16),
               pltpu.VMEM((dims.groups, q, SSD_N), BF16),
               pltpu.VMEM((h, q), F32), pltpu.VMEM((h, q), F32), pltpu.VMEM((h, q), F32),
               pltpu.VMEM((h, q), F32), pltpu.VMEM((h, q), F32)]
    vmem = (2 * (3 * q * ds_ + 2 * q * d + 2 * LANE * q + 2 * ds_ * q + 2 * ds_ * SSD_N) * 4
            + (nchunk + 3) * ds_ * q * 4 + (12 << 20))
    return pl.pallas_call(
        functools.partial(_ssd_kernel, dims, nchunk, has_init, fin_prev is not None),
        grid=(n_seq, 2 * nchunk),
        in_specs=in_specs,
        out_specs=(pl.BlockSpec((q, ds_), lambda b, s: (ytok(b, s), 0)),
                   pl.BlockSpec((1, 1, 1, ds_, SSD_N), lambda b, s: (b, slab, s // nchunk, 0, 0))),
        out_shape=(jax.ShapeDtypeStruct((n_seq * seq_len, ds_), F32),
                   jax.ShapeDtypeStruct((n_seq, n_slab, 2, ds_, SSD_N), F32)),
        input_output_aliases=aliases,
        scratch_shapes=scratch,
        compiler_params=_cparams(("arbitrary", "arbitrary"), vmem),
        name="ssd_scan",
    )(*args)


def _hgrn_gates(u, lb):
    t = jnp.exp(-jnp.abs(u))
    pos = u >= 0.0
    den = 1.0 + t
    num = jnp.where(pos, 1.0 + lb * t, lb + t)
    log_num = jnp.where(pos | (jnp.broadcast_to(lb, u.shape) > 0.0), jnp.log(num), u)
    return log_num - jnp.log(den), (1.0 - lb) * jnp.where(pos, t, 1.0) / den


def _hg_dir_operands(lam, kk, qq, blockmat, fwd):
    c, w = lam.shape
    bs, nb = HG_DIAG, lam.shape[0] // HG_DIAG
    lam_hi = lam.astype(BF16)
    lam_lo = (lam - lam_hi.astype(F32)).astype(BF16)
    p = _dot(blockmat, lam_hi) + _dot(blockmat, lam_lo)
    edge, mid = (bs - 1, bs // 2 - 1) if fwd else (0, bs // 2)
    tot = [p[b * bs + edge:b * bs + edge + 1] for b in range(nb)]
    p_mid = [p[b * bs + mid:b * bs + mid + 1] for b in range(nb)]

    def spread(rows):
        return jnp.concatenate([jnp.broadcast_to(r, (bs, w)) for r in rows], axis=0)

    s = spread(tot) - p
    qp = qq * jnp.exp(p)
    ks = kk * jnp.exp(s)
    zero = jnp.zeros((1, w), F32)

    def run_sums(lo_b, hi_b):
        before, acc = {}, zero
        order = range(lo_b, hi_b) if fwd else range(hi_b - 1, lo_b - 1, -1)
        for b in order:
            before[b] = acc
            acc = acc + tot[b]
        after, acc = {}, zero
        for b in reversed(order):
            after[b] = acc
            acc = acc + tot[b]
        return before, after, acc

    def scaled(x, logs):
        zblk = jnp.zeros((bs, w), BF16)
        parts = []
        for b in range(nb):
            xb = x[b * bs:(b + 1) * bs]
            if logs[b] is None:
                parts.append(zblk)
            elif logs[b] is zero:
                parts.append(xb.astype(BF16))
            else:
                parts.append((xb * jnp.exp(logs[b])).astype(BF16))
        return jnp.concatenate(parts, axis=0)

    level_q, level_k = [], []
    for hf in HG_LEVELS:
        nh = hf // bs
        ql, kl = [None] * nb, [None] * nb
        for par in range(nb // (2 * nh)):
            halves = (par * 2 * nh, par * 2 * nh + nh), (par * 2 * nh + nh, (par + 1) * 2 * nh)
            key_half, query_half = halves if fwd else halves[::-1]
            _, after, _ = run_sums(*key_half)
            before, _, _ = run_sums(*query_half)
            for b in range(*key_half):
                kl[b] = after[b]
            for b in range(*query_half):
                ql[b] = before[b]
        level_q.append(scaled(qp, ql))
        level_k.append(scaled(ks, kl))
    s_mid = [t - pm for t, pm in zip(tot, p_mid)]
    diag_q = (qp * spread([jnp.exp(-pm) for pm in p_mid])).astype(BF16)
    diag_k = (ks * spread([jnp.exp(-sm) for sm in s_mid])).astype(BF16)
    before, after, total = run_sums(0, nb)
    state_q = scaled(qp, [before[b] for b in range(nb)])
    state_k = scaled(ks, [after[b] for b in range(nb)])
    return level_q, level_k, diag_q, diag_k, state_q, state_k, jnp.exp(total)


def _hgrn_kernel(dims, nblk, tb, has_init, has_prev, *refs):
    q_ref, lamf_ref, lamb_ref, kf_ref, kb_ref, v_ref, g_ref, nw_ref = refs[:8]
    s0_ref = refs[8] if has_init else None
    o_ref, sfin_ref, st, o_s, qb_s, kb_s, decb_s = refs[8 + has_init + has_prev:]
    hh = dims.hh
    c = HG_C
    nck = tb // c
    single = nblk == 1
    s = pl.program_id(1)
    is_bwd = s >= nblk
    blk = 0 if single else jnp.where(is_bwd, 2 * nblk - 1 - s, s)

    def init_state(direction):
        for hd in range(hh):
            st[hd] = s0_ref[0, direction, hd].T if has_init else jnp.zeros((HG_V, HG_K), F32)

    def save_state(direction):
        for hd in range(hh):
            sfin_ref[0, 0, direction, hd] = st[hd].T

    ti = lax.broadcasted_iota(jnp.int32, (c, c), 0)
    tj = lax.broadcasted_iota(jnp.int32, (c, c), 1)
    same_diag = (ti // HG_DIAG) == (tj // HG_DIAG)
    diag_f = same_diag & (tj <= ti)
    diag_b = same_diag & (tj >= ti)
    mask_df, mask_db = diag_f.astype(F32), diag_b.astype(F32)
    parent = {hf: ((ti // (2 * hf)) == (tj // (2 * hf))).astype(F32) for hf in HG_LEVELS if 2 * hf < c}

    def heads(x):
        return [x[:, hd * HG_K:(hd + 1) * HG_K] for hd in range(hh)]

    def state_step(hd, qs_h, ke_h, v_h, dec_h):
        sh = st[hd]
        off = _dot_nt(qs_h, sh.astype(BF16))
        st[hd] = sh * dec_h + _dot_tn(v_h, ke_h)
        return off

    def fwd_chunk(idx, carry):
        r0 = pl.multiple_of(idx * c, c)
        rows = pl.ds(r0, c)
        g0 = pl.multiple_of(blk * tb + r0, c)
        qq = _silu(q_ref[rows, :].astype(F32))
        vh = heads(v_ref[rows, :])
        lq_f, lk_f, dq_f, dk_f, sq_f, sk_f, dec_f = _hg_dir_operands(
            lamf_ref[rows, :], kf_ref[rows, :].astype(F32), qq, diag_f.astype(BF16), True)
        lq_b, lk_b, dq_b, dk_b, sq_b, sk_b, dec_b = _hg_dir_operands(
            lamb_ref[rows, :], kb_ref[rows, :].astype(F32), qq, diag_b.astype(BF16), False)
        qb_s[pl.ds(g0, c), :] = sq_b
        kb_s[pl.ds(g0, c), :] = sk_b
        decb_s[g0 // c] = dec_b
        lq_f, lk_f, lq_b, lk_b = ([heads(x) for x in xs] for xs in (lq_f, lk_f, lq_b, lk_b))
        dq_f, dk_f, dq_b, dk_b, sq_f, sk_f, dec_f = (heads(x) for x in (dq_f, dk_f, dq_b, dk_b, sq_f, sk_f, dec_f))
        outs = []
        for hd in range(hh):
            att = _dot_nt(dq_f[hd], dk_f[hd]) * mask_df + _dot_nt(dq_b[hd], dk_b[hd]) * mask_db
            for li, hf in enumerate(HG_LEVELS):
                p = _dot_nt(jnp.concatenate([lq_f[li][hd], lq_b[li][hd]], axis=1),
                            jnp.concatenate([lk_f[li][hd], lk_b[li][hd]], axis=1))
                att = att + (p * parent[hf] if hf in parent else p)
            o_h = _dot(att.astype(BF16), vh[hd])
            outs.append(o_h + state_step(hd, sq_f[hd], sk_f[hd], vh[hd], dec_f[hd]))
        o_s[pl.ds(g0, c), :] = jnp.concatenate(outs, axis=1)
        return carry

    def bwd_chunk(idx, carry):
        r0 = pl.multiple_of((nck - 1 - idx) * c, c)
        rows = pl.ds(r0, c)
        g0 = pl.multiple_of(blk * tb + r0, c)
        vh = heads(v_ref[rows, :])
        qs = heads(qb_s[pl.ds(g0, c), :])
        ke = heads(kb_s[pl.ds(g0, c), :])
        dec = heads(decb_s[g0 // c])
        o_prev = heads(o_s[pl.ds(g0, c), :])
        gate = heads(_silu(g_ref[rows, :].astype(F32)))
        nw = heads(nw_ref[...])
        outs = []
        for hd in range(hh):
            o_h = o_prev[hd] + state_step(hd, qs[hd], ke[hd], vh[hd], dec[hd])
            o_h = o_h * lax.rsqrt(jnp.mean(o_h * o_h, axis=-1, keepdims=True) + RMS_EPS) * nw[hd]
            outs.append(o_h * gate[hd])
        o_ref[rows, :] = jnp.concatenate(outs, axis=1)
        return carry

    if single:
        init_state(0)
        lax.fori_loop(0, nck, fwd_chunk, 0)
        save_state(0)
        init_state(1)
        lax.fori_loop(0, nck, bwd_chunk, 0)
        save_state(1)
        return

    @pl.when(s == 0)
    def _():
        init_state(0)

    @pl.when(s == nblk)
    def _():
        init_state(1)

    @pl.when(jnp.logical_not(is_bwd))
    def _():
        lax.fori_loop(0, nck, fwd_chunk, 0)

    @pl.when(is_bwd)
    def _():
        lax.fori_loop(0, nck, bwd_chunk, 0)

    @pl.when(s == nblk - 1)
    def _():
        save_state(0)

    @pl.when(s == 2 * nblk - 1)
    def _():
        save_state(1)


def _hgrn_call(dims, proj, lam, kk, nw, s0, n_seq, seq_len, tok_off, layer=None, fin_prev=None):
    d, hh = dims.d, dims.hh
    n_slab, slab = (1, 0) if layer is None else (dims.depth, layer)
    tb = min(512, seq_len)
    nblk = seq_len // tb
    off = tok_off // tb
    has_init = s0 is not None
    col0 = (dims.d_ssd + 2 * d + dims.d_ssd) // d

    def tok(b, s):
        return off + b * nblk + jnp.where(s < nblk, s, 2 * nblk - 1 - s)

    def tok_pass1(b, s):
        return off + b * nblk + jnp.minimum(s, nblk - 1)

    def otok(b, s):
        return b * nblk + jnp.where(s < nblk, nblk - 1, 2 * nblk - 1 - s)

    def tok_pass2(b, s):
        return off + otok(b, s)

    srcs = ((proj, col0, tok_pass1), (lam, 0, tok_pass1), (lam, 1, tok_pass1), (kk, 0, tok_pass1), (kk, 1, tok_pass1),
            (proj, col0 + 1, tok), (proj, col0 + 2, tok_pass2))
    in_specs = [pl.BlockSpec((tb, d), functools.partial(lambda cc, tk, b, s: (tk(b, s), cc), col, tk))
                for _, col, tk in srcs]
    in_specs += [pl.BlockSpec((1, d), lambda b, s: (0, 0))]
    args = [a for a, _, _ in srcs] + [nw]
    if has_init:
        in_specs.append(pl.BlockSpec((1, 2, hh, HG_K, HG_V), lambda b, s: (b, 0, 0, 0, 0)))
        args.append(s0)
    aliases = {}
    if fin_prev is not None:
        in_specs.append(pl.BlockSpec(memory_space=pl.ANY))
        aliases = {len(args): 1}
        args.append(fin_prev)
    vmem = 2 * 7 * tb * d * 4 + 2 * seq_len * d * 4 + 4 * hh * HG_K * HG_V * 4 + (16 << 20)
    return pl.pallas_call(
        functools.partial(_hgrn_kernel, dims, nblk, tb, has_init, fin_prev is not None),
        grid=(n_seq, 1 if nblk == 1 else 2 * nblk),
        in_specs=in_specs,
        out_specs=(pl.BlockSpec((tb, d), lambda b, s: (otok(b, s), 0)),
                   pl.BlockSpec((1, 1, 2, hh, HG_K, HG_V), lambda b, s: (b, slab, 0, 0, 0, 0))),
        out_shape=(jax.ShapeDtypeStruct((n_seq * seq_len, d), F32),
                   jax.ShapeDtypeStruct((n_seq, n_slab, 2, hh, HG_K, HG_V), F32)),
        input_output_aliases=aliases,
        scratch_shapes=[pltpu.VMEM((hh, HG_V, HG_K), F32),
                        pltpu.VMEM((seq_len, d), F32),
                        pltpu.VMEM((seq_len, d), BF16),
                        pltpu.VMEM((seq_len, d), BF16),
                        pltpu.VMEM((seq_len // HG_C, 1, d), F32)],
        compiler_params=_cparams(("arbitrary", "arbitrary"), vmem),
        name="hgrn_scan",
    )(*args)


def _merge_kernel(dims, tm, yc_ref, yl_ref, oc_ref, ol_ref, g1_ref, g2_ref, x_ref, mod_ref,
                  wso_ref, who_ref, wo_ref, out_ref):
    d = dims.d
    is_ctx = pl.program_id(0) < dims.t_ctx // tm
    y = jnp.where(is_ctx, yc_ref[...], yl_ref[...])
    o = jnp.where(is_ctx, oc_ref[...], ol_ref[...])
    y_ssd = _dot(y.astype(BF16), wso_ref[...])
    y_hg = _dot(o.astype(BF16), who_ref[...])
    merged = _sigmoid(g1_ref[...].astype(F32)) * y_ssd + _sigmoid(g2_ref[...].astype(F32)) * y_hg
    mix = _dot(merged.astype(BF16), wo_ref[...])
    out_ref[...] = x_ref[...] + mod_ref[0][:, 2 * d:3 * d] * mix


def _merge_call(dims, y_ctx, y_lat, o_ctx, o_lat, proj, x, mod_l, wso, who, wo):
    d, ds_, t = dims.d, dims.d_ssd, dims.t
    tm = min(512, dims.row_tile)
    nctx = dims.t_ctx // tm
    gcol = proj.shape[1] // d - 2

    def ctx_blk(i):
        return jnp.minimum(i, nctx - 1)

    def lat_blk(i):
        return jnp.maximum(i - nctx, 0)

    const = lambda i: (0, 0)
    return pl.pallas_call(
        functools.partial(_merge_kernel, dims, tm),
        grid=(t // tm,),
        in_specs=[pl.BlockSpec((tm, ds_), lambda i: (ctx_blk(i), 0)),
                  pl.BlockSpec((tm, ds_), lambda i: (lat_blk(i), 0)),
                  pl.BlockSpec((tm, d), lambda i: (ctx_blk(i), 0)),
                  pl.BlockSpec((tm, d), lambda i: (lat_blk(i), 0)),
                  pl.BlockSpec((tm, d), lambda i: (i, gcol)),
                  pl.BlockSpec((tm, d), lambda i: (i, gcol + 1)),
                  pl.BlockSpec((tm, d), lambda i: (i, 0)),
                  pl.BlockSpec((1, 1, 6 * d), lambda i: (_mod_row(i, tm, dims), 0, 0)),
                  pl.BlockSpec((ds_, d), const), pl.BlockSpec((d, d), const), pl.BlockSpec((d, d), const)],
        out_specs=pl.BlockSpec((tm, d), lambda i: (i, 0)),
        out_shape=jax.ShapeDtypeStruct((t, d), F32),
        compiler_params=_cparams(("arbitrary",), 2 * (2 * tm * ds_ + 6 * tm * d) * 4 + 2 * (ds_ + 2 * d) * d * 2 + (12 << 20)),
        name="merge_outproj",
    )(y_ctx, y_lat, o_ctx, o_lat, proj, proj, x, mod_l, wso, who, wo)


def _ffn_kernel(dims, final_norm, x_ref, mod_ref, nw_ref, w1_ref, w3_ref, w2_ref, *rest):
    d = dims.d
    fw_ref, out_ref = rest if final_norm else (None, rest[0])

    x = x_ref[...]
    y = x * lax.rsqrt(jnp.mean(x * x, axis=-1, keepdims=True) + RMS_EPS) * nw_ref[...]
    mod = mod_ref[0]
    hb = (y * (1.0 + mod[:, 4 * d:5 * d]) + mod[:, 3 * d:4 * d]).astype(BF16)
    acc = None
    for lo in range(0, dims.ffn, MXU_COLS):
        hi = min(lo + MXU_COLS, dims.ffn)
        act = _silu(_dot(hb, w1_ref[:, lo:hi])) * _dot(hb, w3_ref[:, lo:hi])
        part = _dot(act.astype(BF16), w2_ref[lo:hi, :])
        acc = part if acc is None else acc + part
    out = x + mod[:, 5 * d:6 * d] * acc
    if final_norm:
        out = out * lax.rsqrt(jnp.mean(out * out, axis=-1, keepdims=True) + RMS_EPS) * fw_ref[...]
    out_ref[...] = out


def _ffn_call(dims, x, mod_l, nw, w1, w3, w2, final_nw=None):
    d, t, f = dims.d, dims.t, dims.ffn
    tm = min(512, dims.row_tile)
    const = lambda i: (0, 0)
    resident = pl.Buffered(1)
    in_specs = [pl.BlockSpec((tm, d), lambda i: (i, 0)),
                pl.BlockSpec((1, 1, 6 * d), lambda i: (_mod_row(i, tm, dims), 0, 0)),
                pl.BlockSpec((1, d), const),
                pl.BlockSpec((d, f), const, pipeline_mode=resident),
                pl.BlockSpec((d, f), const, pipeline_mode=resident),
                pl.BlockSpec((f, d), const, pipeline_mode=resident)]
    args = [x, mod_l, nw, w1, w3, w2]
    if final_nw is not None:
        in_specs.append(pl.BlockSpec((1, d), const))
        args.append(final_nw)
    return pl.pallas_call(
        functools.partial(_ffn_kernel, dims, final_nw is not None),
        grid=(t // tm,),
        in_specs=in_specs,
        out_specs=pl.BlockSpec((tm, d), lambda i: (i, 0)),
        out_shape=jax.ShapeDtypeStruct((t, d), F32),
        compiler_params=_cparams(("arbitrary",), 4 * tm * d * 4 + 3 * d * f * 2 + 6 * tm * d * 4 + (12 << 20)),
        name="ffn",
    )(*args)


def _forward(dims, x_prompt, x_sample, c, state_ssm, state_hgrn, c_ctx, norm_mix_w, norm_ffn_w, ada_w, ada_b, w_in,
             ssd_conv_w, ssd_conv_b, ssd_dt_bias, ssd_a_log, ssd_d, ssd_norm_w, hgrn_lower_bounds, hgrn_norm_w,
             w_ssd_out, w_hgrn_out, w_out, ffn_w1, ffn_w3, ffn_w2, norm_final_w):
    d, ds_, h = dims.d, dims.d_ssd, dims.h_ssd
    assert dims.db + 1 <= MOD_ROWS and 2 * h <= LANE and dims.nl % HG_C == 0 and dims.dl % 512 == 0
    conv_dim = ds_ + 2 * dims.groups * SSD_N
    x = jnp.concatenate([x_prompt.reshape(dims.t_ctx, d), x_sample.reshape(dims.t_lat, d)], axis=0)
    cvec = jnp.zeros((MOD_ROWS, d), F32).at[0].set(c_ctx).at[1:1 + dims.db].set(c)
    mod = _mod_call(dims, cvec, ada_w, ada_b)
    lb_all = jnp.cumsum(jax.nn.softmax(hgrn_lower_bounds.astype(F32), axis=0), axis=0)
    lb_all = lb_all - lb_all[:1]
    dt_cols = slice(conv_dim + ds_, conv_dim + ds_ + 2 * h)

    s_ssm = s_hg = None
    for l in range(dims.depth):
        mod_l = mod[l].reshape(MOD_ROWS, 1, 6 * d)
        w_l = w_in[l]
        w_a = w_l[:, :dt_cols.start].astype(BF16)
        w_b = w_l[:, dt_cols.stop:].astype(BF16)
        wdt = jnp.pad(w_l[:, dt_cols], ((0, 0), (0, LANE - 2 * h)))
        dtb = jnp.zeros((LANE, 1), F32).at[:2 * h, 0].set(ssd_dt_bias[l].reshape(-1))
        alog = jnp.zeros((LANE, 1), F32).at[:2 * h, 0].set(ssd_a_log[l].reshape(-1))
        lb = jnp.zeros((8, d), F32).at[:2].set(lb_all[l])
        proj, lam, kk, dt_t, da_t = _inproj_call(dims, x, mod_l, norm_mix_w[l].reshape(1, d), w_a, w_b, ssd_conv_w[l],
                                                 ssd_conv_b[l].reshape(1, conv_dim), wdt, dtb, alog, lb)
        dskip = jnp.broadcast_to(jnp.repeat(ssd_d[l], SSD_P)[:, None], (ds_, SSD_Q))
        ssd_nw = jnp.broadcast_to(ssd_norm_w[l][:, None], (ds_, SSD_Q))
        y_ctx, s_ssm = _ssd_call(dims, proj, dt_t, da_t, dskip, ssd_nw, None, dims.nb, dims.nl, 0, l, s_ssm)
        y_lat, _ = _ssd_call(dims, proj, dt_t, da_t, dskip, ssd_nw,
                             state_ssm[:, l].reshape(dims.db, 2, ds_, SSD_N), dims.db, dims.dl, dims.t_ctx)
        hg_nw = hgrn_norm_w[l].reshape(1, d)
        o_ctx, s_hg = _hgrn_call(dims, proj, lam, kk, hg_nw, None, dims.nb, dims.nl, 0, l, s_hg)
        o_lat, _ = _hgrn_call(dims, proj, lam, kk, hg_nw, state_hgrn[:, l], dims.db, dims.dl, dims.t_ctx)
        x = _merge_call(dims, y_ctx, y_lat, o_ctx, o_lat, proj, x, mod_l,
                        w_ssd_out[l].astype(BF16), w_hgrn_out[l].astype(BF16), w_out[l].astype(BF16))
        x = _ffn_call(dims, x, mod_l, norm_ffn_w[l].reshape(1, d),
                      ffn_w1[l].astype(BF16), ffn_w3[l].astype(BF16), ffn_w2[l].astype(BF16),
                      norm_final_w.reshape(1, d) if l == dims.depth - 1 else None)
    y_prompt = x[:dims.t_ctx].reshape(dims.nb, dims.nl, d)
    y_sample = x[dims.t_ctx:].reshape(dims.db, dims.dl, d)
    return (y_prompt, y_sample, s_ssm.reshape(dims.nb, dims.depth, 2, h, SSD_P, SSD_N), s_hg)


def kernel(x_prompt, x_sample, c, state_ssm, state_hgrn, c_ctx, norm_mix_w, norm_ffn_w, ada_w, ada_b, w_in,
           ssd_conv_w, ssd_conv_b, ssd_dt_bias, ssd_a_log, ssd_d, ssd_norm_w, hgrn_lower_bounds, hgrn_norm_w,
           w_ssd_out, w_hgrn_out, w_out, ffn_w1, ffn_w3, ffn_w2, norm_final_w):
    dims = Dims(d=x_prompt.shape[2], nb=x_prompt.shape[0], nl=x_prompt.shape[1], db=x_sample.shape[0],
                dl=x_sample.shape[1], depth=w_in.shape[0], ffn=ffn_w1.shape[2])
    return _forward(dims, x_prompt, x_sample, c, state_ssm, state_hgrn, c_ctx, norm_mix_w, norm_ffn_w, ada_w, ada_b,
                    w_in, ssd_conv_w, ssd_conv_b, ssd_dt_bias, ssd_a_log, ssd_d, ssd_norm_w, hgrn_lower_bounds,
                    hgrn_norm_w, w_ssd_out, w_hgrn_out, w_out, ffn_w1, ffn_w3, ffn_w2, norm_final_w)
```

```python
import functools
from typing import NamedTuple

import jax
import jax.numpy as jnp
from jax import lax
from jax.experimental import pallas as pl
from jax.experimental.pallas import tpu as pltpu

F32 = jnp.float32
BF16 = jnp.bfloat16

LANE = 128
MXU_COLS = 256
VMEM_BYTES = 64 * 1024 * 1024
RMS_EPS = 1e-6
D_CONV = 5
GRID_W = 64
SSD_P = 64
SSD_N = 128
SSD_Q = 128
SSD_R = 4
HG_K = 128
HG_V = 128
HG_C = 128
HG_LEVELS = (64, 32, 16)
HG_DIAG = 16
HF_BLOCKS = 2
MOD_ROWS = 8


class Dims(NamedTuple):
    d: int
    nb: int
    nl: int
    db: int
    dl: int
    depth: int
    ffn: int

    @property
    def t_ctx(self):
        return self.nb * self.nl

    @property
    def t_lat(self):
        return self.db * self.dl

    @property
    def t(self):
        return self.t_ctx + self.t_lat

    @property
    def d_ssd(self):
        return 2 * self.d

    @property
    def h_ssd(self):
        return self.d_ssd // SSD_P

    @property
    def groups(self):
        return self.h_ssd // SSD_R

    @property
    def hh(self):
        return self.d // HG_K

    @property
    def row_tile(self):
        tm = min(1024, self.dl, self.t_ctx)
        assert self.dl % tm == 0 and self.t_ctx % tm == 0 and tm % self.nl == 0 and tm % GRID_W == 0
        return tm


def _cparams(semantics, vmem_bytes):
    return pltpu.CompilerParams(dimension_semantics=semantics, vmem_limit_bytes=min(vmem_bytes, VMEM_BYTES - (4 << 20)))


def _dot(a, b):
    return jnp.dot(a, b, preferred_element_type=F32)


def _dot_nt(a, b):
    return lax.dot_general(a, b, (((1,), (1,)), ((), ())), preferred_element_type=F32)


def _dot_tn(a, b):
    return lax.dot_general(a, b, (((0,), (0,)), ((), ())), preferred_element_type=F32)


def _split3(x):
    hi = x.astype(BF16)
    r = x - hi.astype(F32)
    mid = r.astype(BF16)
    lo = (r - mid.astype(F32)).astype(BF16)
    return hi, mid, lo


def _sum01_right(x, m01):
    hi, mid, lo = _split3(x)
    return _dot(hi, m01) + _dot(mid, m01) + _dot(lo, m01)


def _sigmoid(x):
    return 0.5 * jnp.tanh(0.5 * x) + 0.5


def _silu(x):
    return x * _sigmoid(x)


def _softplus(x):
    return jnp.maximum(x, 0.0) + jnp.log1p(jnp.exp(-jnp.abs(x)))


def _mod_row(i, tm, dims):
    nctx = dims.t_ctx // tm
    return jnp.where(i < nctx, 0, 1 + ((i - nctx) * tm) // dims.dl)


def _mod_kernel(c_ref, w_ref, b_ref, o_ref):
    s = _silu(c_ref[...])
    o_ref[0] = _dot(s.astype(BF16), w_ref[0].astype(BF16)) + b_ref[0]


def _mod_call(dims, cvec, ada_w, ada_b):
    d, n = dims.d, 6 * dims.d
    tn = n // 4
    return pl.pallas_call(
        _mod_kernel,
        grid=(dims.depth, n // tn),
        in_specs=[pl.BlockSpec((MOD_ROWS, d), lambda l, j: (0, 0)),
                  pl.BlockSpec((1, d, tn), lambda l, j: (l, 0, j)),
                  pl.BlockSpec((1, 1, tn), lambda l, j: (l, 0, j))],
        out_specs=pl.BlockSpec((1, MOD_ROWS, tn), lambda l, j: (l, 0, j)),
        out_shape=jax.ShapeDtypeStruct((dims.depth, MOD_ROWS, n), F32),
        compiler_params=_cparams(("arbitrary", "arbitrary"), 2 * d * tn * 4 + (8 << 20)),
        name="adaln_mod",
    )(cvec, ada_w, ada_b.reshape(dims.depth, 1, n))


def _inproj_kernel(dims, tm, x_ref, mod_ref, nw_ref, wa_ref, wb_ref, cw_ref, cb_ref, wdt_ref, dtb_ref, alog_ref, lb_ref,
                   proj_ref, lam_ref, kk_ref, dt_ref, da_ref, h_s):
    d = dims.d
    i = pl.program_id(0)
    j = pl.program_id(1)
    n_conv = (dims.d_ssd + 2 * dims.groups * SSD_N) // d
    n_a = n_conv + dims.d_ssd // d
    n_bf = pl.num_programs(1) - HF_BLOCKS

    @pl.when(j == 0)
    def _():
        x = x_ref[...]
        y = x * lax.rsqrt(jnp.mean(x * x, axis=-1, keepdims=True) + RMS_EPS) * nw_ref[...]
        mod = mod_ref[0]
        hb = (y * (1.0 + mod[:, d:2 * d]) + mod[:, 0:d]).astype(BF16)
        h_s[...] = hb
        wdt_t = wdt_ref[...].T.astype(BF16)
        dt = _softplus(_dot_nt(wdt_t, hb) + dtb_ref[...])
        da = dt * -jnp.exp(alog_ref[...])
        for k in range(tm // SSD_Q):
            dt_ref[k] = dt[:, k * SSD_Q:(k + 1) * SSD_Q]
            da_ref[k] = da[:, k * SSD_Q:(k + 1) * SSD_Q]

    @pl.when(j >= n_bf)
    def _():
        lam, kk = _hgrn_gates(_dot(h_s[...], wb_ref[...]), lb_ref[pl.ds(j - n_bf, 1), :])
        lam_ref[...] = lam
        kk_ref[...] = kk.astype(BF16)

    @pl.when((j >= n_conv) & (j < n_a))
    def _():
        proj_ref[...] = _dot(h_s[...], wa_ref[...]).astype(BF16)

    @pl.when((j >= n_a) & (j < n_bf))
    def _():
        proj_ref[...] = _dot(h_s[...], wb_ref[...]).astype(BF16)

    def conv_silu(seg):
        ns, half, sub = tm // seg, D_CONV // 2, 8
        n8 = ns * sub
        r8 = lax.broadcasted_iota(jnp.int32, (n8, LANE), 0) & (sub - 1)
        res = None
        for sl in range(d // LANE):
            cols = slice(sl * LANE, (sl + 1) * LANE)
            if sl % (MXU_COLS // LANE) == 0:
                res = _dot(h_s[...], wa_ref[:, sl * LANE:sl * LANE + MXU_COLS])
            u = res[:, (sl * LANE) % MXU_COLS:(sl * LANE) % MXU_COLS + LANE]
            w = [cw_ref[k:k + 1, cols] for k in range(D_CONV)]
            acc = cb_ref[:, cols] + w[half] * u
            for k in range(D_CONV):
                if k != half:
                    acc = acc + w[k] * pltpu.roll(u, (half - k) % tm, 0)
            u3 = u.reshape(ns, seg, LANE)
            a3 = acc.reshape(ns, seg, LANE)
            edge_lo = u3[:, 0:sub, :].reshape(n8, LANE)
            edge_hi = u3[:, seg - sub:seg, :].reshape(n8, LANE)
            leak_lo = (w[0] * jnp.where(r8 < 2, pltpu.roll(edge_hi, 2, 0), 0.0)
                       + w[1] * jnp.where(r8 < 1, pltpu.roll(edge_hi, 1, 0), 0.0))
            leak_hi = (w[4] * jnp.where(r8 >= sub - 2, pltpu.roll(edge_lo, n8 - 2, 0), 0.0)
                       + w[3] * jnp.where(r8 >= sub - 1, pltpu.roll(edge_lo, n8 - 1, 0), 0.0))
            a3 = jnp.concatenate([a3[:, 0:sub, :] - leak_lo.reshape(ns, sub, LANE),
                                  a3[:, sub:seg - sub, :],
                                  a3[:, seg - sub:seg, :] - leak_hi.reshape(ns, sub, LANE)], axis=1)
            proj_ref[:, cols] = _silu(a3.reshape(tm, LANE)).astype(BF16)

    is_ctx = i < dims.t_ctx // tm

    @pl.when((j < n_conv) & is_ctx)
    def _():
        conv_silu(dims.nl)

    @pl.when((j < n_conv) & jnp.logical_not(is_ctx))
    def _():
        conv_silu(GRID_W)


def _inproj_call(dims, x, mod_l, nw, wa, wb, cw, cb, wdt, dtb, alog, lb):
    d, t, tm = dims.d, dims.t, dims.row_tile
    n_conv = (dims.d_ssd + 2 * dims.groups * SSD_N) // d
    n_a = wa.shape[1] // d
    ncol = n_a + wb.shape[1] // d
    nchunk = t // SSD_Q
    n_bf = ncol - HF_BLOCKS

    def wb_block(j):
        return jnp.where(j <= n_a, 0, jnp.where(j < n_bf, j - n_a + HF_BLOCKS, j - n_bf + 1))

    out_shapes = (jax.ShapeDtypeStruct((t, n_bf * d), BF16),
                  jax.ShapeDtypeStruct((t, HF_BLOCKS * d), F32),
                  jax.ShapeDtypeStruct((t, HF_BLOCKS * d), BF16),
                  jax.ShapeDtypeStruct((nchunk, LANE, SSD_Q), F32),
                  jax.ShapeDtypeStruct((nchunk, LANE, SSD_Q), F32))
    kc = tm // SSD_Q
    gate_spec = pl.BlockSpec((tm, d), lambda i, j: (i, jnp.maximum(j - n_bf, 0)))
    return pl.pallas_call(
        functools.partial(_inproj_kernel, dims, tm),
        grid=(t // tm, ncol),
        in_specs=[pl.BlockSpec((tm, d), lambda i, j: (i, 0)),
                  pl.BlockSpec((1, 1, 6 * d), lambda i, j: (_mod_row(i, tm, dims), 0, 0)),
                  pl.BlockSpec((1, d), lambda i, j: (0, 0)),
                  pl.BlockSpec((d, d), lambda i, j: (0, jnp.minimum(j, n_a - 1))),
                  pl.BlockSpec((d, d), lambda i, j: (0, wb_block(j))),
                  pl.BlockSpec((D_CONV, d), lambda i, j: (0, jnp.minimum(j, n_conv - 1))),
                  pl.BlockSpec((1, d), lambda i, j: (0, jnp.minimum(j, n_conv - 1))),
                  pl.BlockSpec((d, LANE), lambda i, j: (0, 0)),
                  pl.BlockSpec((LANE, 1), lambda i, j: (0, 0)),
                  pl.BlockSpec((LANE, 1), lambda i, j: (0, 0)),
                  pl.BlockSpec((8, d), lambda i, j: (0, 0))],
        out_specs=(pl.BlockSpec((tm, d), lambda i, j: (i, jnp.minimum(j, n_bf - 1))),
                   gate_spec, gate_spec,
                   pl.BlockSpec((kc, LANE, SSD_Q), lambda i, j: (i, 0, 0)),
                   pl.BlockSpec((kc, LANE, SSD_Q), lambda i, j: (i, 0, 0))),
        out_shape=out_shapes,
        scratch_shapes=[pltpu.VMEM((tm, d), BF16)],
        compiler_params=_cparams(("arbitrary", "arbitrary"), 8 * tm * d * 4 + 4 * d * d * 2 + tm * d * 2 + (8 << 20)),
        name="inproj",
    )(x, mod_l, nw, wa, wb, cw, cb, wdt, dtb, alog, lb)


def _ssd_kernel(dims, nchunk, has_init, has_prev, *refs):
    x_ref, b_ref, c_ref, z_ref, dt_ref, da_ref, dsk_ref, nw_ref = refs[:8]
    s0_ref = refs[8] if has_init else None
    (y_ref, sfin_ref, st, yf_s, xt_s, yb_s, bg_s, cg_s, cs_s, ecs_s, wend_s, dec_s,
     dtr_s) = refs[8 + has_init + has_prev:]
    h, g_n = dims.h_ssd, dims.groups
    q = SSD_Q
    s = pl.program_id(1)
    is_bwd = s >= nchunk
    ck = jnp.where(is_bwd, 2 * nchunk - 1 - s, s)
    first = (s == 0) | (s == nchunk)
    last = (s == nchunk - 1) | (s == 2 * nchunk - 1)

    @pl.when(first)
    def _():
        st[...] = s0_ref[0, 0] if has_init else jnp.zeros(st.shape, F32)

    row0 = pl.multiple_of(jnp.where(is_bwd, h, 0), 8)
    dtr = dt_ref[0, pl.ds(row0, h), :]
    dar = da_ref[0, pl.ds(row0, h), :]
    jj = lax.broadcasted_iota(jnp.int32, (q, q), 0)
    ii = lax.broadcasted_iota(jnp.int32, (q, q), 1)
    seen = jnp.where(is_bwd, jj - ii, ii - jj) >= 0
    cs = _sum01_right(dar, seen.astype(BF16))
    tot = _sum01_right(dar, jnp.ones((q, q), BF16))
    cs_s[...] = cs
    ecs_s[...] = jnp.exp(cs)
    wend_s[...] = jnp.exp(tot - cs) * dtr
    dec_s[...] = jnp.exp(tot)
    dtr_s[...] = dtr
    xt_s[...] = x_ref[...].astype(F32).T
    for g in range(g_n):
        bg_s[g] = b_ref[:, g * SSD_N:(g + 1) * SSD_N]
        cg_s[g] = c_ref[:, g * SSD_N:(g + 1) * SSD_N]

    gp = SSD_R * SSD_P

    def group_body(g, carry):
        bg = bg_s[g]
        cg = cg_s[g]
        cbt = _dot_nt(bg, cg)
        r0 = pl.multiple_of(g * gp, gp)
        sg = st[pl.ds(r0, gp), :]
        sc = _dot_nt(sg.astype(BF16), cg)
        xg = xt_s[pl.ds(r0, gp), :]
        xw, decs = [], []
        for r in range(SSD_R):
            hd = g * SSD_R + r
            xh = xg[r * SSD_P:(r + 1) * SSD_P]
            cs_i = cs_s[pl.ds(hd, 1), :]
            cs_j = jnp.broadcast_to(cs_i, (q, q)).T
            decay = jnp.exp(jnp.where(seen, cs_i - cs_j, -jnp.inf))
            mt = (cbt * decay).astype(BF16)
            xdt = (xh * dtr_s[pl.ds(hd, 1), :]).astype(BF16)
            yh = _dot(xdt, mt) + sc[r * SSD_P:(r + 1) * SSD_P] * ecs_s[pl.ds(hd, 1), :]
            yb_s[pl.ds(pl.multiple_of(r0 + r * SSD_P, SSD_P), SSD_P), :] = yh
            xw.append((xh * wend_s[pl.ds(hd, 1), :]).astype(BF16))
            decs.append(jnp.broadcast_to(dec_s[pl.ds(hd, 1), :], (SSD_P, SSD_N)))
        upd = _dot(jnp.concatenate(xw, axis=0), bg)
        st[pl.ds(r0, gp), :] = sg * jnp.concatenate(decs, axis=0) + upd
        return carry

    lax.fori_loop(0, g_n, group_body, 0, unroll=True)

    @pl.when(jnp.logical_not(is_bwd))
    def _():
        yf_s[ck] = yb_s[...]

    @pl.when(is_bwd)
    def _():
        yt = yb_s[...] + yf_s[ck] + dsk_ref[...] * xt_s[...]
        yt = yt * _silu(z_ref[...].astype(F32).T)
        y3 = yt.reshape(g_n, gp, q)
        yn = y3 * lax.rsqrt(jnp.mean(y3 * y3, axis=1, keepdims=True) + RMS_EPS)
        y_ref[...] = (yn.reshape(dims.d_ssd, q) * nw_ref[...]).T.astype(BF16)

    @pl.when(last)
    def _():
        sfin_ref[0, 0, 0] = st[...]


def _ssd_call(dims, proj, dt_t, da_t, dskip, nw, s0, n_seq, seq_len, tok_off, layer=None, fin_prev=None):
    d, ds_ = dims.d, dims.d_ssd
    n_slab, slab = (1, 0) if layer is None else (dims.depth, layer)
    q = SSD_Q
    nchunk = seq_len // q
    off = tok_off // q
    has_init = s0 is not None

    def ck(s):
        return jnp.where(s < nchunk, s, 2 * nchunk - 1 - s)

    def tok(b, s):
        return off + b * nchunk + ck(s)

    def ytok(b, s):
        return b * nchunk + jnp.where(s < nchunk, nchunk - 1, 2 * nchunk - 1 - s)

    in_specs = [pl.BlockSpec((q, ds_), lambda b, s: (tok(b, s), 0)),
                pl.BlockSpec((q, d), lambda b, s: (tok(b, s), ds_ // d)),
                pl.BlockSpec((q, d), lambda b, s: (tok(b, s), ds_ // d + 1)),
                pl.BlockSpec((q, ds_), lambda b, s: (off + ytok(b, s), (ds_ + 2 * d) // ds_)),
                pl.BlockSpec((1, LANE, q), lambda b, s: (tok(b, s), 0, 0)),
                pl.BlockSpec((1, LANE, q), lambda b, s: (tok(b, s), 0, 0)),
                pl.BlockSpec((ds_, q), lambda b, s: (0, 0)),
                pl.BlockSpec((ds_, q), lambda b, s: (0, 0))]
    args = [proj, proj, proj, proj, dt_t, da_t, dskip, nw]
    if has_init:
        in_specs.append(pl.BlockSpec((1, 1, ds_, SSD_N), lambda b, s: (b, s // nchunk, 0, 0)))
        args.append(s0)
    aliases = {}
    if fin_prev is not None:
        in_specs.append(pl.BlockSpec(memory_space=pl.ANY))
        aliases = {len(args): 1}
        args.append(fin_prev)
    h = dims.h_ssd
    scratch = [pltpu.VMEM((ds_, SSD_N), F32),
               pltpu.VMEM((nchunk, ds_, q), F32),
               pltpu.VMEM((ds_, q), F32),
               pltpu.VMEM((ds_, q), F32),
               pltpu.VMEM((dims.groups, q, SSD_N), BF16),
               pltpu.VMEM((dims.groups, q, SSD_N), BF16),
               pltpu.VMEM((h, q), F32), pltpu.VMEM((h, q), F32), pltpu.VMEM((h, q), F32),
               pltpu.VMEM((h, q), F32), pltpu.VMEM((h, q), F32)]
    vmem = (2 * (3 * q * ds_ + 2 * q * d + 2 * LANE * q + 2 * ds_ * q + 2 * ds_ * SSD_N) * 4
            + (nchunk + 3) * ds_ * q * 4 + (12 << 20))
    return pl.pallas_call(
        functools.partial(_ssd_kernel, dims, nchunk, has_init, fin_prev is not None),
        grid=(n_seq, 2 * nchunk),
        in_specs=in_specs,
        out_specs=(pl.BlockSpec((q, ds_), lambda b, s: (ytok(b, s), 0)),
                   pl.BlockSpec((1, 1, 1, ds_, SSD_N), lambda b, s: (b, slab, s // nchunk, 0, 0))),
        out_shape=(jax.ShapeDtypeStruct((n_seq * seq_len, ds_), BF16),
                   jax.ShapeDtypeStruct((n_seq, n_slab, 2, ds_, SSD_N), F32)),
        input_output_aliases=aliases,
        scratch_shapes=scratch,
        compiler_params=_cparams(("arbitrary", "arbitrary"), vmem),
        name="ssd_scan",
    )(*args)


def _hgrn_gates(u, lb):
    t = jnp.exp(-jnp.abs(u))
    pos = u >= 0.0
    den = 1.0 + t
    num = jnp.where(pos, 1.0 + lb * t, lb + t)
    log_num = jnp.where(pos | (jnp.broadcast_to(lb, u.shape) > 0.0), jnp.log(num), u)
    return log_num - jnp.log(den), (1.0 - lb) * jnp.where(pos, t, 1.0) / den


def _hg_dir_operands(lam, kk, qq, blockmat, fwd):
    c, w = lam.shape
    bs, nb = HG_DIAG, lam.shape[0] // HG_DIAG
    lam_hi = lam.astype(BF16)
    lam_lo = (lam - lam_hi.astype(F32)).astype(BF16)
    p = _dot(blockmat, lam_hi) + _dot(blockmat, lam_lo)
    edge, mid = (bs - 1, bs // 2 - 1) if fwd else (0, bs // 2)
    tot = [p[b * bs + edge:b * bs + edge + 1] for b in range(nb)]
    p_mid = [p[b * bs + mid:b * bs + mid + 1] for b in range(nb)]

    def spread(rows):
        return jnp.concatenate([jnp.broadcast_to(r, (bs, w)) for r in rows], axis=0)

    s = spread(tot) - p
    qp = qq * jnp.exp(p)
    ks = kk * jnp.exp(s)
    zero = jnp.zeros((1, w), F32)

    def run_sums(lo_b, hi_b):
        before, acc = {}, zero
        order = range(lo_b, hi_b) if fwd else range(hi_b - 1, lo_b - 1, -1)
        for b in order:
            before[b] = acc
            acc = acc + tot[b]
        after, acc = {}, zero
        for b in reversed(order):
            after[b] = acc
            acc = acc + tot[b]
        return before, after, acc

    def scaled(x, logs):
        zblk = jnp.zeros((bs, w), BF16)
        parts = []
        for b in range(nb):
            xb = x[b * bs:(b + 1) * bs]
            if logs[b] is None:
                parts.append(zblk)
            elif logs[b] is zero:
                parts.append(xb.astype(BF16))
            else:
                parts.append((xb * jnp.exp(logs[b])).astype(BF16))
        return jnp.concatenate(parts, axis=0)

    level_q, level_k = [], []
    for hf in HG_LEVELS:
        nh = hf // bs
        ql, kl = [None] * nb, [None] * nb
        for par in range(nb // (2 * nh)):
            halves = (par * 2 * nh, par * 2 * nh + nh), (par * 2 * nh + nh, (par + 1) * 2 * nh)
            key_half, query_half = halves if fwd else halves[::-1]
            _, after, _ = run_sums(*key_half)
            before, _, _ = run_sums(*query_half)
            for b in range(*key_half):
                kl[b] = after[b]
            for b in range(*query_half):
                ql[b] = before[b]
        level_q.append(scaled(qp, ql))
        level_k.append(scaled(ks, kl))
    s_mid = [t - pm for t, pm in zip(tot, p_mid)]
    diag_q = (qp * spread([jnp.exp(-pm) for pm in p_mid])).astype(BF16)
    diag_k = (ks * spread([jnp.exp(-sm) for sm in s_mid])).astype(BF16)
    before, after, total = run_sums(0, nb)
    state_q = scaled(qp, [before[b] for b in range(nb)])
    state_k = scaled(ks, [after[b] for b in range(nb)])
    return level_q, level_k, diag_q, diag_k, state_q, state_k, jnp.exp(total)


def _hgrn_kernel(dims, nblk, tb, has_init, has_prev, *refs):
    q_ref, lamf_ref, lamb_ref, kf_ref, kb_ref, v_ref, g_ref, nw_ref = refs[:8]
    s0_ref = refs[8] if has_init else None
    o_ref, sfin_ref, st, o_s, qb_s, kb_s, decb_s = refs[8 + has_init + has_prev:]
    hh = dims.hh
    c = HG_C
    nck = tb // c
    single = nblk == 1
    s = pl.program_id(1)
    is_bwd = s >= nblk
    blk = 0 if single else jnp.where(is_bwd, 2 * nblk - 1 - s, s)

    def init_state(direction):
        for hd in range(hh):
            st[hd] = s0_ref[0, direction, hd].T if has_init else jnp.zeros((HG_V, HG_K), F32)

    def save_state(direction):
        for hd in range(hh):
            sfin_ref[0, 0, direction, hd] = st[hd].T

    ti = lax.broadcasted_iota(jnp.int32, (c, c), 0)
    tj = lax.broadcasted_iota(jnp.int32, (c, c), 1)
    same_diag = (ti // HG_DIAG) == (tj // HG_DIAG)
    diag_f = same_diag & (tj <= ti)
    diag_b = same_diag & (tj >= ti)
    mask_df, mask_db = diag_f.astype(F32), diag_b.astype(F32)
    parent = {hf: ((ti // (2 * hf)) == (tj // (2 * hf))).astype(F32) for hf in HG_LEVELS if 2 * hf < c}

    def heads(x):
        return [x[:, hd * HG_K:(hd + 1) * HG_K] for hd in range(hh)]

    def state_step(hd, qs_h, ke_h, v_h, dec_h):
        sh = st[hd]
        off = _dot_nt(qs_h, sh.astype(BF16))
        st[hd] = sh * dec_h + _dot_tn(v_h, ke_h)
        return off

    def fwd_chunk(idx, carry):
        r0 = pl.multiple_of(idx * c, c)
        rows = pl.ds(r0, c)
        g0 = pl.multiple_of(blk * tb + r0, c)
        qq = _silu(q_ref[rows, :].astype(F32))
        vh = heads(v_ref[rows, :])
        lq_f, lk_f, dq_f, dk_f, sq_f, sk_f, dec_f = _hg_dir_operands(
            lamf_ref[rows, :], kf_ref[rows, :].astype(F32), qq, diag_f.astype(BF16), True)
        lq_b, lk_b, dq_b, dk_b, sq_b, sk_b, dec_b = _hg_dir_operands(
            lamb_ref[rows, :], kb_ref[rows, :].astype(F32), qq, diag_b.astype(BF16), False)
        qb_s[pl.ds(g0, c), :] = sq_b
        kb_s[pl.ds(g0, c), :] = sk_b
        decb_s[g0 // c] = dec_b
        lq_f, lk_f, lq_b, lk_b = ([heads(x) for x in xs] for xs in (lq_f, lk_f, lq_b, lk_b))
        dq_f, dk_f, dq_b, dk_b, sq_f, sk_f, dec_f = (heads(x) for x in (dq_f, dk_f, dq_b, dk_b, sq_f, sk_f, dec_f))
        outs = []
        for hd in range(hh):
            att = _dot_nt(dq_f[hd], dk_f[hd]) * mask_df + _dot_nt(dq_b[hd], dk_b[hd]) * mask_db
            for li, hf in enumerate(HG_LEVELS):
                p = _dot_nt(jnp.concatenate([lq_f[li][hd], lq_b[li][hd]], axis=1),
                            jnp.concatenate([lk_f[li][hd], lk_b[li][hd]], axis=1))
                att = att + (p * parent[hf] if hf in parent else p)
            o_h = _dot(att.astype(BF16), vh[hd])
            outs.append(o_h + state_step(hd, sq_f[hd], sk_f[hd], vh[hd], dec_f[hd]))
        o_s[pl.ds(g0, c), :] = jnp.concatenate(outs, axis=1)
        return carry

    def bwd_chunk(idx, carry):
        r0 = pl.multiple_of((nck - 1 - idx) * c, c)
        rows = pl.ds(r0, c)
        g0 = pl.multiple_of(blk * tb + r0, c)
        vh = heads(v_ref[rows, :])
        qs = heads(qb_s[pl.ds(g0, c), :])
        ke = heads(kb_s[pl.ds(g0, c), :])
        dec = heads(decb_s[g0 // c])
        o_prev = heads(o_s[pl.ds(g0, c), :])
        gate = heads(_silu(g_ref[rows, :].astype(F32)))
        nw = heads(nw_ref[...])
        outs = []
        for hd in range(hh):
            o_h = o_prev[hd] + state_step(hd, qs[hd], ke[hd], vh[hd], dec[hd])
            o_h = o_h * lax.rsqrt(jnp.mean(o_h * o_h, axis=-1, keepdims=True) + RMS_EPS) * nw[hd]
            outs.append(o_h * gate[hd])
        o_ref[rows, :] = jnp.concatenate(outs, axis=1).astype(BF16)
        return carry

    if single:
        init_state(0)
        lax.fori_loop(0, nck, fwd_chunk, 0)
        save_state(0)
        init_state(1)
        lax.fori_loop(0, nck, bwd_chunk, 0)
        save_state(1)
        return

    @pl.when(s == 0)
    def _():
        init_state(0)

    @pl.when(s == nblk)
    def _():
        init_state(1)

    @pl.when(jnp.logical_not(is_bwd))
    def _():
        lax.fori_loop(0, nck, fwd_chunk, 0)

    @pl.when(is_bwd)
    def _():
        lax.fori_loop(0, nck, bwd_chunk, 0)

    @pl.when(s == nblk - 1)
    def _():
        save_state(0)

    @pl.when(s == 2 * nblk - 1)
    def _():
        save_state(1)


def _hgrn_call(dims, proj, lam, kk, nw, s0, n_seq, seq_len, tok_off, layer=None, fin_prev=None):
    d, hh = dims.d, dims.hh
    n_slab, slab = (1, 0) if layer is None else (dims.depth, layer)
    tb = min(512, seq_len)
    nblk = seq_len // tb
    off = tok_off // tb
    has_init = s0 is not None
    col0 = (dims.d_ssd + 2 * d + dims.d_ssd) // d

    def tok(b, s):
        return off + b * nblk + jnp.where(s < nblk, s, 2 * nblk - 1 - s)

    def tok_pass1(b, s):
        return off + b * nblk + jnp.minimum(s, nblk - 1)

    def otok(b, s):
        return b * nblk + jnp.where(s < nblk, nblk - 1, 2 * nblk - 1 - s)

    def tok_pass2(b, s):
        return off + otok(b, s)

    srcs = ((proj, col0, tok_pass1), (lam, 0, tok_pass1), (lam, 1, tok_pass1), (kk, 0, tok_pass1), (kk, 1, tok_pass1),
            (proj, col0 + 1, tok), (proj, col0 + 2, tok_pass2))
    in_specs = [pl.BlockSpec((tb, d), functools.partial(lambda cc, tk, b, s: (tk(b, s), cc), col, tk))
                for _, col, tk in srcs]
    in_specs += [pl.BlockSpec((1, d), lambda b, s: (0, 0))]
    args = [a for a, _, _ in srcs] + [nw]
    if has_init:
        in_specs.append(pl.BlockSpec((1, 2, hh, HG_K, HG_V), lambda b, s: (b, 0, 0, 0, 0)))
        args.append(s0)
    aliases = {}
    if fin_prev is not None:
        in_specs.append(pl.BlockSpec(memory_space=pl.ANY))
        aliases = {len(args): 1}
        args.append(fin_prev)
    vmem = 2 * 7 * tb * d * 4 + 2 * seq_len * d * 4 + 4 * hh * HG_K * HG_V * 4 + (16 << 20)
    return pl.pallas_call(
        functools.partial(_hgrn_kernel, dims, nblk, tb, has_init, fin_prev is not None),
        grid=(n_seq, 1 if nblk == 1 else 2 * nblk),
        in_specs=in_specs,
        out_specs=(pl.BlockSpec((tb, d), lambda b, s: (otok(b, s), 0)),
                   pl.BlockSpec((1, 1, 2, hh, HG_K, HG_V), lambda b, s: (b, slab, 0, 0, 0, 0))),
        out_shape=(jax.ShapeDtypeStruct((n_seq * seq_len, d), BF16),
                   jax.ShapeDtypeStruct((n_seq, n_slab, 2, hh, HG_K, HG_V), F32)),
        input_output_aliases=aliases,
        scratch_shapes=[pltpu.VMEM((hh, HG_V, HG_K), F32),
                        pltpu.VMEM((seq_len, d), F32),
                        pltpu.VMEM((seq_len, d), BF16),
                        pltpu.VMEM((seq_len, d), BF16),
                        pltpu.VMEM((seq_len // HG_C, 1, d), F32)],
        compiler_params=_cparams(("arbitrary", "arbitrary"), vmem),
        name="hgrn_scan",
    )(*args)


def _merge_kernel(dims, tm, yc_ref, yl_ref, oc_ref, ol_ref, g1_ref, g2_ref, x_ref, mod_ref,
                  wso_ref, who_ref, wo_ref, out_ref):
    d = dims.d
    is_ctx = pl.program_id(0) < dims.t_ctx // tm
    y = jnp.where(is_ctx, yc_ref[...], yl_ref[...])
    o = jnp.where(is_ctx, oc_ref[...], ol_ref[...])
    y_ssd = _dot(y, wso_ref[...])
    y_hg = _dot(o, who_ref[...])
    merged = _sigmoid(g1_ref[...].astype(F32)) * y_ssd + _sigmoid(g2_ref[...].astype(F32)) * y_hg
    mix = _dot(merged.astype(BF16), wo_ref[...])
    out_ref[...] = x_ref[...] + mod_ref[0][:, 2 * d:3 * d] * mix


def _merge_call(dims, y_ctx, y_lat, o_ctx, o_lat, proj, x, mod_l, wso, who, wo):
    d, ds_, t = dims.d, dims.d_ssd, dims.t
    tm = min(512, dims.row_tile)
    nctx = dims.t_ctx // tm
    gcol = proj.shape[1] // d - 2

    def ctx_blk(i):
        return jnp.minimum(i, nctx - 1)

    def lat_blk(i):
        return jnp.maximum(i - nctx, 0)

    const = lambda i: (0, 0)
    return pl.pallas_call(
        functools.partial(_merge_kernel, dims, tm),
        grid=(t // tm,),
        in_specs=[pl.BlockSpec((tm, ds_), lambda i: (ctx_blk(i), 0)),
                  pl.BlockSpec((tm, ds_), lambda i: (lat_blk(i), 0)),
                  pl.BlockSpec((tm, d), lambda i: (ctx_blk(i), 0)),
                  pl.BlockSpec((tm, d), lambda i: (lat_blk(i), 0)),
                  pl.BlockSpec((tm, d), lambda i: (i, gcol)),
                  pl.BlockSpec((tm, d), lambda i: (i, gcol + 1)),
                  pl.BlockSpec((tm, d), lambda i: (i, 0)),
                  pl.BlockSpec((1, 1, 6 * d), lambda i: (_mod_row(i, tm, dims), 0, 0)),
                  pl.BlockSpec((ds_, d), const), pl.BlockSpec((d, d), const), pl.BlockSpec((d, d), const)],
        out_specs=pl.BlockSpec((tm, d), lambda i: (i, 0)),
        out_shape=jax.ShapeDtypeStruct((t, d), F32),
        compiler_params=_cparams(("arbitrary",), 2 * (2 * tm * ds_ + 6 * tm * d) * 4 + 2 * (ds_ + 2 * d) * d * 2 + (12 << 20)),
        name="merge_outproj",
    )(y_ctx, y_lat, o_ctx, o_lat, proj, proj, x, mod_l, wso, who, wo)


def _ffn_kernel(dims, tm, final_norm, x_ref, mod_ref, nw_ref, w1_ref, w3_ref, w2_ref, *rest):
    d = dims.d
    if final_norm:
        fw_ref, out_ctx_ref, out_lat_ref = rest
    else:
        (out_ref,) = rest

    x = x_ref[...]
    y = x * lax.rsqrt(jnp.mean(x * x, axis=-1, keepdims=True) + RMS_EPS) * nw_ref[...]
    mod = mod_ref[0]
    hb = (y * (1.0 + mod[:, 4 * d:5 * d]) + mod[:, 3 * d:4 * d]).astype(BF16)
    acc = None
    for lo in range(0, dims.ffn, MXU_COLS):
        hi = min(lo + MXU_COLS, dims.ffn)
        act = _silu(_dot(hb, w1_ref[:, lo:hi])) * _dot(hb, w3_ref[:, lo:hi])
        part = _dot(act.astype(BF16), w2_ref[lo:hi, :])
        acc = part if acc is None else acc + part
    out = x + mod[:, 5 * d:6 * d] * acc
    if not final_norm:
        out_ref[...] = out
        return
    out = out * lax.rsqrt(jnp.mean(out * out, axis=-1, keepdims=True) + RMS_EPS) * fw_ref[...]
    is_ctx = pl.program_id(0) < dims.t_ctx // tm

    @pl.when(is_ctx)
    def _():
        out_ctx_ref[...] = out

    @pl.when(jnp.logical_not(is_ctx))
    def _():
        out_lat_ref[...] = out


def _ffn_call(dims, x, mod_l, nw, w1, w3, w2, final_nw=None):
    d, t, f = dims.d, dims.t, dims.ffn
    tm = min(512, dims.row_tile)
    nctx = dims.t_ctx // tm
    const = lambda i: (0, 0)
    resident = pl.Buffered(1)
    in_specs = [pl.BlockSpec((tm, d), lambda i: (i, 0)),
                pl.BlockSpec((1, 1, 6 * d), lambda i: (_mod_row(i, tm, dims), 0, 0)),
                pl.BlockSpec((1, d), const),
                pl.BlockSpec((d, f), const, pipeline_mode=resident),
                pl.BlockSpec((d, f), const, pipeline_mode=resident),
                pl.BlockSpec((f, d), const, pipeline_mode=resident)]
    args = [x, mod_l, nw, w1, w3, w2]
    out_specs = pl.BlockSpec((tm, d), lambda i: (i, 0))
    out_shape = jax.ShapeDtypeStruct((t, d), F32)
    if final_nw is not None:
        in_specs.append(pl.BlockSpec((1, d), const))
        args.append(final_nw)
        out_specs = (pl.BlockSpec((tm, d), lambda i: (jnp.minimum(i, nctx - 1), 0)),
                     pl.BlockSpec((tm, d), lambda i: (jnp.maximum(i - nctx, 0), 0)))
        out_shape = (jax.ShapeDtypeStruct((dims.t_ctx, d), F32), jax.ShapeDtypeStruct((dims.t_lat, d), F32))
    return pl.pallas_call(
        functools.partial(_ffn_kernel, dims, tm, final_nw is not None),
        grid=(t // tm,),
        in_specs=in_specs,
        out_specs=out_specs,
        out_shape=out_shape,
        compiler_params=_cparams(("arbitrary",), 4 * tm * d * 4 + 3 * d * f * 2 + 6 * tm * d * 4 + (12 << 20)),
        name="ffn",
    )(*args)


def _forward(dims, x_prompt, x_sample, c, state_ssm, state_hgrn, c_ctx, norm_mix_w, norm_ffn_w, ada_w, ada_b, w_in,
             ssd_conv_w, ssd_conv_b, ssd_dt_bias, ssd_a_log, ssd_d, ssd_norm_w, hgrn_lower_bounds, hgrn_norm_w,
             w_ssd_out, w_hgrn_out, w_out, ffn_w1, ffn_w3, ffn_w2, norm_final_w):
    d, ds_, h = dims.d, dims.d_ssd, dims.h_ssd
    assert dims.db + 1 <= MOD_ROWS and 2 * h <= LANE and dims.nl % HG_C == 0 and dims.dl % 512 == 0
    conv_dim = ds_ + 2 * dims.groups * SSD_N
    x = jnp.concatenate([x_prompt.reshape(dims.t_ctx, d), x_sample.reshape(dims.t_lat, d)], axis=0)
    cvec = jnp.zeros((MOD_ROWS, d), F32).at[0].set(c_ctx).at[1:1 + dims.db].set(c)
    mod = _mod_call(dims, cvec, ada_w, ada_b)
    lb_all = jnp.cumsum(jax.nn.softmax(hgrn_lower_bounds.astype(F32), axis=0), axis=0)
    lb_all = lb_all - lb_all[:1]
    dt_cols = slice(conv_dim + ds_, conv_dim + ds_ + 2 * h)

    s_ssm = s_hg = None
    for l in range(dims.depth):
        mod_l = mod[l].reshape(MOD_ROWS, 1, 6 * d)
        w_l = w_in[l]
        w_a = w_l[:, :dt_cols.start].astype(BF16)
        w_b = w_l[:, dt_cols.stop:].astype(BF16)
        wdt = jnp.pad(w_l[:, dt_cols], ((0, 0), (0, LANE - 2 * h)))
        dtb = jnp.zeros((LANE, 1), F32).at[:2 * h, 0].set(ssd_dt_bias[l].reshape(-1))
        alog = jnp.zeros((LANE, 1), F32).at[:2 * h, 0].set(ssd_a_log[l].reshape(-1))
        lb = jnp.zeros((8, d), F32).at[:2].set(lb_all[l])
        proj, lam, kk, dt_t, da_t = _inproj_call(dims, x, mod_l, norm_mix_w[l].reshape(1, d), w_a, w_b, ssd_conv_w[l],
                                                 ssd_conv_b[l].reshape(1, conv_dim), wdt, dtb, alog, lb)
        dskip = jnp.broadcast_to(jnp.repeat(ssd_d[l], SSD_P)[:, None], (ds_, SSD_Q))
        ssd_nw = jnp.broadcast_to(ssd_norm_w[l][:, None], (ds_, SSD_Q))
        y_ctx, s_ssm = _ssd_call(dims, proj, dt_t, da_t, dskip, ssd_nw, None, dims.nb, dims.nl, 0, l, s_ssm)
        y_lat, _ = _ssd_call(dims, proj, dt_t, da_t, dskip, ssd_nw,
                             state_ssm[:, l].reshape(dims.db, 2, ds_, SSD_N), dims.db, dims.dl, dims.t_ctx)
        hg_nw = hgrn_norm_w[l].reshape(1, d)
        o_ctx, s_hg = _hgrn_call(dims, proj, lam, kk, hg_nw, None, dims.nb, dims.nl, 0, l, s_hg)
        o_lat, _ = _hgrn_call(dims, proj, lam, kk, hg_nw, state_hgrn[:, l], dims.db, dims.dl, dims.t_ctx)
        x = _merge_call(dims, y_ctx, y_lat, o_ctx, o_lat, proj, x, mod_l,
                        w_ssd_out[l].astype(BF16), w_hgrn_out[l].astype(BF16), w_out[l].astype(BF16))
        x = _ffn_call(dims, x, mod_l, norm_ffn_w[l].reshape(1, d),
                      ffn_w1[l].astype(BF16), ffn_w3[l].astype(BF16), ffn_w2[l].astype(BF16),
                      norm_final_w.reshape(1, d) if l == dims.depth - 1 else None)
    y_ctx, y_lat = x
    return (y_ctx.reshape(dims.nb, dims.nl, d), y_lat.reshape(dims.db, dims.dl, d),
            s_ssm.reshape(dims.nb, dims.depth, 2, h, SSD_P, SSD_N), s_hg)


def kernel(x_prompt, x_sample, c, state_ssm, state_hgrn, c_ctx, norm_mix_w, norm_ffn_w, ada_w, ada_b, w_in,
           ssd_conv_w, ssd_conv_b, ssd_dt_bias, ssd_a_log, ssd_d, ssd_norm_w, hgrn_lower_bounds, hgrn_norm_w,
           w_ssd_out, w_hgrn_out, w_out, ffn_w1, ffn_w3, ffn_w2, norm_final_w):
    dims = Dims(d=x_prompt.shape[2], nb=x_prompt.shape[0], nl=x_prompt.shape[1], db=x_sample.shape[0],
                dl=x_sample.shape[1], depth=w_in.shape[0], ffn=ffn_w1.shape[2])
    return _forward(dims, x_prompt, x_sample, c, state_ssm, state_hgrn, c_ctx, norm_mix_w, norm_ffn_w, ada_w, ada_b,
                    w_in, ssd_conv_w, ssd_conv_b, ssd_dt_bias, ssd_a_log, ssd_d, ssd_norm_w, hgrn_lower_bounds,
                    hgrn_norm_w, w_ssd_out, w_hgrn_out, w_out, ffn_w1, ffn_w3, ffn_w2, norm_final_w)
```

```python
import functools
from typing import NamedTuple

import jax
import jax.numpy as jnp
from jax import lax
from jax.experimental import pallas as pl
from jax.experimental.pallas import tpu as pltpu

F32 = jnp.float32
BF16 = jnp.bfloat16

LANE = 128
MXU_COLS = 256
VMEM_BYTES = 64 * 1024 * 1024
RMS_EPS = 1e-6
D_CONV = 5
GRID_W = 64
SSD_P = 64
SSD_N = 128
SSD_Q = 128
SSD_R = 4
SSD_CPS = 2
HG_K = 128
HG_V = 128
HG_C = 128
HG_LEVELS = (64, 32, 16)
HG_DIAG = 16
HF_BLOCKS = 2
MOD_ROWS = 8


class Dims(NamedTuple):
    d: int
    nb: int
    nl: int
    db: int
    dl: int
    depth: int
    ffn: int

    @property
    def t_ctx(self):
        return self.nb * self.nl

    @property
    def t_lat(self):
        return self.db * self.dl

    @property
    def t(self):
        return self.t_ctx + self.t_lat

    @property
    def d_ssd(self):
        return 2 * self.d

    @property
    def h_ssd(self):
        return self.d_ssd // SSD_P

    @property
    def groups(self):
        return self.h_ssd // SSD_R

    @property
    def hh(self):
        return self.d // HG_K

    @property
    def row_tile(self):
        tm = min(1024, self.dl, self.t_ctx)
        assert self.dl % tm == 0 and self.t_ctx % tm == 0 and tm % self.nl == 0 and tm % GRID_W == 0
        return tm


def _cparams(semantics, vmem_bytes):
    return pltpu.CompilerParams(dimension_semantics=semantics, vmem_limit_bytes=min(vmem_bytes, VMEM_BYTES - (4 << 20)))


def _dot(a, b):
    return jnp.dot(a, b, preferred_element_type=F32)


def _dot_nt(a, b):
    return lax.dot_general(a, b, (((1,), (1,)), ((), ())), preferred_element_type=F32)


def _dot_tn(a, b):
    return lax.dot_general(a, b, (((0,), (0,)), ((), ())), preferred_element_type=F32)


def _split3(x):
    hi = x.astype(BF16)
    r = x - hi.astype(F32)
    mid = r.astype(BF16)
    lo = (r - mid.astype(F32)).astype(BF16)
    return hi, mid, lo


def _sum01_right(x, m01):
    hi, mid, lo = _split3(x)
    return _dot(hi, m01) + _dot(mid, m01) + _dot(lo, m01)


def _sigmoid(x):
    return 0.5 * jnp.tanh(0.5 * x) + 0.5


def _silu(x):
    return x * _sigmoid(x)


def _softplus(x):
    return jnp.maximum(x, 0.0) + jnp.log1p(jnp.exp(-jnp.abs(x)))


def _mod_row(i, tm, dims):
    nctx = dims.t_ctx // tm
    return jnp.where(i < nctx, 0, 1 + ((i - nctx) * tm) // dims.dl)


def _mod_kernel(c_ref, w_ref, b_ref, o_ref):
    s = _silu(c_ref[...])
    o_ref[0] = _dot(s.astype(BF16), w_ref[0].astype(BF16)) + b_ref[0]


def _mod_call(dims, cvec, ada_w, ada_b):
    d, n = dims.d, 6 * dims.d
    tn = n // 4
    return pl.pallas_call(
        _mod_kernel,
        grid=(dims.depth, n // tn),
        in_specs=[pl.BlockSpec((MOD_ROWS, d), lambda l, j: (0, 0)),
                  pl.BlockSpec((1, d, tn), lambda l, j: (l, 0, j)),
                  pl.BlockSpec((1, 1, tn), lambda l, j: (l, 0, j))],
        out_specs=pl.BlockSpec((1, MOD_ROWS, tn), lambda l, j: (l, 0, j)),
        out_shape=jax.ShapeDtypeStruct((dims.depth, MOD_ROWS, n), F32),
        compiler_params=_cparams(("arbitrary", "arbitrary"), 2 * d * tn * 4 + (8 << 20)),
        name="adaln_mod",
    )(cvec, ada_w, ada_b.reshape(dims.depth, 1, n))


def _inproj_kernel(dims, tm, x_ref, mod_ref, nw_ref, wa_ref, wb_ref, cw_ref, cb_ref, wdt_ref, dtb_ref, alog_ref, lb_ref,
                   proj_ref, lam_ref, kk_ref, dt_ref, da_ref, h_s):
    d = dims.d
    i = pl.program_id(0)
    j = pl.program_id(1)
    n_conv = (dims.d_ssd + 2 * dims.groups * SSD_N) // d
    n_a = n_conv + dims.d_ssd // d
    n_bf = pl.num_programs(1) - HF_BLOCKS

    @pl.when(j == 0)
    def _():
        x = x_ref[...]
        y = x * lax.rsqrt(jnp.mean(x * x, axis=-1, keepdims=True) + RMS_EPS) * nw_ref[...]
        mod = mod_ref[0]
        hb = (y * (1.0 + mod[:, d:2 * d]) + mod[:, 0:d]).astype(BF16)
        h_s[...] = hb
        wdt_t = wdt_ref[...].T.astype(BF16)
        dt = _softplus(_dot_nt(wdt_t, hb) + dtb_ref[...])
        da = dt * -jnp.exp(alog_ref[...])
        for k in range(tm // SSD_Q):
            dt_ref[k] = dt[:, k * SSD_Q:(k + 1) * SSD_Q]
            da_ref[k] = da[:, k * SSD_Q:(k + 1) * SSD_Q]

    @pl.when(j >= n_bf)
    def _():
        lam, kk = _hgrn_gates(_dot(h_s[...], wb_ref[...]), lb_ref[pl.ds(j - n_bf, 1), :])
        lam_ref[...] = lam
        kk_ref[...] = kk.astype(BF16)

    @pl.when((j >= n_conv) & (j < n_a))
    def _():
        proj_ref[...] = _dot(h_s[...], wa_ref[...]).astype(BF16)

    @pl.when((j >= n_a) & (j < n_bf))
    def _():
        proj_ref[...] = _dot(h_s[...], wb_ref[...]).astype(BF16)

    def conv_silu(seg):
        ns, half, sub = tm // seg, D_CONV // 2, 8
        n8 = ns * sub
        r8 = lax.broadcasted_iota(jnp.int32, (n8, LANE), 0) & (sub - 1)
        res = None
        for sl in range(d // LANE):
            cols = slice(sl * LANE, (sl + 1) * LANE)
            if sl % (MXU_COLS // LANE) == 0:
                res = _dot(h_s[...], wa_ref[:, sl * LANE:sl * LANE + MXU_COLS])
            u = res[:, (sl * LANE) % MXU_COLS:(sl * LANE) % MXU_COLS + LANE]
            w = [cw_ref[k:k + 1, cols] for k in range(D_CONV)]
            acc = cb_ref[:, cols] + w[half] * u
            for k in range(D_CONV):
                if k != half:
                    acc = acc + w[k] * pltpu.roll(u, (half - k) % tm, 0)
            u3 = u.reshape(ns, seg, LANE)
            a3 = acc.reshape(ns, seg, LANE)
            edge_lo = u3[:, 0:sub, :].reshape(n8, LANE)
            edge_hi = u3[:, seg - sub:seg, :].reshape(n8, LANE)
            leak_lo = (w[0] * jnp.where(r8 < 2, pltpu.roll(edge_hi, 2, 0), 0.0)
                       + w[1] * jnp.where(r8 < 1, pltpu.roll(edge_hi, 1, 0), 0.0))
            leak_hi = (w[4] * jnp.where(r8 >= sub - 2, pltpu.roll(edge_lo, n8 - 2, 0), 0.0)
                       + w[3] * jnp.where(r8 >= sub - 1, pltpu.roll(edge_lo, n8 - 1, 0), 0.0))
            a3 = jnp.concatenate([a3[:, 0:sub, :] - leak_lo.reshape(ns, sub, LANE),
                                  a3[:, sub:seg - sub, :],
                                  a3[:, seg - sub:seg, :] - leak_hi.reshape(ns, sub, LANE)], axis=1)
            proj_ref[:, cols] = _silu(a3.reshape(tm, LANE)).astype(BF16)

    is_ctx = i < dims.t_ctx // tm

    @pl.when((j < n_conv) & is_ctx)
    def _():
        conv_silu(dims.nl)

    @pl.when((j < n_conv) & jnp.logical_not(is_ctx))
    def _():
        conv_silu(GRID_W)


def _inproj_call(dims, x, mod_l, nw, wa, wb, cw, cb, wdt, dtb, alog, lb):
    d, t, tm = dims.d, dims.t, dims.row_tile
    n_conv = (dims.d_ssd + 2 * dims.groups * SSD_N) // d
    n_a = wa.shape[1] // d
    ncol = n_a + wb.shape[1] // d
    nchunk = t // SSD_Q
    n_bf = ncol - HF_BLOCKS

    def wb_block(j):
        return jnp.where(j <= n_a, 0, jnp.where(j < n_bf, j - n_a + HF_BLOCKS, j - n_bf + 1))

    out_shapes = (jax.ShapeDtypeStruct((t, n_bf * d), BF16),
                  jax.ShapeDtypeStruct((t, HF_BLOCKS * d), F32),
                  jax.ShapeDtypeStruct((t, HF_BLOCKS * d), BF16),
                  jax.ShapeDtypeStruct((nchunk, LANE, SSD_Q), F32),
                  jax.ShapeDtypeStruct((nchunk, LANE, SSD_Q), F32))
    kc = tm // SSD_Q
    gate_spec = pl.BlockSpec((tm, d), lambda i, j: (i, jnp.maximum(j - n_bf, 0)))
    return pl.pallas_call(
        functools.partial(_inproj_kernel, dims, tm),
        grid=(t // tm, ncol),
        in_specs=[pl.BlockSpec((tm, d), lambda i, j: (i, 0)),
                  pl.BlockSpec((1, 1, 6 * d), lambda i, j: (_mod_row(i, tm, dims), 0, 0)),
                  pl.BlockSpec((1, d), lambda i, j: (0, 0)),
                  pl.BlockSpec((d, d), lambda i, j: (0, jnp.minimum(j, n_a - 1))),
                  pl.BlockSpec((d, d), lambda i, j: (0, wb_block(j))),
                  pl.BlockSpec((D_CONV, d), lambda i, j: (0, jnp.minimum(j, n_conv - 1))),
                  pl.BlockSpec((1, d), lambda i, j: (0, jnp.minimum(j, n_conv - 1))),
                  pl.BlockSpec((d, LANE), lambda i, j: (0, 0)),
                  pl.BlockSpec((LANE, 1), lambda i, j: (0, 0)),
                  pl.BlockSpec((LANE, 1), lambda i, j: (0, 0)),
                  pl.BlockSpec((8, d), lambda i, j: (0, 0))],
        out_specs=(pl.BlockSpec((tm, d), lambda i, j: (i, jnp.minimum(j, n_bf - 1))),
                   gate_spec, gate_spec,
                   pl.BlockSpec((kc, LANE, SSD_Q), lambda i, j: (i, 0, 0)),
                   pl.BlockSpec((kc, LANE, SSD_Q), lambda i, j: (i, 0, 0))),
        out_shape=out_shapes,
        scratch_shapes=[pltpu.VMEM((tm, d), BF16)],
        compiler_params=_cparams(("arbitrary", "arbitrary"), 8 * tm * d * 4 + 4 * d * d * 2 + tm * d * 2 + (8 << 20)),
        name="inproj",
    )(x, mod_l, nw, wa, wb, cw, cb, wdt, dtb, alog, lb)


def _ssd_kernel(dims, nchunk, has_init, has_prev, *refs):
    x_ref, b_ref, c_ref, z_ref, dt_ref, da_ref, dsk_ref, nw_ref = refs[:8]
    s0_ref = refs[8] if has_init else None
    (y_ref, sfin_ref, st, yf_s, xt_s, yb_s, bg_s, cg_s, cs_s, ecs_s, wend_s, dec_s,
     dtr_s) = refs[8 + has_init + has_prev:]
    h, g_n = dims.h_ssd, dims.groups
    q = SSD_Q
    nstep = nchunk // SSD_CPS
    s = pl.program_id(1)
    is_bwd = s >= nstep
    blk = jnp.where(is_bwd, 2 * nstep - 1 - s, s)
    first = (s == 0) | (s == nstep)
    last = (s == nstep - 1) | (s == 2 * nstep - 1)

    @pl.when(first)
    def _():
        st[...] = s0_ref[0, 0] if has_init else jnp.zeros(st.shape, F32)

    jj = lax.broadcasted_iota(jnp.int32, (q, q), 0)
    ii = lax.broadcasted_iota(jnp.int32, (q, q), 1)
    seen = jnp.where(is_bwd, jj - ii, ii - jj) >= 0
    row0 = pl.multiple_of(jnp.where(is_bwd, h, 0), 8)
    gp = SSD_R * SSD_P

    def chunk_body(k, chunk_carry):
        sub = jnp.where(is_bwd, SSD_CPS - 1 - k, k)
        ck = blk * SSD_CPS + sub
        rows = pl.ds(pl.multiple_of(sub * q, q), q)
        dtr = dt_ref[sub, pl.ds(row0, h), :]
        dar = da_ref[sub, pl.ds(row0, h), :]
        cs = _sum01_right(dar, seen.astype(BF16))
        tot = _sum01_right(dar, jnp.ones((q, q), BF16))
        cs_s[...] = cs
        ecs_s[...] = jnp.exp(cs)
        wend_s[...] = jnp.exp(tot - cs) * dtr
        dec_s[...] = jnp.exp(tot)
        dtr_s[...] = dtr
        xt_s[...] = x_ref[rows, :].astype(F32).T
        for g in range(g_n):
            bg_s[g] = b_ref[rows, g * SSD_N:(g + 1) * SSD_N]
            cg_s[g] = c_ref[rows, g * SSD_N:(g + 1) * SSD_N]

        def group_body(g, carry):
            bg = bg_s[g]
            cg = cg_s[g]
            cbt = _dot_nt(bg, cg)
            r0 = pl.multiple_of(g * gp, gp)
            sg = st[pl.ds(r0, gp), :]
            sc = _dot_nt(sg.astype(BF16), cg)
            xg = xt_s[pl.ds(r0, gp), :]
            xw, decs = [], []
            for r in range(SSD_R):
                hd = g * SSD_R + r
                xh = xg[r * SSD_P:(r + 1) * SSD_P]
                cs_i = cs_s[pl.ds(hd, 1), :]
                cs_j = jnp.broadcast_to(cs_i, (q, q)).T
                decay = jnp.exp(jnp.where(seen, cs_i - cs_j, -jnp.inf))
                mt = (cbt * decay).astype(BF16)
                xdt = (xh * dtr_s[pl.ds(hd, 1), :]).astype(BF16)
                yh = _dot(xdt, mt) + sc[r * SSD_P:(r + 1) * SSD_P] * ecs_s[pl.ds(hd, 1), :]
                yb_s[pl.ds(pl.multiple_of(r0 + r * SSD_P, SSD_P), SSD_P), :] = yh
                xw.append((xh * wend_s[pl.ds(hd, 1), :]).astype(BF16))
                decs.append(jnp.broadcast_to(dec_s[pl.ds(hd, 1), :], (SSD_P, SSD_N)))
            upd = _dot(jnp.concatenate(xw, axis=0), bg)
            st[pl.ds(r0, gp), :] = sg * jnp.concatenate(decs, axis=0) + upd
            return carry

        lax.fori_loop(0, g_n, group_body, 0, unroll=True)

        @pl.when(jnp.logical_not(is_bwd))
        def _():
            yf_s[ck] = yb_s[...]

        @pl.when(is_bwd)
        def _():
            yt = yb_s[...] + yf_s[ck] + dsk_ref[...] * xt_s[...]
            yt = yt * _silu(z_ref[rows, :].astype(F32).T)
            y3 = yt.reshape(g_n, gp, q)
            yn = y3 * lax.rsqrt(jnp.mean(y3 * y3, axis=1, keepdims=True) + RMS_EPS)
            y_ref[rows, :] = (yn.reshape(dims.d_ssd, q) * nw_ref[...]).T.astype(BF16)
        return chunk_carry

    lax.fori_loop(0, SSD_CPS, chunk_body, 0)

    @pl.when(last)
    def _():
        sfin_ref[0, 0, 0] = st[...]


def _ssd_call(dims, proj, dt_t, da_t, dskip, nw, s0, n_seq, seq_len, tok_off, layer=None, fin_prev=None):
    d, ds_ = dims.d, dims.d_ssd
    n_slab, slab = (1, 0) if layer is None else (dims.depth, layer)
    q = SSD_Q
    nchunk = seq_len // q
    assert nchunk % SSD_CPS == 0
    rb = SSD_CPS * q
    nstep = nchunk // SSD_CPS
    off = tok_off // rb
    has_init = s0 is not None

    def tok(b, s):
        return off + b * nstep + jnp.where(s < nstep, s, 2 * nstep - 1 - s)

    def ytok(b, s):
        return b * nstep + jnp.where(s < nstep, nstep - 1, 2 * nstep - 1 - s)

    in_specs = [pl.BlockSpec((rb, ds_), lambda b, s: (tok(b, s), 0)),
                pl.BlockSpec((rb, d), lambda b, s: (tok(b, s), ds_ // d)),
                pl.BlockSpec((rb, d), lambda b, s: (tok(b, s), ds_ // d + 1)),
                pl.BlockSpec((rb, ds_), lambda b, s: (off + ytok(b, s), (ds_ + 2 * d) // ds_)),
                pl.BlockSpec((SSD_CPS, LANE, q), lambda b, s: (tok(b, s), 0, 0)),
                pl.BlockSpec((SSD_CPS, LANE, q), lambda b, s: (tok(b, s), 0, 0)),
                pl.BlockSpec((ds_, q), lambda b, s: (0, 0)),
                pl.BlockSpec((ds_, q), lambda b, s: (0, 0))]
    args = [proj, proj, proj, proj, dt_t, da_t, dskip, nw]
    if has_init:
        in_specs.append(pl.BlockSpec((1, 1, ds_, SSD_N), lambda b, s: (b, s // nstep, 0, 0)))
        args.append(s0)
    aliases = {}
    if fin_prev is not None:
        in_specs.append(pl.BlockSpec(memory_space=pl.ANY))
        aliases = {len(args): 1}
        args.append(fin_prev)
    h = dims.h_ssd
    scratch = [pltpu.VMEM((ds_, SSD_N), F32),
               pltpu.VMEM((nchunk, ds_, q), F32),
               pltpu.VMEM((ds_, q), F32),
               pltpu.VMEM((ds_, q), F32),
               pltpu.VMEM((dims.groups, q, SSD_N), BF16),
               pltpu.VMEM((dims.groups, q, SSD_N), BF16),
               pltpu.VMEM((h, q), F32), pltpu.VMEM((h, q), F32), pltpu.VMEM((h, q), F32),
               pltpu.VMEM((h, q), F32), pltpu.VMEM((h, q), F32)]
    vmem = (2 * (3 * rb * ds_ + 2 * rb * d + 2 * SSD_CPS * LANE * q + 2 * ds_ * q + 2 * ds_ * SSD_N) * 4
            + (nchunk + 3) * ds_ * q * 4 + (12 << 20))
    return pl.pallas_call(
        functools.partial(_ssd_kernel, dims, nchunk, has_init, fin_prev is not None),
        grid=(n_seq, 2 * nstep),
        in_specs=in_specs,
        out_specs=(pl.BlockSpec((rb, ds_), lambda b, s: (ytok(b, s), 0)),
                   pl.BlockSpec((1, 1, 1, ds_, SSD_N), lambda b, s: (b, slab, s // nstep, 0, 0))),
        out_shape=(jax.ShapeDtypeStruct((n_seq * seq_len, ds_), BF16),
                   jax.ShapeDtypeStruct((n_seq, n_slab, 2, ds_, SSD_N), F32)),
        input_output_aliases=aliases,
        scratch_shapes=scratch,
        compiler_params=_cparams(("arbitrary", "arbitrary"), vmem),
        name="ssd_scan",
    )(*args)


def _hgrn_gates(u, lb):
    t = jnp.exp(-jnp.abs(u))
    pos = u >= 0.0
    den = 1.0 + t
    num = jnp.where(pos, 1.0 + lb * t, lb + t)
    log_num = jnp.where(pos | (jnp.broadcast_to(lb, u.shape) > 0.0), jnp.log(num), u)
    return log_num - jnp.log(den), (1.0 - lb) * jnp.where(pos, t, 1.0) / den


def _hg_dir_operands(lam, kk, qq, blockmat, fwd):
    c, w = lam.shape
    bs, nb = HG_DIAG, lam.shape[0] // HG_DIAG
    lam_hi = lam.astype(BF16)
    lam_lo = (lam - lam_hi.astype(F32)).astype(BF16)
    p = _dot(blockmat, lam_hi) + _dot(blockmat, lam_lo)
    edge, mid = (bs - 1, bs // 2 - 1) if fwd else (0, bs // 2)
    tot = [p[b * bs + edge:b * bs + edge + 1] for b in range(nb)]
    p_mid = [p[b * bs + mid:b * bs + mid + 1] for b in range(nb)]

    def spread(rows):
        return jnp.concatenate([jnp.broadcast_to(r, (bs, w)) for r in rows], axis=0)

    s = spread(tot) - p
    qp = qq * jnp.exp(p)
    ks = kk * jnp.exp(s)
    zero = jnp.zeros((1, w), F32)

    def run_sums(lo_b, hi_b):
        before, acc = {}, zero
        order = range(lo_b, hi_b) if fwd else range(hi_b - 1, lo_b - 1, -1)
        for b in order:
            before[b] = acc
            acc = acc + tot[b]
        after, acc = {}, zero
        for b in reversed(order):
            after[b] = acc
            acc = acc + tot[b]
        return before, after, acc

    def scaled(x, logs):
        zblk = jnp.zeros((bs, w), BF16)
        parts = []
        for b in range(nb):
            xb = x[b * bs:(b + 1) * bs]
            if logs[b] is None:
                parts.append(zblk)
            elif logs[b] is zero:
                parts.append(xb.astype(BF16))
            else:
                parts.append((xb * jnp.exp(logs[b])).astype(BF16))
        return jnp.concatenate(parts, axis=0)

    level_q, level_k = [], []
    for hf in HG_LEVELS:
        nh = hf // bs
        ql, kl = [None] * nb, [None] * nb
        for par in range(nb // (2 * nh)):
            halves = (par * 2 * nh, par * 2 * nh + nh), (par * 2 * nh + nh, (par + 1) * 2 * nh)
            key_half, query_half = halves if fwd else halves[::-1]
            _, after, _ = run_sums(*key_half)
            before, _, _ = run_sums(*query_half)
            for b in range(*key_half):
                kl[b] = after[b]
            for b in range(*query_half):
                ql[b] = before[b]
        level_q.append(scaled(qp, ql))
        level_k.append(scaled(ks, kl))
    s_mid = [t - pm for t, pm in zip(tot, p_mid)]
    diag_q = (qp * spread([jnp.exp(-pm) for pm in p_mid])).astype(BF16)
    diag_k = (ks * spread([jnp.exp(-sm) for sm in s_mid])).astype(BF16)
    before, after, total = run_sums(0, nb)
    state_q = scaled(qp, [before[b] for b in range(nb)])
    state_k = scaled(ks, [after[b] for b in range(nb)])
    return level_q, level_k, diag_q, diag_k, state_q, state_k, jnp.exp(total)


def _hgrn_kernel(dims, nblk, tb, has_init, has_prev, *refs):
    q_ref, lamf_ref, lamb_ref, kf_ref, kb_ref, v_ref, g_ref, nw_ref = refs[:8]
    s0_ref = refs[8] if has_init else None
    o_ref, sfin_ref, st, o_s, qb_s, kb_s, decb_s = refs[8 + has_init + has_prev:]
    hh = dims.hh
    c = HG_C
    nck = tb // c
    single = nblk == 1
    s = pl.program_id(1)
    is_bwd = s >= nblk
    blk = 0 if single else jnp.where(is_bwd, 2 * nblk - 1 - s, s)

    def init_state(direction):
        for hd in range(hh):
            st[hd] = s0_ref[0, direction, hd].T if has_init else jnp.zeros((HG_V, HG_K), F32)

    def save_state(direction):
        for hd in range(hh):
            sfin_ref[0, 0, direction, hd] = st[hd].T

    ti = lax.broadcasted_iota(jnp.int32, (c, c), 0)
    tj = lax.broadcasted_iota(jnp.int32, (c, c), 1)
    same_diag = (ti // HG_DIAG) == (tj // HG_DIAG)
    diag_f = same_diag & (tj <= ti)
    diag_b = same_diag & (tj >= ti)
    mask_df, mask_db = diag_f.astype(F32), diag_b.astype(F32)
    parent = {hf: ((ti // (2 * hf)) == (tj // (2 * hf))).astype(F32) for hf in HG_LEVELS if 2 * hf < c}

    def heads(x):
        return [x[:, hd * HG_K:(hd + 1) * HG_K] for hd in range(hh)]

    def state_step(hd, qs_h, ke_h, v_h, dec_h):
        sh = st[hd]
        off = _dot_nt(qs_h, sh.astype(BF16))
        st[hd] = sh * dec_h + _dot_tn(v_h, ke_h)
        return off

    def fwd_chunk(idx, carry):
        r0 = pl.multiple_of(idx * c, c)
        rows = pl.ds(r0, c)
        g0 = pl.multiple_of(blk * tb + r0, c)
        qq = _silu(q_ref[rows, :].astype(F32))
        vh = heads(v_ref[rows, :])
        lq_f, lk_f, dq_f, dk_f, sq_f, sk_f, dec_f = _hg_dir_operands(
            lamf_ref[rows, :], kf_ref[rows, :].astype(F32), qq, diag_f.astype(BF16), True)
        lq_b, lk_b, dq_b, dk_b, sq_b, sk_b, dec_b = _hg_dir_operands(
            lamb_ref[rows, :], kb_ref[rows, :].astype(F32), qq, diag_b.astype(BF16), False)
        qb_s[pl.ds(g0, c), :] = sq_b
        kb_s[pl.ds(g0, c), :] = sk_b
        decb_s[g0 // c] = dec_b
        lq_f, lk_f, lq_b, lk_b = ([heads(x) for x in xs] for xs in (lq_f, lk_f, lq_b, lk_b))
        dq_f, dk_f, dq_b, dk_b, sq_f, sk_f, dec_f = (heads(x) for x in (dq_f, dk_f, dq_b, dk_b, sq_f, sk_f, dec_f))
        outs = []
        for hd in range(hh):
            att = _dot_nt(dq_f[hd], dk_f[hd]) * mask_df + _dot_nt(dq_b[hd], dk_b[hd]) * mask_db
            for li, hf in enumerate(HG_LEVELS):
                p = _dot_nt(jnp.concatenate([lq_f[li][hd], lq_b[li][hd]], axis=1),
                            jnp.concatenate([lk_f[li][hd], lk_b[li][hd]], axis=1))
                att = att + (p * parent[hf] if hf in parent else p)
            o_h = _dot(att.astype(BF16), vh[hd])
            outs.append(o_h + state_step(hd, sq_f[hd], sk_f[hd], vh[hd], dec_f[hd]))
        o_s[pl.ds(g0, c), :] = jnp.concatenate(outs, axis=1)
        return carry

    def bwd_chunk(idx, carry):
        r0 = pl.multiple_of((nck - 1 - idx) * c, c)
        rows = pl.ds(r0, c)
        g0 = pl.multiple_of(blk * tb + r0, c)
        vh = heads(v_ref[rows, :])
        qs = heads(qb_s[pl.ds(g0, c), :])
        ke = heads(kb_s[pl.ds(g0, c), :])
        dec = heads(decb_s[g0 // c])
        o_prev = heads(o_s[pl.ds(g0, c), :])
        gate = heads(_silu(g_ref[rows, :].astype(F32)))
        nw = heads(nw_ref[...])
        outs = []
        for hd in range(hh):
            o_h = o_prev[hd] + state_step(hd, qs[hd], ke[hd], vh[hd], dec[hd])
            o_h = o_h * lax.rsqrt(jnp.mean(o_h * o_h, axis=-1, keepdims=True) + RMS_EPS) * nw[hd]
            outs.append(o_h * gate[hd])
        o_ref[rows, :] = jnp.concatenate(outs, axis=1).astype(BF16)
        return carry

    if single:
        init_state(0)
        lax.fori_loop(0, nck, fwd_chunk, 0)
        save_state(0)
        init_state(1)
        lax.fori_loop(0, nck, bwd_chunk, 0)
        save_state(1)
        return

    @pl.when(s == 0)
    def _():
        init_state(0)

    @pl.when(s == nblk)
    def _():
        init_state(1)

    @pl.when(jnp.logical_not(is_bwd))
    def _():
        lax.fori_loop(0, nck, fwd_chunk, 0)

    @pl.when(is_bwd)
    def _():
        lax.fori_loop(0, nck, bwd_chunk, 0)

    @pl.when(s == nblk - 1)
    def _():
        save_state(0)

    @pl.when(s == 2 * nblk - 1)
    def _():
        save_state(1)


def _hgrn_call(dims, proj, lam, kk, nw, s0, n_seq, seq_len, tok_off, layer=None, fin_prev=None):
    d, hh = dims.d, dims.hh
    n_slab, slab = (1, 0) if layer is None else (dims.depth, layer)
    tb = min(512, seq_len)
    nblk = seq_len // tb
    off = tok_off // tb
    has_init = s0 is not None
    col0 = (dims.d_ssd + 2 * d + dims.d_ssd) // d

    def tok(b, s):
        return off + b * nblk + jnp.where(s < nblk, s, 2 * nblk - 1 - s)

    def tok_pass1(b, s):
        return off + b * nblk + jnp.minimum(s, nblk - 1)

    def otok(b, s):
        return b * nblk + jnp.where(s < nblk, nblk - 1, 2 * nblk - 1 - s)

    def tok_pass2(b, s):
        return off + otok(b, s)

    srcs = ((proj, col0, tok_pass1), (lam, 0, tok_pass1), (lam, 1, tok_pass1), (kk, 0, tok_pass1), (kk, 1, tok_pass1),
            (proj, col0 + 1, tok), (proj, col0 + 2, tok_pass2))
    in_specs = [pl.BlockSpec((tb, d), functools.partial(lambda cc, tk, b, s: (tk(b, s), cc), col, tk))
                for _, col, tk in srcs]
    in_specs += [pl.BlockSpec((1, d), lambda b, s: (0, 0))]
    args = [a for a, _, _ in srcs] + [nw]
    if has_init:
        in_specs.append(pl.BlockSpec((1, 2, hh, HG_K, HG_V), lambda b, s: (b, 0, 0, 0, 0)))
        args.append(s0)
    aliases = {}
    if fin_prev is not None:
        in_specs.append(pl.BlockSpec(memory_space=pl.ANY))
        aliases = {len(args): 1}
        args.append(fin_prev)
    vmem = 2 * 7 * tb * d * 4 + 2 * seq_len * d * 4 + 4 * hh * HG_K * HG_V * 4 + (16 << 20)
    return pl.pallas_call(
        functools.partial(_hgrn_kernel, dims, nblk, tb, has_init, fin_prev is not None),
        grid=(n_seq, 1 if nblk == 1 else 2 * nblk),
        in_specs=in_specs,
        out_specs=(pl.BlockSpec((tb, d), lambda b, s: (otok(b, s), 0)),
                   pl.BlockSpec((1, 1, 2, hh, HG_K, HG_V), lambda b, s: (b, slab, 0, 0, 0, 0))),
        out_shape=(jax.ShapeDtypeStruct((n_seq * seq_len, d), BF16),
                   jax.ShapeDtypeStruct((n_seq, n_slab, 2, hh, HG_K, HG_V), F32)),
        input_output_aliases=aliases,
        scratch_shapes=[pltpu.VMEM((hh, HG_V, HG_K), F32),
                        pltpu.VMEM((seq_len, d), F32),
                        pltpu.VMEM((seq_len, d), BF16),
                        pltpu.VMEM((seq_len, d), BF16),
                        pltpu.VMEM((seq_len // HG_C, 1, d), F32)],
        compiler_params=_cparams(("arbitrary", "arbitrary"), vmem),
        name="hgrn_scan",
    )(*args)


def _merge_kernel(dims, tm, yc_ref, yl_ref, oc_ref, ol_ref, g1_ref, g2_ref, x_ref, mod_ref,
                  wso_ref, who_ref, wo_ref, out_ref):
    d = dims.d
    is_ctx = pl.program_id(0) < dims.t_ctx // tm
    y = jnp.where(is_ctx, yc_ref[...], yl_ref[...])
    o = jnp.where(is_ctx, oc_ref[...], ol_ref[...])
    y_ssd = _dot(y, wso_ref[...])
    y_hg = _dot(o, who_ref[...])
    merged = _sigmoid(g1_ref[...].astype(F32)) * y_ssd + _sigmoid(g2_ref[...].astype(F32)) * y_hg
    mix = _dot(merged.astype(BF16), wo_ref[...])
    out_ref[...] = x_ref[...] + mod_ref[0][:, 2 * d:3 * d] * mix


def _merge_call(dims, y_ctx, y_lat, o_ctx, o_lat, proj, x, mod_l, wso, who, wo):
    d, ds_, t = dims.d, dims.d_ssd, dims.t
    tm = min(512, dims.row_tile)
    nctx = dims.t_ctx // tm
    gcol = proj.shape[1] // d - 2

    def ctx_blk(i):
        return jnp.minimum(i, nctx - 1)

    def lat_blk(i):
        return jnp.maximum(i - nctx, 0)

    const = lambda i: (0, 0)
    return pl.pallas_call(
        functools.partial(_merge_kernel, dims, tm),
        grid=(t // tm,),
        in_specs=[pl.BlockSpec((tm, ds_), lambda i: (ctx_blk(i), 0)),
                  pl.BlockSpec((tm, ds_), lambda i: (lat_blk(i), 0)),
                  pl.BlockSpec((tm, d), lambda i: (ctx_blk(i), 0)),
                  pl.BlockSpec((tm, d), lambda i: (lat_blk(i), 0)),
                  pl.BlockSpec((tm, d), lambda i: (i, gcol)),
                  pl.BlockSpec((tm, d), lambda i: (i, gcol + 1)),
                  pl.BlockSpec((tm, d), lambda i: (i, 0)),
                  pl.BlockSpec((1, 1, 6 * d), lambda i: (_mod_row(i, tm, dims), 0, 0)),
                  pl.BlockSpec((ds_, d), const), pl.BlockSpec((d, d), const), pl.BlockSpec((d, d), const)],
        out_specs=pl.BlockSpec((tm, d), lambda i: (i, 0)),
        out_shape=jax.ShapeDtypeStruct((t, d), F32),
        compiler_params=_cparams(("arbitrary",), 2 * (2 * tm * ds_ + 6 * tm * d) * 4 + 2 * (ds_ + 2 * d) * d * 2 + (12 << 20)),
        name="merge_outproj",
    )(y_ctx, y_lat, o_ctx, o_lat, proj, proj, x, mod_l, wso, who, wo)


def _ffn_kernel(dims, tm, final_norm, x_ref, mod_ref, nw_ref, w1_ref, w3_ref, w2_ref, *rest):
    d = dims.d
    if final_norm:
        fw_ref, out_ctx_ref, out_lat_ref = rest
    else:
        (out_ref,) = rest

    x = x_ref[...]
    y = x * lax.rsqrt(jnp.mean(x * x, axis=-1, keepdims=True) + RMS_EPS) * nw_ref[...]
    mod = mod_ref[0]
    hb = (y * (1.0 + mod[:, 4 * d:5 * d]) + mod[:, 3 * d:4 * d]).astype(BF16)
    acc = None
    for lo in range(0, dims.ffn, MXU_COLS):
        hi = min(lo + MXU_COLS, dims.ffn)
        act = _silu(_dot(hb, w1_ref[:, lo:hi])) * _dot(hb, w3_ref[:, lo:hi])
        part = _dot(act.astype(BF16), w2_ref[lo:hi, :])
        acc = part if acc is None else acc + part
    out = x + mod[:, 5 * d:6 * d] * acc
    if not final_norm:
        out_ref[...] = out
        return
    out = out * lax.rsqrt(jnp.mean(out * out, axis=-1, keepdims=True) + RMS_EPS) * fw_ref[...]
    is_ctx = pl.program_id(0) < dims.t_ctx // tm

    @pl.when(is_ctx)
    def _():
        out_ctx_ref[...] = out

    @pl.when(jnp.logical_not(is_ctx))
    def _():
        out_lat_ref[...] = out


def _ffn_call(dims, x, mod_l, nw, w1, w3, w2, final_nw=None):
    d, t, f = dims.d, dims.t, dims.ffn
    tm = min(512, dims.row_tile)
    nctx = dims.t_ctx // tm
    const = lambda i: (0, 0)
    resident = pl.Buffered(1)
    in_specs = [pl.BlockSpec((tm, d), lambda i: (i, 0)),
                pl.BlockSpec((1, 1, 6 * d), lambda i: (_mod_row(i, tm, dims), 0, 0)),
                pl.BlockSpec((1, d), const),
                pl.BlockSpec((d, f), const, pipeline_mode=resident),
                pl.BlockSpec((d, f), const, pipeline_mode=resident),
                pl.BlockSpec((f, d), const, pipeline_mode=resident)]
    args = [x, mod_l, nw, w1, w3, w2]
    out_specs = pl.BlockSpec((tm, d), lambda i: (i, 0))
    out_shape = jax.ShapeDtypeStruct((t, d), F32)
    if final_nw is not None:
        in_specs.append(pl.BlockSpec((1, d), const))
        args.append(final_nw)
        out_specs = (pl.BlockSpec((tm, d), lambda i: (jnp.minimum(i, nctx - 1), 0)),
                     pl.BlockSpec((tm, d), lambda i: (jnp.maximum(i - nctx, 0), 0)))
        out_shape = (jax.ShapeDtypeStruct((dims.t_ctx, d), F32), jax.ShapeDtypeStruct((dims.t_lat, d), F32))
    return pl.pallas_call(
        functools.partial(_ffn_kernel, dims, tm, final_nw is not None),
        grid=(t // tm,),
        in_specs=in_specs,
        out_specs=out_specs,
        out_shape=out_shape,
        compiler_params=_cparams(("arbitrary",), 4 * tm * d * 4 + 3 * d * f * 2 + 6 * tm * d * 4 + (12 << 20)),
        name="ffn",
    )(*args)


def _forward(dims, x_prompt, x_sample, c, state_ssm, state_hgrn, c_ctx, norm_mix_w, norm_ffn_w, ada_w, ada_b, w_in,
             ssd_conv_w, ssd_conv_b, ssd_dt_bias, ssd_a_log, ssd_d, ssd_norm_w, hgrn_lower_bounds, hgrn_norm_w,
             w_ssd_out, w_hgrn_out, w_out, ffn_w1, ffn_w3, ffn_w2, norm_final_w):
    d, ds_, h = dims.d, dims.d_ssd, dims.h_ssd
    assert dims.db + 1 <= MOD_ROWS and 2 * h <= LANE and dims.nl % HG_C == 0 and dims.dl % 512 == 0
    conv_dim = ds_ + 2 * dims.groups * SSD_N
    x = jnp.concatenate([x_prompt.reshape(dims.t_ctx, d), x_sample.reshape(dims.t_lat, d)], axis=0)
    cvec = jnp.zeros((MOD_ROWS, d), F32).at[0].set(c_ctx).at[1:1 + dims.db].set(c)
    mod = _mod_call(dims, cvec, ada_w, ada_b)
    lb_all = jnp.cumsum(jax.nn.softmax(hgrn_lower_bounds.astype(F32), axis=0), axis=0)
    lb_all = lb_all - lb_all[:1]
    dt_cols = slice(conv_dim + ds_, conv_dim + ds_ + 2 * h)

    s_ssm = s_hg = None
    for l in range(dims.depth):
        mod_l = mod[l].reshape(MOD_ROWS, 1, 6 * d)
        w_l = w_in[l]
        w_a = w_l[:, :dt_cols.start].astype(BF16)
        w_b = w_l[:, dt_cols.stop:].astype(BF16)
        wdt = jnp.pad(w_l[:, dt_cols], ((0, 0), (0, LANE - 2 * h)))
        dtb = jnp.zeros((LANE, 1), F32).at[:2 * h, 0].set(ssd_dt_bias[l].reshape(-1))
        alog = jnp.zeros((LANE, 1), F32).at[:2 * h, 0].set(ssd_a_log[l].reshape(-1))
        lb = jnp.zeros((8, d), F32).at[:2].set(lb_all[l])
        proj, lam, kk, dt_t, da_t = _inproj_call(dims, x, mod_l, norm_mix_w[l].reshape(1, d), w_a, w_b, ssd_conv_w[l],
                                                 ssd_conv_b[l].reshape(1, conv_dim), wdt, dtb, alog, lb)
        dskip = jnp.broadcast_to(jnp.repeat(ssd_d[l], SSD_P)[:, None], (ds_, SSD_Q))
        ssd_nw = jnp.broadcast_to(ssd_norm_w[l][:, None], (ds_, SSD_Q))
        y_ctx, s_ssm = _ssd_call(dims, proj, dt_t, da_t, dskip, ssd_nw, None, dims.nb, dims.nl, 0, l, s_ssm)
        y_lat, _ = _ssd_call(dims, proj, dt_t, da_t, dskip, ssd_nw,
                             state_ssm[:, l].reshape(dims.db, 2, ds_, SSD_N), dims.db, dims.dl, dims.t_ctx)
        hg_nw = hgrn_norm_w[l].reshape(1, d)
        o_ctx, s_hg = _hgrn_call(dims, proj, lam, kk, hg_nw, None, dims.nb, dims.nl, 0, l, s_hg)
        o_lat, _ = _hgrn_call(dims, proj, lam, kk, hg_nw, state_hgrn[:, l], dims.db, dims.dl, dims.t_ctx)
        x = _merge_call(dims, y_ctx, y_lat, o_ctx, o_lat, proj, x, mod_l,
                        w_ssd_out[l].astype(BF16), w_hgrn_out[l].astype(BF16), w_out[l].astype(BF16))
        x = _ffn_call(dims, x, mod_l, norm_ffn_w[l].reshape(1, d),
                      ffn_w1[l].astype(BF16), ffn_w3[l].astype(BF16), ffn_w2[l].astype(BF16),
                      norm_final_w.reshape(1, d) if l == dims.depth - 1 else None)
    y_ctx, y_lat = x
    return (y_ctx.reshape(dims.nb, dims.nl, d), y_lat.reshape(dims.db, dims.dl, d),
            s_ssm.reshape(dims.nb, dims.depth, 2, h, SSD_P, SSD_N), s_hg)


def kernel(x_prompt, x_sample, c, state_ssm, state_hgrn, c_ctx, norm_mix_w, norm_ffn_w, ada_w, ada_b, w_in,
           ssd_conv_w, ssd_conv_b, ssd_dt_bias, ssd_a_log, ssd_d, ssd_norm_w, hgrn_lower_bounds, hgrn_norm_w,
           w_ssd_out, w_hgrn_out, w_out, ffn_w1, ffn_w3, ffn_w2, norm_final_w):
    dims = Dims(d=x_prompt.shape[2], nb=x_prompt.shape[0], nl=x_prompt.shape[1], db=x_sample.shape[0],
                dl=x_sample.shape[1], depth=w_in.shape[0], ffn=ffn_w1.shape[2])
    return _forward(dims, x_prompt, x_sample, c, state_ssm, state_hgrn, c_ctx, norm_mix_w, norm_ffn_w, ada_w, ada_b,
                    w_in, ssd_conv_w, ssd_conv_b, ssd_dt_bias, ssd_a_log, ssd_d, ssd_norm_w, hgrn_lower_bounds,
                    hgrn_norm_w, w_ssd_out, w_hgrn_out, w_out, ffn_w1, ffn_w3, ffn_w2, norm_final_w)
```

```python
import functools
from typing import NamedTuple

import jax
import jax.numpy as jnp
from jax import lax
from jax.experimental import pallas as pl
from jax.experimental.pallas import tpu as pltpu

F32 = jnp.float32
BF16 = jnp.bfloat16

LANE = 128
MXU_COLS = 256
VMEM_BYTES = 64 * 1024 * 1024
RMS_EPS = 1e-6
D_CONV = 5
GRID_W = 64
SSD_P = 64
SSD_N = 128
SSD_Q = 128
SSD_R = 4
HG_K = 128
HG_V = 128
HG_C = 128
HG_LEVELS = (64, 32, 16)
HG_DIAG = 16
HF_BLOCKS = 2
MOD_ROWS = 8


class Dims(NamedTuple):
    d: int
    nb: int
    nl: int
    db: int
    dl: int
    depth: int
    ffn: int

    @property
    def t_ctx(self):
        return self.nb * self.nl

    @property
    def t_lat(self):
        return self.db * self.dl

    @property
    def t(self):
        return self.t_ctx + self.t_lat

    @property
    def d_ssd(self):
        return 2 * self.d

    @property
    def h_ssd(self):
        return self.d_ssd // SSD_P

    @property
    def groups(self):
        return self.h_ssd // SSD_R

    @property
    def hh(self):
        return self.d // HG_K

    @property
    def row_tile(self):
        tm = min(1024, self.dl, self.t_ctx)
        assert self.dl % tm == 0 and self.t_ctx % tm == 0 and tm % self.nl == 0 and tm % GRID_W == 0
        return tm


def _cparams(semantics, vmem_bytes):
    return pltpu.CompilerParams(dimension_semantics=semantics, vmem_limit_bytes=min(vmem_bytes, VMEM_BYTES - (4 << 20)))


def _dot(a, b):
    return jnp.dot(a, b, preferred_element_type=F32)


def _dot_nt(a, b):
    return lax.dot_general(a, b, (((1,), (1,)), ((), ())), preferred_element_type=F32)


def _dot_tn(a, b):
    return lax.dot_general(a, b, (((0,), (0,)), ((), ())), preferred_element_type=F32)


def _split3(x):
    hi = x.astype(BF16)
    r = x - hi.astype(F32)
    mid = r.astype(BF16)
    lo = (r - mid.astype(F32)).astype(BF16)
    return hi, mid, lo


def _sum01_right(x, m01):
    hi, mid, lo = _split3(x)
    return _dot(hi, m01) + _dot(mid, m01) + _dot(lo, m01)


def _sigmoid(x):
    return 0.5 * jnp.tanh(0.5 * x) + 0.5


def _silu(x):
    return x * _sigmoid(x)


def _softplus(x):
    return jnp.maximum(x, 0.0) + jnp.log1p(jnp.exp(-jnp.abs(x)))


def _mod_row(i, tm, dims):
    nctx = dims.t_ctx // tm
    return jnp.where(i < nctx, 0, 1 + ((i - nctx) * tm) // dims.dl)


def _mod_kernel(c_ref, w_ref, b_ref, o_ref):
    s = _silu(c_ref[...])
    o_ref[0] = _dot(s.astype(BF16), w_ref[0].astype(BF16)) + b_ref[0]


def _mod_call(dims, cvec, ada_w, ada_b):
    d, n = dims.d, 6 * dims.d
    tn = n // 4
    return pl.pallas_call(
        _mod_kernel,
        grid=(dims.depth, n // tn),
        in_specs=[pl.BlockSpec((MOD_ROWS, d), lambda l, j: (0, 0)),
                  pl.BlockSpec((1, d, tn), lambda l, j: (l, 0, j)),
                  pl.BlockSpec((1, 1, tn), lambda l, j: (l, 0, j))],
        out_specs=pl.BlockSpec((1, MOD_ROWS, tn), lambda l, j: (l, 0, j)),
        out_shape=jax.ShapeDtypeStruct((dims.depth, MOD_ROWS, n), F32),
        compiler_params=_cparams(("arbitrary", "arbitrary"), 2 * d * tn * 4 + (8 << 20)),
        name="adaln_mod",
    )(cvec, ada_w, ada_b.reshape(dims.depth, 1, n))


def _inproj_kernel(dims, tm, x_ref, mod_ref, nw_ref, wa_ref, wb_ref, cw_ref, cb_ref, wdt_ref, dtb_ref, alog_ref, lb_ref,
                   proj_ref, lam_ref, kk_ref, dt_ref, da_ref, h_s):
    d = dims.d
    i = pl.program_id(0)
    j = pl.program_id(1)
    n_conv = (dims.d_ssd + 2 * dims.groups * SSD_N) // d
    n_a = n_conv + dims.d_ssd // d
    n_bf = pl.num_programs(1) - HF_BLOCKS

    @pl.when(j == 0)
    def _():
        x = x_ref[...]
        y = x * lax.rsqrt(jnp.mean(x * x, axis=-1, keepdims=True) + RMS_EPS) * nw_ref[...]
        mod = mod_ref[0]
        hb = (y * (1.0 + mod[:, d:2 * d]) + mod[:, 0:d]).astype(BF16)
        h_s[...] = hb
        wdt_t = wdt_ref[...].T.astype(BF16)
        dt = _softplus(_dot_nt(wdt_t, hb) + dtb_ref[...])
        da = dt * -jnp.exp(alog_ref[...])
        for k in range(tm // SSD_Q):
            dt_ref[k] = dt[:, k * SSD_Q:(k + 1) * SSD_Q]
            da_ref[k] = da[:, k * SSD_Q:(k + 1) * SSD_Q]

    @pl.when(j >= n_bf)
    def _():
        lam, kk = _hgrn_gates(_dot(h_s[...], wb_ref[...]), lb_ref[pl.ds(j - n_bf, 1), :])
        lam_ref[...] = lam
        kk_ref[...] = kk.astype(BF16)

    @pl.when((j >= n_conv) & (j < n_a))
    def _():
        proj_ref[...] = _dot(h_s[...], wa_ref[...]).astype(BF16)

    @pl.when((j >= n_a) & (j < n_bf))
    def _():
        proj_ref[...] = _dot(h_s[...], wb_ref[...]).astype(BF16)

    def conv_silu(seg):
        ns, half, sub = tm // seg, D_CONV // 2, 8
        n8 = ns * sub
        r8 = lax.broadcasted_iota(jnp.int32, (n8, LANE), 0) & (sub - 1)
        res = None
        for sl in range(d // LANE):
            cols = slice(sl * LANE, (sl + 1) * LANE)
            if sl % (MXU_COLS // LANE) == 0:
                res = _dot(h_s[...], wa_ref[:, sl * LANE:sl * LANE + MXU_COLS])
            u = res[:, (sl * LANE) % MXU_COLS:(sl * LANE) % MXU_COLS + LANE]
            w = [cw_ref[k:k + 1, cols] for k in range(D_CONV)]
            acc = cb_ref[:, cols] + w[half] * u
            for k in range(D_CONV):
                if k != half:
                    acc = acc + w[k] * pltpu.roll(u, (half - k) % tm, 0)
            u3 = u.reshape(ns, seg, LANE)
            a3 = acc.reshape(ns, seg, LANE)
            edge_lo = u3[:, 0:sub, :].reshape(n8, LANE)
            edge_hi = u3[:, seg - sub:seg, :].reshape(n8, LANE)
            leak_lo = (w[0] * jnp.where(r8 < 2, pltpu.roll(edge_hi, 2, 0), 0.0)
                       + w[1] * jnp.where(r8 < 1, pltpu.roll(edge_hi, 1, 0), 0.0))
            leak_hi = (w[4] * jnp.where(r8 >= sub - 2, pltpu.roll(edge_lo, n8 - 2, 0), 0.0)
                       + w[3] * jnp.where(r8 >= sub - 1, pltpu.roll(edge_lo, n8 - 1, 0), 0.0))
            a3 = jnp.concatenate([a3[:, 0:sub, :] - leak_lo.reshape(ns, sub, LANE),
                                  a3[:, sub:seg - sub, :],
                                  a3[:, seg - sub:seg, :] - leak_hi.reshape(ns, sub, LANE)], axis=1)
            proj_ref[:, cols] = _silu(a3.reshape(tm, LANE)).astype(BF16)

    is_ctx = i < dims.t_ctx // tm

    @pl.when((j < n_conv) & is_ctx)
    def _():
        conv_silu(dims.nl)

    @pl.when((j < n_conv) & jnp.logical_not(is_ctx))
    def _():
        conv_silu(GRID_W)


def _inproj_call(dims, x, mod_l, nw, wa, wb, cw, cb, wdt, dtb, alog, lb):
    d, t, tm = dims.d, dims.t, dims.row_tile
    n_conv = (dims.d_ssd + 2 * dims.groups * SSD_N) // d
    n_a = wa.shape[1] // d
    ncol = n_a + wb.shape[1] // d
    nchunk = t // SSD_Q
    n_bf = ncol - HF_BLOCKS

    def wb_block(j):
        return jnp.where(j <= n_a, 0, jnp.where(j < n_bf, j - n_a + HF_BLOCKS, j - n_bf + 1))

    out_shapes = (jax.ShapeDtypeStruct((t, n_bf * d), BF16),
                  jax.ShapeDtypeStruct((t, HF_BLOCKS * d), F32),
                  jax.ShapeDtypeStruct((t, HF_BLOCKS * d), BF16),
                  jax.ShapeDtypeStruct((nchunk, LANE, SSD_Q), F32),
                  jax.ShapeDtypeStruct((nchunk, LANE, SSD_Q), F32))
    kc = tm // SSD_Q
    gate_spec = pl.BlockSpec((tm, d), lambda i, j: (i, jnp.maximum(j - n_bf, 0)))
    return pl.pallas_call(
        functools.partial(_inproj_kernel, dims, tm),
        grid=(t // tm, ncol),
        in_specs=[pl.BlockSpec((tm, d), lambda i, j: (i, 0)),
                  pl.BlockSpec((1, 1, 6 * d), lambda i, j: (_mod_row(i, tm, dims), 0, 0)),
                  pl.BlockSpec((1, d), lambda i, j: (0, 0)),
                  pl.BlockSpec((d, d), lambda i, j: (0, jnp.minimum(j, n_a - 1))),
                  pl.BlockSpec((d, d), lambda i, j: (0, wb_block(j))),
                  pl.BlockSpec((D_CONV, d), lambda i, j: (0, jnp.minimum(j, n_conv - 1))),
                  pl.BlockSpec((1, d), lambda i, j: (0, jnp.minimum(j, n_conv - 1))),
                  pl.BlockSpec((d, LANE), lambda i, j: (0, 0)),
                  pl.BlockSpec((LANE, 1), lambda i, j: (0, 0)),
                  pl.BlockSpec((LANE, 1), lambda i, j: (0, 0)),
                  pl.BlockSpec((8, d), lambda i, j: (0, 0))],
        out_specs=(pl.BlockSpec((tm, d), lambda i, j: (i, jnp.minimum(j, n_bf - 1))),
                   gate_spec, gate_spec,
                   pl.BlockSpec((kc, LANE, SSD_Q), lambda i, j: (i, 0, 0)),
                   pl.BlockSpec((kc, LANE, SSD_Q), lambda i, j: (i, 0, 0))),
        out_shape=out_shapes,
        scratch_shapes=[pltpu.VMEM((tm, d), BF16)],
        compiler_params=_cparams(("arbitrary", "arbitrary"), 8 * tm * d * 4 + 4 * d * d * 2 + tm * d * 2 + (8 << 20)),
        name="inproj",
    )(x, mod_l, nw, wa, wb, cw, cb, wdt, dtb, alog, lb)


def _ssd_kernel(dims, nchunk, has_init, has_prev, *refs):
    x_ref, b_ref, c_ref, z_ref, dt_ref, da_ref, dsk_ref, nw_ref = refs[:8]
    s0_ref = refs[8] if has_init else None
    (y_ref, sfin_ref, st, yf_s, xt_s, yb_s, bg_s, cg_s, cs_s, ecs_s, wend_s, dec_s,
     dtr_s, mt_s) = refs[8 + has_init + has_prev:]
    h, g_n = dims.h_ssd, dims.groups
    q = SSD_Q
    s = pl.program_id(1)
    is_bwd = s >= nchunk
    ck = jnp.where(is_bwd, 2 * nchunk - 1 - s, s)
    first = (s == 0) | (s == nchunk)
    last = (s == nchunk - 1) | (s == 2 * nchunk - 1)

    @pl.when(first)
    def _():
        st[...] = s0_ref[0, 0] if has_init else jnp.zeros(st.shape, F32)

    row0 = pl.multiple_of(jnp.where(is_bwd, h, 0), 8)
    dtr = dt_ref[0, pl.ds(row0, h), :]
    dar = da_ref[0, pl.ds(row0, h), :]
    jj = lax.broadcasted_iota(jnp.int32, (q, q), 0)
    ii = lax.broadcasted_iota(jnp.int32, (q, q), 1)
    seen = jnp.where(is_bwd, jj - ii, ii - jj) >= 0
    cs = _sum01_right(dar, seen.astype(BF16))
    tot = _sum01_right(dar, jnp.ones((q, q), BF16))
    cs_s[...] = cs
    ecs_s[...] = jnp.exp(cs)
    wend_s[...] = jnp.exp(tot - cs) * dtr
    dec_s[...] = jnp.exp(tot)
    dtr_s[...] = dtr
    xt_s[...] = x_ref[...].astype(F32).T
    for g in range(g_n):
        bg_s[g] = b_ref[:, g * SSD_N:(g + 1) * SSD_N]
        cg_s[g] = c_ref[:, g * SSD_N:(g + 1) * SSD_N]

    gp = SSD_R * SSD_P

    def decay_body(g, carry):
        cbt = _dot_nt(bg_s[g], cg_s[g])
        for r in range(SSD_R):
            hd = g * SSD_R + r
            cs_i = cs_s[pl.ds(hd, 1), :]
            cs_j = jnp.broadcast_to(cs_i, (q, q)).T
            mt_s[hd] = (cbt * jnp.exp(jnp.where(seen, cs_i - cs_j, -jnp.inf))).astype(BF16)
        return carry

    lax.fori_loop(0, g_n, decay_body, 0, unroll=True)

    def group_body(g, carry):
        bg = bg_s[g]
        cg = cg_s[g]
        r0 = pl.multiple_of(g * gp, gp)
        sg = st[pl.ds(r0, gp), :]
        sc = _dot_nt(sg.astype(BF16), cg)
        xg = xt_s[pl.ds(r0, gp), :]
        xw, decs = [], []
        for r in range(SSD_R):
            hd = g * SSD_R + r
            xh = xg[r * SSD_P:(r + 1) * SSD_P]
            mt = mt_s[hd]
            xdt = (xh * dtr_s[pl.ds(hd, 1), :]).astype(BF16)
            yh = _dot(xdt, mt) + sc[r * SSD_P:(r + 1) * SSD_P] * ecs_s[pl.ds(hd, 1), :]
            yb_s[pl.ds(pl.multiple_of(r0 + r * SSD_P, SSD_P), SSD_P), :] = yh
            xw.append((xh * wend_s[pl.ds(hd, 1), :]).astype(BF16))
            decs.append(jnp.broadcast_to(dec_s[pl.ds(hd, 1), :], (SSD_P, SSD_N)))
        upd = _dot(jnp.concatenate(xw, axis=0), bg)
        st[pl.ds(r0, gp), :] = sg * jnp.concatenate(decs, axis=0) + upd
        return carry

    lax.fori_loop(0, g_n, group_body, 0, unroll=True)

    @pl.when(jnp.logical_not(is_bwd))
    def _():
        yf_s[ck] = yb_s[...]

    @pl.when(is_bwd)
    def _():
        yt = yb_s[...] + yf_s[ck] + dsk_ref[...] * xt_s[...]
        yt = yt * _silu(z_ref[...].astype(F32).T)
        y3 = yt.reshape(g_n, gp, q)
        yn = y3 * lax.rsqrt(jnp.mean(y3 * y3, axis=1, keepdims=True) + RMS_EPS)
        y_ref[...] = (yn.reshape(dims.d_ssd, q) * nw_ref[...]).T.astype(BF16)

    @pl.when(last)
    def _():
        sfin_ref[0, 0, 0] = st[...]


def _ssd_call(dims, proj, dt_t, da_t, dskip, nw, s0, n_seq, seq_len, tok_off, layer=None, fin_prev=None):
    d, ds_ = dims.d, dims.d_ssd
    n_slab, slab = (1, 0) if layer is None else (dims.depth, layer)
    q = SSD_Q
    nchunk = seq_len // q
    off = tok_off // q
    has_init = s0 is not None

    def ck(s):
        return jnp.where(s < nchunk, s, 2 * nchunk - 1 - s)

    def tok(b, s):
        return off + b * nchunk + ck(s)

    def ytok(b, s):
        return b * nchunk + jnp.where(s < nchunk, nchunk - 1, 2 * nchunk - 1 - s)

    in_specs = [pl.BlockSpec((q, ds_), lambda b, s: (tok(b, s), 0)),
                pl.BlockSpec((q, d), lambda b, s: (tok(b, s), ds_ // d)),
                pl.BlockSpec((q, d), lambda b, s: (tok(b, s), ds_ // d + 1)),
                pl.BlockSpec((q, ds_), lambda b, s: (off + ytok(b, s), (ds_ + 2 * d) // ds_)),
                pl.BlockSpec((1, LANE, q), lambda b, s: (tok(b, s), 0, 0)),
                pl.BlockSpec((1, LANE, q), lambda b, s: (tok(b, s), 0, 0)),
                pl.BlockSpec((ds_, q), lambda b, s: (0, 0)),
                pl.BlockSpec((ds_, q), lambda b, s: (0, 0))]
    args = [proj, proj, proj, proj, dt_t, da_t, dskip, nw]
    if has_init:
        in_specs.append(pl.BlockSpec((1, 1, ds_, SSD_N), lambda b, s: (b, s // nchunk, 0, 0)))
        args.append(s0)
    aliases = {}
    if fin_prev is not None:
        in_specs.append(pl.BlockSpec(memory_space=pl.ANY))
        aliases = {len(args): 1}
        args.append(fin_prev)
    h = dims.h_ssd
    scratch = [pltpu.VMEM((ds_, SSD_N), F32),
               pltpu.VMEM((nchunk, ds_, q), F32),
               pltpu.VMEM((ds_, q), F32),
               pltpu.VMEM((ds_, q), F32),
               pltpu.VMEM((dims.groups, q, SSD_N), BF16),
               pltpu.VMEM((dims.groups, q, SSD_N), BF16),
               pltpu.VMEM((h, q), F32), pltpu.VMEM((h, q), F32), pltpu.VMEM((h, q), F32),
               pltpu.VMEM((h, q), F32), pltpu.VMEM((h, q), F32), pltpu.VMEM((h, q, q), BF16)]
    vmem = (2 * (3 * q * ds_ + 2 * q * d + 2 * LANE * q + 2 * ds_ * q + 2 * ds_ * SSD_N) * 4
            + (nchunk + 3) * ds_ * q * 4 + (12 << 20))
    return pl.pallas_call(
        functools.partial(_ssd_kernel, dims, nchunk, has_init, fin_prev is not None),
        grid=(n_seq, 2 * nchunk),
        in_specs=in_specs,
        out_specs=(pl.BlockSpec((q, ds_), lambda b, s: (ytok(b, s), 0)),
                   pl.BlockSpec((1, 1, 1, ds_, SSD_N), lambda b, s: (b, slab, s // nchunk, 0, 0))),
        out_shape=(jax.ShapeDtypeStruct((n_seq * seq_len, ds_), BF16),
                   jax.ShapeDtypeStruct((n_seq, n_slab, 2, ds_, SSD_N), F32)),
        input_output_aliases=aliases,
        scratch_shapes=scratch,
        compiler_params=_cparams(("arbitrary", "arbitrary"), vmem),
        name="ssd_scan",
    )(*args)


def _hgrn_gates(u, lb):
    t = jnp.exp(-jnp.abs(u))
    pos = u >= 0.0
    den = 1.0 + t
    num = jnp.where(pos, 1.0 + lb * t, lb + t)
    log_num = jnp.where(pos | (jnp.broadcast_to(lb, u.shape) > 0.0), jnp.log(num), u)
    return log_num - jnp.log(den), (1.0 - lb) * jnp.where(pos, t, 1.0) / den


def _hg_dir_operands(lam, kk, qq, blockmat, fwd):
    c, w = lam.shape
    bs, nb = HG_DIAG, lam.shape[0] // HG_DIAG
    lam_hi = lam.astype(BF16)
    lam_lo = (lam - lam_hi.astype(F32)).astype(BF16)
    p = _dot(blockmat, lam_hi) + _dot(blockmat, lam_lo)
    edge, mid = (bs - 1, bs // 2 - 1) if fwd else (0, bs // 2)
    tot = [p[b * bs + edge:b * bs + edge + 1] for b in range(nb)]
    p_mid = [p[b * bs + mid:b * bs + mid + 1] for b in range(nb)]

    def spread(rows):
        return jnp.concatenate([jnp.broadcast_to(r, (bs, w)) for r in rows], axis=0)

    s = spread(tot) - p
    qp = qq * jnp.exp(p)
    ks = kk * jnp.exp(s)
    zero = jnp.zeros((1, w), F32)

    def run_sums(lo_b, hi_b):
        before, acc = {}, zero
        order = range(lo_b, hi_b) if fwd else range(hi_b - 1, lo_b - 1, -1)
        for b in order:
            before[b] = acc
            acc = acc + tot[b]
        after, acc = {}, zero
        for b in reversed(order):
            after[b] = acc
            acc = acc + tot[b]
        return before, after, acc

    def scaled(x, logs):
        zblk = jnp.zeros((bs, w), BF16)
        parts = []
        for b in range(nb):
            xb = x[b * bs:(b + 1) * bs]
            if logs[b] is None:
                parts.append(zblk)
            elif logs[b] is zero:
                parts.append(xb.astype(BF16))
            else:
                parts.append((xb * jnp.exp(logs[b])).astype(BF16))
        return jnp.concatenate(parts, axis=0)

    level_q, level_k = [], []
    for hf in HG_LEVELS:
        nh = hf // bs
        ql, kl = [None] * nb, [None] * nb
        for par in range(nb // (2 * nh)):
            halves = (par * 2 * nh, par * 2 * nh + nh), (par * 2 * nh + nh, (par + 1) * 2 * nh)
            key_half, query_half = halves if fwd else halves[::-1]
            _, after, _ = run_sums(*key_half)
            before, _, _ = run_sums(*query_half)
            for b in range(*key_half):
                kl[b] = after[b]
            for b in range(*query_half):
                ql[b] = before[b]
        level_q.append(scaled(qp, ql))
        level_k.append(scaled(ks, kl))
    s_mid = [t - pm for t, pm in zip(tot, p_mid)]
    diag_q = (qp * spread([jnp.exp(-pm) for pm in p_mid])).astype(BF16)
    diag_k = (ks * spread([jnp.exp(-sm) for sm in s_mid])).astype(BF16)
    before, after, total = run_sums(0, nb)
    state_q = scaled(qp, [before[b] for b in range(nb)])
    state_k = scaled(ks, [after[b] for b in range(nb)])
    return level_q, level_k, diag_q, diag_k, state_q, state_k, jnp.exp(total)


def _hgrn_kernel(dims, nblk, tb, has_init, has_prev, *refs):
    q_ref, lamf_ref, lamb_ref, kf_ref, kb_ref, v_ref, g_ref, nw_ref = refs[:8]
    s0_ref = refs[8] if has_init else None
    o_ref, sfin_ref, st, o_s, qb_s, kb_s, decb_s = refs[8 + has_init + has_prev:]
    hh = dims.hh
    c = HG_C
    nck = tb // c
    single = nblk == 1
    s = pl.program_id(1)
    is_bwd = s >= nblk
    blk = 0 if single else jnp.where(is_bwd, 2 * nblk - 1 - s, s)

    def init_state(direction):
        for hd in range(hh):
            st[hd] = s0_ref[0, direction, hd].T if has_init else jnp.zeros((HG_V, HG_K), F32)

    def save_state(direction):
        for hd in range(hh):
            sfin_ref[0, 0, direction, hd] = st[hd].T

    ti = lax.broadcasted_iota(jnp.int32, (c, c), 0)
    tj = lax.broadcasted_iota(jnp.int32, (c, c), 1)
    same_diag = (ti // HG_DIAG) == (tj // HG_DIAG)
    diag_f = same_diag & (tj <= ti)
    diag_b = same_diag & (tj >= ti)
    mask_df, mask_db = diag_f.astype(F32), diag_b.astype(F32)
    parent = {hf: ((ti // (2 * hf)) == (tj // (2 * hf))).astype(F32) for hf in HG_LEVELS if 2 * hf < c}

    def heads(x):
        return [x[:, hd * HG_K:(hd + 1) * HG_K] for hd in range(hh)]

    def state_step(hd, qs_h, ke_h, v_h, dec_h):
        sh = st[hd]
        off = _dot_nt(qs_h, sh.astype(BF16))
        st[hd] = sh * dec_h + _dot_tn(v_h, ke_h)
        return off

    def fwd_chunk(idx, carry):
        r0 = pl.multiple_of(idx * c, c)
        rows = pl.ds(r0, c)
        g0 = pl.multiple_of(blk * tb + r0, c)
        qq = _silu(q_ref[rows, :].astype(F32))
        vh = heads(v_ref[rows, :])
        lq_f, lk_f, dq_f, dk_f, sq_f, sk_f, dec_f = _hg_dir_operands(
            lamf_ref[rows, :], kf_ref[rows, :].astype(F32), qq, diag_f.astype(BF16), True)
        lq_b, lk_b, dq_b, dk_b, sq_b, sk_b, dec_b = _hg_dir_operands(
            lamb_ref[rows, :], kb_ref[rows, :].astype(F32), qq, diag_b.astype(BF16), False)
        qb_s[pl.ds(g0, c), :] = sq_b
        kb_s[pl.ds(g0, c), :] = sk_b
        decb_s[g0 // c] = dec_b
        lq_f, lk_f, lq_b, lk_b = ([heads(x) for x in xs] for xs in (lq_f, lk_f, lq_b, lk_b))
        dq_f, dk_f, dq_b, dk_b, sq_f, sk_f, dec_f = (heads(x) for x in (dq_f, dk_f, dq_b, dk_b, sq_f, sk_f, dec_f))
        outs = []
        for hd in range(hh):
            att = _dot_nt(dq_f[hd], dk_f[hd]) * mask_df + _dot_nt(dq_b[hd], dk_b[hd]) * mask_db
            for li, hf in enumerate(HG_LEVELS):
                p = _dot_nt(jnp.concatenate([lq_f[li][hd], lq_b[li][hd]], axis=1),
                            jnp.concatenate([lk_f[li][hd], lk_b[li][hd]], axis=1))
                att = att + (p * parent[hf] if hf in parent else p)
            o_h = _dot(att.astype(BF16), vh[hd])
            outs.append(o_h + state_step(hd, sq_f[hd], sk_f[hd], vh[hd], dec_f[hd]))
        o_s[pl.ds(g0, c), :] = jnp.concatenate(outs, axis=1)
        return carry

    def bwd_chunk(idx, carry):
        r0 = pl.multiple_of((nck - 1 - idx) * c, c)
        rows = pl.ds(r0, c)
        g0 = pl.multiple_of(blk * tb + r0, c)
        vh = heads(v_ref[rows, :])
        qs = heads(qb_s[pl.ds(g0, c), :])
        ke = heads(kb_s[pl.ds(g0, c), :])
        dec = heads(decb_s[g0 // c])
        o_prev = heads(o_s[pl.ds(g0, c), :])
        gate = heads(_silu(g_ref[rows, :].astype(F32)))
        nw = heads(nw_ref[...])
        outs = []
        for hd in range(hh):
            o_h = o_prev[hd] + state_step(hd, qs[hd], ke[hd], vh[hd], dec[hd])
            o_h = o_h * lax.rsqrt(jnp.mean(o_h * o_h, axis=-1, keepdims=True) + RMS_EPS) * nw[hd]
            outs.append(o_h * gate[hd])
        o_ref[rows, :] = jnp.concatenate(outs, axis=1).astype(BF16)
        return carry

    if single:
        init_state(0)
        lax.fori_loop(0, nck, fwd_chunk, 0)
        save_state(0)
        init_state(1)
        lax.fori_loop(0, nck, bwd_chunk, 0)
        save_state(1)
        return

    @pl.when(s == 0)
    def _():
        init_state(0)

    @pl.when(s == nblk)
    def _():
        init_state(1)

    @pl.when(jnp.logical_not(is_bwd))
    def _():
        lax.fori_loop(0, nck, fwd_chunk, 0)

    @pl.when(is_bwd)
    def _():
        lax.fori_loop(0, nck, bwd_chunk, 0)

    @pl.when(s == nblk - 1)
    def _():
        save_state(0)

    @pl.when(s == 2 * nblk - 1)
    def _():
        save_state(1)


def _hgrn_call(dims, proj, lam, kk, nw, s0, n_seq, seq_len, tok_off, layer=None, fin_prev=None):
    d, hh = dims.d, dims.hh
    n_slab, slab = (1, 0) if layer is None else (dims.depth, layer)
    tb = min(512, seq_len)
    nblk = seq_len // tb
    off = tok_off // tb
    has_init = s0 is not None
    col0 = (dims.d_ssd + 2 * d + dims.d_ssd) // d

    def tok(b, s):
        return off + b * nblk + jnp.where(s < nblk, s, 2 * nblk - 1 - s)

    def tok_pass1(b, s):
        return off + b * nblk + jnp.minimum(s, nblk - 1)

    def otok(b, s):
        return b * nblk + jnp.where(s < nblk, nblk - 1, 2 * nblk - 1 - s)

    def tok_pass2(b, s):
        return off + otok(b, s)

    srcs = ((proj, col0, tok_pass1), (lam, 0, tok_pass1), (lam, 1, tok_pass1), (kk, 0, tok_pass1), (kk, 1, tok_pass1),
            (proj, col0 + 1, tok), (proj, col0 + 2, tok_pass2))
    in_specs = [pl.BlockSpec((tb, d), functools.partial(lambda cc, tk, b, s: (tk(b, s), cc), col, tk))
                for _, col, tk in srcs]
    in_specs += [pl.BlockSpec((1, d), lambda b, s: (0, 0))]
    args = [a for a, _, _ in srcs] + [nw]
    if has_init:
        in_specs.append(pl.BlockSpec((1, 2, hh, HG_K, HG_V), lambda b, s: (b, 0, 0, 0, 0)))
        args.append(s0)
    aliases = {}
    if fin_prev is not None:
        in_specs.append(pl.BlockSpec(memory_space=pl.ANY))
        aliases = {len(args): 1}
        args.append(fin_prev)
    vmem = 2 * 7 * tb * d * 4 + 2 * seq_len * d * 4 + 4 * hh * HG_K * HG_V * 4 + (16 << 20)
    return pl.pallas_call(
        functools.partial(_hgrn_kernel, dims, nblk, tb, has_init, fin_prev is not None),
        grid=(n_seq, 1 if nblk == 1 else 2 * nblk),
        in_specs=in_specs,
        out_specs=(pl.BlockSpec((tb, d), lambda b, s: (otok(b, s), 0)),
                   pl.BlockSpec((1, 1, 2, hh, HG_K, HG_V), lambda b, s: (b, slab, 0, 0, 0, 0))),
        out_shape=(jax.ShapeDtypeStruct((n_seq * seq_len, d), BF16),
                   jax.ShapeDtypeStruct((n_seq, n_slab, 2, hh, HG_K, HG_V), F32)),
        input_output_aliases=aliases,
        scratch_shapes=[pltpu.VMEM((hh, HG_V, HG_K), F32),
                        pltpu.VMEM((seq_len, d), F32),
                        pltpu.VMEM((seq_len, d), BF16),
                        pltpu.VMEM((seq_len, d), BF16),
                        pltpu.VMEM((seq_len // HG_C, 1, d), F32)],
        compiler_params=_cparams(("arbitrary", "arbitrary"), vmem),
        name="hgrn_scan",
    )(*args)


def _merge_kernel(dims, tm, yc_ref, yl_ref, oc_ref, ol_ref, g1_ref, g2_ref, x_ref, mod_ref,
                  wso_ref, who_ref, wo_ref, out_ref):
    d = dims.d
    is_ctx = pl.program_id(0) < dims.t_ctx // tm
    y = jnp.where(is_ctx, yc_ref[...], yl_ref[...])
    o = jnp.where(is_ctx, oc_ref[...], ol_ref[...])
    y_ssd = _dot(y, wso_ref[...])
    y_hg = _dot(o, who_ref[...])
    merged = _sigmoid(g1_ref[...].astype(F32)) * y_ssd + _sigmoid(g2_ref[...].astype(F32)) * y_hg
    mix = _dot(merged.astype(BF16), wo_ref[...])
    out_ref[...] = x_ref[...] + mod_ref[0][:, 2 * d:3 * d] * mix


def _merge_call(dims, y_ctx, y_lat, o_ctx, o_lat, proj, x, mod_l, wso, who, wo):
    d, ds_, t = dims.d, dims.d_ssd, dims.t
    tm = min(512, dims.row_tile)
    nctx = dims.t_ctx // tm
    gcol = proj.shape[1] // d - 2

    def ctx_blk(i):
        return jnp.minimum(i, nctx - 1)

    def lat_blk(i):
        return jnp.maximum(i - nctx, 0)

    const = lambda i: (0, 0)
    return pl.pallas_call(
        functools.partial(_merge_kernel, dims, tm),
        grid=(t // tm,),
        in_specs=[pl.BlockSpec((tm, ds_), lambda i: (ctx_blk(i), 0)),
                  pl.BlockSpec((tm, ds_), lambda i: (lat_blk(i), 0)),
                  pl.BlockSpec((tm, d), lambda i: (ctx_blk(i), 0)),
                  pl.BlockSpec((tm, d), lambda i: (lat_blk(i), 0)),
                  pl.BlockSpec((tm, d), lambda i: (i, gcol)),
                  pl.BlockSpec((tm, d), lambda i: (i, gcol + 1)),
                  pl.BlockSpec((tm, d), lambda i: (i, 0)),
                  pl.BlockSpec((1, 1, 6 * d), lambda i: (_mod_row(i, tm, dims), 0, 0)),
                  pl.BlockSpec((ds_, d), const), pl.BlockSpec((d, d), const), pl.BlockSpec((d, d), const)],
        out_specs=pl.BlockSpec((tm, d), lambda i: (i, 0)),
        out_shape=jax.ShapeDtypeStruct((t, d), F32),
        compiler_params=_cparams(("arbitrary",), 2 * (2 * tm * ds_ + 6 * tm * d) * 4 + 2 * (ds_ + 2 * d) * d * 2 + (12 << 20)),
        name="merge_outproj",
    )(y_ctx, y_lat, o_ctx, o_lat, proj, proj, x, mod_l, wso, who, wo)


def _ffn_kernel(dims, tm, final_norm, x_ref, mod_ref, nw_ref, w1_ref, w3_ref, w2_ref, *rest):
    d = dims.d
    if final_norm:
        fw_ref, out_ctx_ref, out_lat_ref = rest
    else:
        (out_ref,) = rest

    x = x_ref[...]
    y = x * lax.rsqrt(jnp.mean(x * x, axis=-1, keepdims=True) + RMS_EPS) * nw_ref[...]
    mod = mod_ref[0]
    hb = (y * (1.0 + mod[:, 4 * d:5 * d]) + mod[:, 3 * d:4 * d]).astype(BF16)
    acc = None
    for lo in range(0, dims.ffn, MXU_COLS):
        hi = min(lo + MXU_COLS, dims.ffn)
        act = _silu(_dot(hb, w1_ref[:, lo:hi])) * _dot(hb, w3_ref[:, lo:hi])
        part = _dot(act.astype(BF16), w2_ref[lo:hi, :])
        acc = part if acc is None else acc + part
    out = x + mod[:, 5 * d:6 * d] * acc
    if not final_norm:
        out_ref[...] = out
        return
    out = out * lax.rsqrt(jnp.mean(out * out, axis=-1, keepdims=True) + RMS_EPS) * fw_ref[...]
    is_ctx = pl.program_id(0) < dims.t_ctx // tm

    @pl.when(is_ctx)
    def _():
        out_ctx_ref[...] = out

    @pl.when(jnp.logical_not(is_ctx))
    def _():
        out_lat_ref[...] = out


def _ffn_call(dims, x, mod_l, nw, w1, w3, w2, final_nw=None):
    d, t, f = dims.d, dims.t, dims.ffn
    tm = min(512, dims.row_tile)
    nctx = dims.t_ctx // tm
    const = lambda i: (0, 0)
    resident = pl.Buffered(1)
    in_specs = [pl.BlockSpec((tm, d), lambda i: (i, 0)),
                pl.BlockSpec((1, 1, 6 * d), lambda i: (_mod_row(i, tm, dims), 0, 0)),
                pl.BlockSpec((1, d), const),
                pl.BlockSpec((d, f), const, pipeline_mode=resident),
                pl.BlockSpec((d, f), const, pipeline_mode=resident),
                pl.BlockSpec((f, d), const, pipeline_mode=resident)]
    args = [x, mod_l, nw, w1, w3, w2]
    out_specs = pl.BlockSpec((tm, d), lambda i: (i, 0))
    out_shape = jax.ShapeDtypeStruct((t, d), F32)
    if final_nw is not None:
        in_specs.append(pl.BlockSpec((1, d), const))
        args.append(final_nw)
        out_specs = (pl.BlockSpec((tm, d), lambda i: (jnp.minimum(i, nctx - 1), 0)),
                     pl.BlockSpec((tm, d), lambda i: (jnp.maximum(i - nctx, 0), 0)))
        out_shape = (jax.ShapeDtypeStruct((dims.t_ctx, d), F32), jax.ShapeDtypeStruct((dims.t_lat, d), F32))
    return pl.pallas_call(
        functools.partial(_ffn_kernel, dims, tm, final_nw is not None),
        grid=(t // tm,),
        in_specs=in_specs,
        out_specs=out_specs,
        out_shape=out_shape,
        compiler_params=_cparams(("arbitrary",), 4 * tm * d * 4 + 3 * d * f * 2 + 6 * tm * d * 4 + (12 << 20)),
        name="ffn",
    )(*args)


def _forward(dims, x_prompt, x_sample, c, state_ssm, state_hgrn, c_ctx, norm_mix_w, norm_ffn_w, ada_w, ada_b, w_in,
             ssd_conv_w, ssd_conv_b, ssd_dt_bias, ssd_a_log, ssd_d, ssd_norm_w, hgrn_lower_bounds, hgrn_norm_w,
             w_ssd_out, w_hgrn_out, w_out, ffn_w1, ffn_w3, ffn_w2, norm_final_w):
    d, ds_, h = dims.d, dims.d_ssd, dims.h_ssd
    assert dims.db + 1 <= MOD_ROWS and 2 * h <= LANE and dims.nl % HG_C == 0 and dims.dl % 512 == 0
    conv_dim = ds_ + 2 * dims.groups * SSD_N
    x = jnp.concatenate([x_prompt.reshape(dims.t_ctx, d), x_sample.reshape(dims.t_lat, d)], axis=0)
    cvec = jnp.zeros((MOD_ROWS, d), F32).at[0].set(c_ctx).at[1:1 + dims.db].set(c)
    mod = _mod_call(dims, cvec, ada_w, ada_b)
    lb_all = jnp.cumsum(jax.nn.softmax(hgrn_lower_bounds.astype(F32), axis=0), axis=0)
    lb_all = lb_all - lb_all[:1]
    dt_cols = slice(conv_dim + ds_, conv_dim + ds_ + 2 * h)

    s_ssm = s_hg = None
    for l in range(dims.depth):
        mod_l = mod[l].reshape(MOD_ROWS, 1, 6 * d)
        w_l = w_in[l]
        w_a = w_l[:, :dt_cols.start].astype(BF16)
        w_b = w_l[:, dt_cols.stop:].astype(BF16)
        wdt = jnp.pad(w_l[:, dt_cols], ((0, 0), (0, LANE - 2 * h)))
        dtb = jnp.zeros((LANE, 1), F32).at[:2 * h, 0].set(ssd_dt_bias[l].reshape(-1))
        alog = jnp.zeros((LANE, 1), F32).at[:2 * h, 0].set(ssd_a_log[l].reshape(-1))
        lb = jnp.zeros((8, d), F32).at[:2].set(lb_all[l])
        proj, lam, kk, dt_t, da_t = _inproj_call(dims, x, mod_l, norm_mix_w[l].reshape(1, d), w_a, w_b, ssd_conv_w[l],
                                                 ssd_conv_b[l].reshape(1, conv_dim), wdt, dtb, alog, lb)
        dskip = jnp.broadcast_to(jnp.repeat(ssd_d[l], SSD_P)[:, None], (ds_, SSD_Q))
        ssd_nw = jnp.broadcast_to(ssd_norm_w[l][:, None], (ds_, SSD_Q))
        y_ctx, s_ssm = _ssd_call(dims, proj, dt_t, da_t, dskip, ssd_nw, None, dims.nb, dims.nl, 0, l, s_ssm)
        y_lat, _ = _ssd_call(dims, proj, dt_t, da_t, dskip, ssd_nw,
                             state_ssm[:, l].reshape(dims.db, 2, ds_, SSD_N), dims.db, dims.dl, dims.t_ctx)
        hg_nw = hgrn_norm_w[l].reshape(1, d)
        o_ctx, s_hg = _hgrn_call(dims, proj, lam, kk, hg_nw, None, dims.nb, dims.nl, 0, l, s_hg)
        o_lat, _ = _hgrn_call(dims, proj, lam, kk, hg_nw, state_hgrn[:, l], dims.db, dims.dl, dims.t_ctx)
        x = _merge_call(dims, y_ctx, y_lat, o_ctx, o_lat, proj, x, mod_l,
                        w_ssd_out[l].astype(BF16), w_hgrn_out[l].astype(BF16), w_out[l].astype(BF16))
        x = _ffn_call(dims, x, mod_l, norm_ffn_w[l].reshape(1, d),
                      ffn_w1[l].astype(BF16), ffn_w3[l].astype(BF16), ffn_w2[l].astype(BF16),
                      norm_final_w.reshape(1, d) if l == dims.depth - 1 else None)
    y_ctx, y_lat = x
    return (y_ctx.reshape(dims.nb, dims.nl, d), y_lat.reshape(dims.db, dims.dl, d),
            s_ssm.reshape(dims.nb, dims.depth, 2, h, SSD_P, SSD_N), s_hg)


def kernel(x_prompt, x_sample, c, state_ssm, state_hgrn, c_ctx, norm_mix_w, norm_ffn_w, ada_w, ada_b, w_in,
           ssd_conv_w, ssd_conv_b, ssd_dt_bias, ssd_a_log, ssd_d, ssd_norm_w, hgrn_lower_bounds, hgrn_norm_w,
           w_ssd_out, w_hgrn_out, w_out, ffn_w1, ffn_w3, ffn_w2, norm_final_w):
    dims = Dims(d=x_prompt.shape[2], nb=x_prompt.shape[0], nl=x_prompt.shape[1], db=x_sample.shape[0],
                dl=x_sample.shape[1], depth=w_in.shape[0], ffn=ffn_w1.shape[2])
    return _forward(dims, x_prompt, x_sample, c, state_ssm, state_hgrn, c_ctx, norm_mix_w, norm_ffn_w, ada_w, ada_b,
                    w_in, ssd_conv_w, ssd_conv_b, ssd_dt_bias, ssd_a_log, ssd_d, ssd_norm_w, hgrn_lower_bounds,
                    hgrn_norm_w, w_ssd_out, w_hgrn_out, w_out, ffn_w1, ffn_w3, ffn_w2, norm_final_w)
```
